```python
import math
import jax
import jax.numpy as jnp
from jax import lax
import numpy as np

D_MODEL = 1024
BATCH = 1
SEQ = 16384
DEPTH = 4

MEM_LEN = 256
N_EVEN = (DEPTH + 1) // 2
N_ODD = DEPTH // 2
EPS = 1e-6
NEG_INF = -1e30

A_GROUPS = 4
A_CH = 128
A_WIDTH = A_GROUPS * A_CH
A_CHUNK = 128
B_HEADS = 8
B_KV_HEADS = 2
B_Q_PER_KV = B_HEADS // B_KV_HEADS
B_HEAD_DIM = 64
B_WIDTH = B_HEADS * B_HEAD_DIM
B_KV_WIDTH = B_KV_HEADS * B_HEAD_DIM
B_HALF_WINDOW = 128
B_BLOCK = 128
EVEN_IN = 2 * A_WIDTH + B_WIDTH + 2 * B_KV_WIDTH
EVEN_OUT = A_WIDTH + B_WIDTH
C_PAIRS = ((128, 1), (512, 4), (2048, 16))
C_GROUPS = len(C_PAIRS)
C_HEADS = 8
C_HEAD_DIM = 128
C_WIDTH = C_HEADS * C_HEAD_DIM
C_BLOCK = 64
ODD_IN = C_GROUPS * 3 * C_WIDTH
REL_BUCKETS = 32
REL_MAX_DIST = 1024
REL_HEADS = 8
X_HEADS = 4
X_HEAD_DIM = 128
X_WIDTH = X_HEADS * X_HEAD_DIM
MOE_GROUPS = 4
MOE_EPG = 4
MOE_TOP_K = 2
D_EXPERT = 512

kernel_name = 'hybrid_sgu_swa_dilated_hmoe_encoder'


def rms_norm(x, g):
    xf = x.astype(jnp.float32)
    y = xf * lax.rsqrt(jnp.mean(xf * xf, axis=-1, keepdims=True) + EPS)
    return (y * g.astype(jnp.float32)).astype(x.dtype)


def layer_norm(x, g, b):
    xf = x.astype(jnp.float32)
    mu = jnp.mean(xf, axis=-1, keepdims=True)
    var = jnp.mean(jnp.square(xf - mu), axis=-1, keepdims=True)
    y = (xf - mu) * lax.rsqrt(var + EPS)
    return (y * g.astype(jnp.float32) + b.astype(jnp.float32)).astype(x.dtype)


def t5_bucket(rel):
    nb = REL_BUCKETS // 2
    max_exact = nb // 2
    ret = jnp.where(rel > 0, nb, 0)
    n = jnp.abs(rel)
    nf = jnp.maximum(n, 1).astype(jnp.float32)
    large = max_exact + (jnp.log(nf / max_exact) / math.log(REL_MAX_DIST / max_exact)
                         * (nb - max_exact)).astype(jnp.int32)
    large = jnp.minimum(large, nb - 1)
    return ret + jnp.where(n < max_exact, n, large)


def rel_bias(table, rel):
    return jnp.moveaxis(table[t5_bucket(rel)], -1, 0).astype(jnp.float32)


def band_rel(block):
    return jnp.arange(3 * block)[None, :] - block - jnp.arange(block)[:, None]


def banded_attention(q, k, v, half, block, bias, sink=None):
    n, L, kvh, g, e = q.shape
    nb = -(-L // block)
    pad = nb * block - L
    qb = jnp.pad(q, ((0, 0), (0, pad), (0, 0), (0, 0), (0, 0))).reshape(n, nb, block, kvh, g, e)

    def windows(t):
        tp = jnp.pad(t, ((0, 0), (block, block + pad), (0, 0), (0, 0))).reshape(n, nb + 2, block, kvh, e)
        return jnp.concatenate([tp[:, :-2], tp[:, 1:-1], tp[:, 2:]], axis=2)

    kw = windows(k)
    vw = windows(v)
    qpos = jnp.arange(nb * block).reshape(nb, block)
    kpos = jnp.arange(nb)[:, None] * block - block + jnp.arange(3 * block)[None, :]
    rel = kpos[:, None, :] - qpos[:, :, None]
    valid = (jnp.abs(rel) <= half) & (kpos[:, None, :] >= 0) & (kpos[:, None, :] < L)
    logits = jnp.einsum('nbqkge,nbjke->nbkgqj', qb, kw).astype(jnp.float32) * (e ** -0.5)
    logits = logits + bias[None, None]
    logits = jnp.where(valid[None, :, None, None], logits, NEG_INF)
    m = jnp.max(logits, axis=-1)
    if sink is not None:
        s = jnp.broadcast_to(sink.astype(jnp.float32)[None, None, :, :, None], m.shape)
        m = jnp.maximum(m, s)
    p = jnp.exp(logits - m[..., None])
    den = jnp.sum(p, axis=-1)
    if sink is not None:
        den = den + jnp.exp(s - m)
    out = jnp.einsum('nbkgqj,nbjke->nbqkge', p, vw.astype(jnp.float32))
    out = out / jnp.moveaxis(den, -1, 2)[..., None]
    out = out.reshape(n, nb * block, kvh, g, e)[:, :L].astype(q.dtype)
    lse = jnp.moveaxis(m + jnp.log(den), -1, 2).reshape(n, nb * block, kvh, g)[:, :L]
    return out, lse


def spatial_gating(u, v, ln_g, ln_b, w_s, b_s):
    bsz, s, _ = v.shape
    vn = layer_norm(v, ln_g, ln_b).reshape(bsz, s // A_CHUNK, A_CHUNK, A_GROUPS, A_CH)
    mixed = jnp.einsum('gpq,bnqgc->bnpgc', w_s, vn) + jnp.transpose(b_s)[None, None, :, :, None]
    return u * mixed.reshape(bsz, s, A_WIDTH).astype(u.dtype)


def even_mixer(h, w_in, w_out, ln_g, ln_b, w_s, b_s, sink, table):
    bsz, s, _ = h.shape
    z = h @ w_in
    u, va, q, k, v = jnp.split(z, [A_WIDTH, 2 * A_WIDTH, 2 * A_WIDTH + B_WIDTH,
                                   2 * A_WIDTH + B_WIDTH + B_KV_WIDTH], axis=-1)
    ya = spatial_gating(jax.nn.gelu(u), jax.nn.gelu(va), ln_g, ln_b, w_s, b_s)
    q = q.reshape(bsz, s, B_KV_HEADS, B_Q_PER_KV, B_HEAD_DIM)
    k = k.reshape(bsz, s, B_KV_HEADS, B_HEAD_DIM)
    v = v.reshape(bsz, s, B_KV_HEADS, B_HEAD_DIM)
    bias = rel_bias(table, band_rel(B_BLOCK)).reshape(B_KV_HEADS, B_Q_PER_KV, B_BLOCK, 3 * B_BLOCK)
    yb, _ = banded_attention(q, k, v, B_HALF_WINDOW, B_BLOCK, bias,
                             sink.reshape(B_KV_HEADS, B_Q_PER_KV))
    yb = yb.reshape(bsz, s, B_WIDTH)
    return jnp.concatenate([ya, yb], axis=-1) @ w_out


def odd_mixer(h, w_in, w_out, table):
    bsz, s, _ = h.shape
    z = (h @ w_in).reshape(bsz, s, C_GROUPS, 3, C_HEADS, C_HEAD_DIM)
    outs = []
    lses = []
    for gi, (window, dil) in enumerate(C_PAIRS):
        half = window // 2 // dil
        length = s // dil

        def strided(t):
            return t.reshape(bsz, length, dil, C_HEADS, C_HEAD_DIM).transpose(0, 2, 1, 3, 4).reshape(
                bsz * dil, length, C_HEADS, C_HEAD_DIM)

        qs = strided(z[:, :, gi, 0])[:, :, :, None]
        bias = rel_bias(table, band_rel(C_BLOCK) * dil)[:, None]
        o, lse = banded_attention(qs, strided(z[:, :, gi, 1]), strided(z[:, :, gi, 2]), half, C_BLOCK, bias)
        o = o[:, :, :, 0].reshape(bsz, dil, length, C_HEADS, C_HEAD_DIM).transpose(0, 2, 1, 3, 4)
        outs.append(o.reshape(bsz, s, C_HEADS, C_HEAD_DIM))
        lse = lse[..., 0].reshape(bsz, dil, length, C_HEADS).transpose(0, 2, 1, 3)
        lses.append(lse.reshape(bsz, s, C_HEADS))
    wts = jax.nn.softmax(jnp.stack(lses, axis=0), axis=0)
    y = jnp.einsum('gbsh,gbshe->bshe', wts, jnp.stack(outs, axis=0).astype(jnp.float32))
    return y.reshape(bsz, s, C_WIDTH).astype(h.dtype) @ w_out


def cross_attention(h, mem_n, wq, wkv, wo):
    bsz, s, _ = h.shape
    q = (h @ wq).reshape(bsz, s, X_HEADS, X_HEAD_DIM)
    kv = (mem_n @ wkv).reshape(bsz, mem_n.shape[1], 2, X_HEADS, X_HEAD_DIM)
    logits = jnp.einsum('bshe,bmhe->bhsm', q, kv[:, :, 0]).astype(jnp.float32) * (X_HEAD_DIM ** -0.5)
    p = jax.nn.softmax(logits, axis=-1)
    o = jnp.einsum('bhsm,bmhe->bshe', p, kv[:, :, 1].astype(jnp.float32)).astype(h.dtype)
    return o.reshape(bsz, s, X_WIDTH) @ wo


def hier_moe(h, w_grp, b_grp, w_exp, b_exp, w_gate, w_up, w_down):
    bsz, s, d = h.shape
    t = h.reshape(-1, d)
    g_logits = (t @ w_grp).astype(jnp.float32) + b_grp.astype(jnp.float32)
    g_prob = jax.nn.softmax(g_logits, axis=-1)
    grp = jnp.argmax(g_logits, axis=-1)
    g_gate = jnp.take_along_axis(g_prob, grp[:, None], axis=-1)
    e_logits = jnp.einsum('td,dge->tge', t, w_exp).astype(jnp.float32) + b_exp.astype(jnp.float32)
    e_sel = jnp.take_along_axis(e_logits, grp[:, None, None], axis=1)[:, 0]
    top_v, top_i = lax.top_k(e_sel, MOE_TOP_K)
    top_w = jax.nn.softmax(top_v, axis=-1)
    within = jnp.sum(jax.nn.one_hot(top_i, MOE_EPG, dtype=jnp.float32) * top_w[..., None], axis=1)
    gate = jax.nn.one_hot(grp, MOE_GROUPS, dtype=jnp.float32)[:, :, None] * (g_gate * within)[:, None, :]
    y = jnp.zeros((t.shape[0], d), jnp.float32)
    for gi in range(MOE_GROUPS):
        hid = jax.nn.silu(jnp.einsum('td,edf->tef', t, w_gate[gi])) * jnp.einsum('td,edf->tef', t, w_up[gi])
        hid = hid * gate[:, gi, :, None]
        y = y + jnp.einsum('tef,efd->td', hid, w_down[gi])
    return y.reshape(bsz, s, d).astype(h.dtype)


def setup_inputs(seed: int = 0) -> dict:
    key = jax.random.key(seed)
    ks = jax.random.split(key, 27)
    f32 = jnp.float32

    def nrm(k, shape, scale):
        return jax.random.normal(k, shape, f32) * scale

    def gain(k, shape):
        return 1.0 + 0.02 * jax.random.normal(k, shape, f32)

    d = D_MODEL
    return {
        'x': nrm(ks[0], (BATCH, SEQ, d), 1.0),
        'mem': nrm(ks[1], (BATCH, MEM_LEN, d), 1.0),
        'ln_mix': gain(ks[2], (DEPTH, d)),
        'ln_cross': gain(ks[3], (DEPTH, d)),
        'ln_mem': gain(ks[4], (DEPTH, d)),
        'ln_ffn': gain(ks[5], (DEPTH, d)),
        'ln_final': gain(ks[6], (d,)),
        'rel_table': nrm(ks[7], (REL_BUCKETS, REL_HEADS), 0.5),
        'even_w_in': nrm(ks[8], (N_EVEN, d, EVEN_IN), d ** -0.5),
        'even_w_out': nrm(ks[9], (N_EVEN, EVEN_OUT, d), EVEN_OUT ** -0.5),
        'sgu_ln_g': gain(ks[10], (N_EVEN, A_WIDTH)),
        'sgu_ln_b': nrm(ks[11], (N_EVEN, A_WIDTH), 0.02),
        'sgu_w': nrm(ks[12], (N_EVEN, A_GROUPS, A_CHUNK, A_CHUNK), A_CHUNK ** -0.5),
        'sgu_b': 1.0 + nrm(ks[13], (N_EVEN, A_GROUPS, A_CHUNK), 0.02),
        'attn_sink': nrm(ks[14], (N_EVEN, B_HEADS), 0.5),
        'odd_w_in': nrm(ks[15], (N_ODD, d, ODD_IN), d ** -0.5),
        'odd_w_out': nrm(ks[16], (N_ODD, C_WIDTH, d), C_WIDTH ** -0.5),
        'xq_w': nrm(ks[17], (DEPTH, d, X_WIDTH), d ** -0.5),
        'xkv_w': nrm(ks[18], (DEPTH, d, 2 * X_WIDTH), d ** -0.5),
        'xo_w': nrm(ks[19], (DEPTH, X_WIDTH, d), X_WIDTH ** -0.5),
        'router_group_w': nrm(ks[20], (DEPTH, d, MOE_GROUPS), d ** -0.5),
        'router_group_b': nrm(ks[21], (DEPTH, MOE_GROUPS), 0.01),
        'router_expert_w': nrm(ks[22], (DEPTH, d, MOE_GROUPS, MOE_EPG), d ** -0.5),
        'router_expert_b': nrm(ks[23], (DEPTH, MOE_GROUPS, MOE_EPG), 0.01),
        'expert_w_gate': nrm(ks[24], (DEPTH, MOE_GROUPS, MOE_EPG, d, D_EXPERT), d ** -0.5),
        'expert_w_up': nrm(ks[25], (DEPTH, MOE_GROUPS, MOE_EPG, d, D_EXPERT), d ** -0.5),
        'expert_w_down': nrm(ks[26], (DEPTH, MOE_GROUPS, MOE_EPG, D_EXPERT, d), D_EXPERT ** -0.5),
    }


def reference(x, mem, ln_mix, ln_cross, ln_mem, ln_ffn, ln_final, rel_table,
              even_w_in, even_w_out, sgu_ln_g, sgu_ln_b, sgu_w, sgu_b, attn_sink,
              odd_w_in, odd_w_out, xq_w, xkv_w, xo_w,
              router_group_w, router_group_b, router_expert_w, router_expert_b,
              expert_w_gate, expert_w_up, expert_w_down):
    h = x
    for layer in range(DEPTH):
        i = layer // 2
        n = rms_norm(h, ln_mix[layer])
        if layer % 2 == 0:
            mix = even_mixer(n, even_w_in[i], even_w_out[i], sgu_ln_g[i], sgu_ln_b[i],
                             sgu_w[i], sgu_b[i], attn_sink[i], rel_table)
        else:
            mix = odd_mixer(n, odd_w_in[i], odd_w_out[i], rel_table)
        h = h + mix.astype(h.dtype)
        mem_n = rms_norm(mem, ln_mem[layer])
        h = h + cross_attention(rms_norm(h, ln_cross[layer]), mem_n, xq_w[layer], xkv_w[layer], xo_w[layer])
        h = h + hier_moe(rms_norm(h, ln_ffn[layer]), router_group_w[layer], router_group_b[layer],
                         router_expert_w[layer], router_expert_b[layer],
                         expert_w_gate[layer], expert_w_up[layer], expert_w_down[layer])
    return rms_norm(h, ln_final)
```

```python
import functools
import math

import numpy as np
import jax
import jax.numpy as jnp
from jax import lax
from jax.experimental import pallas as pl
from jax.experimental.pallas import tpu as pltpu

F32 = jnp.float32
BF16 = jnp.bfloat16

D_MODEL = 1024
SEQ = 16384
DEPTH = 4
MEM_LEN = 256
EPS = 1e-6
NEG_INF = -1e30

A_GROUPS = 4
A_CH = 128
A_WIDTH = A_GROUPS * A_CH
A_CHUNK = 128
B_HEADS = 8
B_KV_HEADS = 2
B_Q_PER_KV = B_HEADS // B_KV_HEADS
B_HEAD_DIM = 64
B_WIDTH = B_HEADS * B_HEAD_DIM
B_KV_WIDTH = B_KV_HEADS * B_HEAD_DIM
B_HALF_WINDOW = 128
B_BLOCK = 128
EVEN_IN = 2 * A_WIDTH + B_WIDTH + 2 * B_KV_WIDTH
EVEN_Q0 = 2 * A_WIDTH
EVEN_K0 = EVEN_Q0 + B_WIDTH
EVEN_V0 = EVEN_K0 + B_KV_WIDTH

C_PAIRS = ((128, 1), (512, 4), (2048, 16))
C_GROUPS = len(C_PAIRS)
C_HEADS = 8
C_HEAD_DIM = 128
C_WIDTH = C_HEADS * C_HEAD_DIM
C_BLOCK = 64
ODD_IN = C_GROUPS * 3 * C_WIDTH

REL_BUCKETS = 32
REL_MAX_DIST = 1024

X_HEADS = 4
X_HEAD_DIM = 128
X_WIDTH = X_HEADS * X_HEAD_DIM

MOE_GROUPS = 4
MOE_EPG = 4
N_EXPERTS = MOE_GROUPS * MOE_EPG
D_EXPERT = 512
PAIR_LO = (0, 0, 0, 1, 1, 2)
PAIR_HI = (1, 2, 3, 2, 3, 3)
N_PAIRS = len(PAIR_LO)
N_BUCKETS = MOE_GROUPS * N_PAIRS

V7X_LANES = 128
V7X_VMEM_BYTES = 64 * 1024 * 1024
VMEM_LIMIT = 56 * 1024 * 1024

PROJ_TM = 1024
PROJ_TN = 1024
EVEN_TB = 512
ODD_TQ = 512
COMB_TB = 512
CROSS_TB = 512
MOE_TM = 256
META_W = V7X_LANES
ROW_W = D_MODEL + META_W
ROUTER_ROWS = 32
MOE_TILES = (SEQ + N_BUCKETS * (MOE_TM - 1)) // MOE_TM
H_ROWS = SEQ + 2 * MOE_TM


def _cparams(*sem):
    return pltpu.CompilerParams(dimension_semantics=sem, vmem_limit_bytes=VMEM_LIMIT)


def _rms(x, g):
    return x * lax.rsqrt(jnp.mean(x * x, axis=-1, keepdims=True) + EPS) * g


def _dot(a, b):
    return jnp.dot(a, b, preferred_element_type=F32)


def _dot_nt(a, b):
    return lax.dot_general(a, b, (((1,), (1,)), ((), ())), preferred_element_type=F32)


def _proj_kernel(h_ref, g_ref, w_ref, o_ref, xn_ref, *, gelu_cols):
    @pl.when(pl.program_id(1) == 0)
    def _():
        xn_ref[...] = _rms(h_ref[...], g_ref[...]).astype(BF16)

    acc = _dot(xn_ref[...], w_ref[...])
    if gelu_cols:
        o_ref[:, :gelu_cols] = jax.nn.gelu(acc[:, :gelu_cols]).astype(o_ref.dtype)
        o_ref[:, gelu_cols:] = acc[:, gelu_cols:].astype(o_ref.dtype)
    else:
        o_ref[...] = acc.astype(o_ref.dtype)


def _proj(h, g, w, *, rows, tm, tn, gelu_cols=0):
    n = w.shape[1]
    return pl.pallas_call(
        functools.partial(_proj_kernel, gelu_cols=gelu_cols),
        grid=(rows // tm, n // tn),
        in_specs=[
            pl.BlockSpec((tm, D_MODEL), lambda i, j: (i, 0)),
            pl.BlockSpec((1, D_MODEL), lambda i, j: (0, 0)),
            pl.BlockSpec((D_MODEL, tn), lambda i, j: (0, j)),
        ],
        out_specs=pl.BlockSpec((tm, tn), lambda i, j: (i, j)),
        out_shape=jax.ShapeDtypeStruct((rows, n), BF16),
        scratch_shapes=[pltpu.VMEM((tm, D_MODEL), BF16)],
        compiler_params=_cparams("parallel", "arbitrary"),
        name="norm_proj",
    )(h, g.reshape(1, D_MODEL), w)


def _even_mix_kernel(z_ref, kvp_ref, kvn_ref, h_ref, lng_ref, lnb_ref, ws_ref, bs_ref, bias_ref,
                     sink_ref, wout_ref, o_ref, kv_scr, y_scr):
    i = pl.program_id(0)
    nsub = EVEN_TB // B_BLOCK
    nblk = SEQ // B_BLOCK
    kv_scr[0:B_BLOCK] = kvp_ref[...]
    kv_scr[B_BLOCK:B_BLOCK + EVEN_TB] = z_ref[:, EVEN_K0:EVEN_IN]
    kv_scr[B_BLOCK + EVEN_TB:] = kvn_ref[...]
    lng = lng_ref[...]
    lnb = lnb_ref[...]
    for s in range(nsub):
        r0 = s * B_BLOCK
        gb = i * nsub + s
        sel = jnp.where(gb == 0, 0, jnp.where(gb == nblk - 1, 2, 1))
        u = z_ref[r0:r0 + A_CHUNK, 0:A_WIDTH].astype(F32)
        va = z_ref[r0:r0 + A_CHUNK, A_WIDTH:2 * A_WIDTH].astype(F32)
        mu = jnp.mean(va, axis=-1, keepdims=True)
        vc = va - mu
        var = jnp.mean(vc * vc, axis=-1, keepdims=True)
        vn = (vc * lax.rsqrt(var + EPS) * lng + lnb).astype(BF16)
        for g in range(A_GROUPS):
            c0 = g * A_CH
            mixed = _dot(ws_ref[g], vn[:, c0:c0 + A_CH]) + bs_ref[g]
            y_scr[r0:r0 + A_CHUNK, c0:c0 + A_CH] = (u[:, c0:c0 + A_CH] * mixed).astype(BF16)
        for kh in range(B_KV_HEADS):
            kw = kv_scr[r0:r0 + 3 * B_BLOCK, kh * B_HEAD_DIM:(kh + 1) * B_HEAD_DIM]
            vw = kv_scr[r0:r0 + 3 * B_BLOCK,
                        B_KV_WIDTH + kh * B_HEAD_DIM:B_KV_WIDTH + (kh + 1) * B_HEAD_DIM]
            for g in range(B_Q_PER_KV):
                hd = kh * B_Q_PER_KV + g
                q = z_ref[r0:r0 + B_BLOCK, EVEN_Q0 + hd * B_HEAD_DIM:EVEN_Q0 + (hd + 1) * B_HEAD_DIM]
                lg = _dot_nt(q, kw) * (B_HEAD_DIM ** -0.5) + bias_ref[sel, hd]
                sk = sink_ref[hd]
                m = jnp.maximum(jnp.max(lg, axis=-1, keepdims=True), sk)
                p = jnp.exp(lg - m)
                den = jnp.sum(p, axis=-1, keepdims=True) + jnp.exp(sk - m)
                o = _dot(p.astype(BF16), vw) / den
                y_scr[r0:r0 + B_BLOCK,
                      A_WIDTH + hd * B_HEAD_DIM:A_WIDTH + (hd + 1) * B_HEAD_DIM] = o.astype(BF16)
    o_ref[...] = h_ref[...] + _dot(y_scr[...], wout_ref[...])


def _even_mix(z, h, ln_g, ln_b, w_s, b_s, bias, sink, w_out):
    nsub = EVEN_TB // B_BLOCK
    nblk = SEQ // B_BLOCK
    kv_cb = EVEN_K0 // (2 * B_KV_WIDTH)
    return pl.pallas_call(
        _even_mix_kernel,
        grid=(SEQ // EVEN_TB,),
        in_specs=[
            pl.BlockSpec((EVEN_TB, EVEN_IN), lambda i: (i, 0)),
            pl.BlockSpec((B_BLOCK, 2 * B_KV_WIDTH), lambda i: (jnp.maximum(i * nsub - 1, 0), kv_cb)),
            pl.BlockSpec((B_BLOCK, 2 * B_KV_WIDTH),
                         lambda i: (jnp.minimum((i + 1) * nsub, nblk - 1), kv_cb)),
            pl.BlockSpec((EVEN_TB, D_MODEL), lambda i: (i, 0)),
            pl.BlockSpec((1, A_WIDTH), lambda i: (0, 0)),
            pl.BlockSpec((1, A_WIDTH), lambda i: (0, 0)),
            pl.BlockSpec((A_GROUPS, A_CHUNK, A_CHUNK), lambda i: (0, 0, 0)),
            pl.BlockSpec((A_GROUPS, A_CHUNK, A_CH), lambda i: (0, 0, 0)),
            pl.BlockSpec((3, B_HEADS, B_BLOCK, 3 * B_BLOCK), lambda i: (0, 0, 0, 0)),
            pl.BlockSpec(memory_space=pltpu.SMEM),
            pl.BlockSpec((A_WIDTH + B_WIDTH, D_MODEL), lambda i: (0, 0)),
        ],
        out_specs=pl.BlockSpec((EVEN_TB, D_MODEL), lambda i: (i, 0)),
        out_shape=jax.ShapeDtypeStruct((SEQ, D_MODEL), F32),
        scratch_shapes=[
            pltpu.VMEM((EVEN_TB + 2 * B_BLOCK, 2 * B_KV_WIDTH), BF16),
            pltpu.VMEM((EVEN_TB, A_WIDTH + B_WIDTH), BF16),
        ],
        compiler_params=_cparams("parallel"),
        name="even_mixer",
    )(z, z, z, h, ln_g.reshape(1, A_WIDTH), ln_b.reshape(1, A_WIDTH), w_s.astype(BF16),
      jnp.broadcast_to(b_s[:, :, None], (A_GROUPS, A_CHUNK, A_CH)), bias, sink, w_out)


def _dil_attn_kernel(q_ref, kp_ref, kc_ref, kn_ref, vp_ref, vc_ref, vn_ref, bias_ref,
                     o_ref, lse_ref, k_scr, v_scr, *, length):
    b = pl.program_id(1)
    nsub = ODD_TQ // C_BLOCK
    nblk = length // C_BLOCK
    k_scr[0:C_BLOCK] = kp_ref[...]
    k_scr[C_BLOCK:C_BLOCK + ODD_TQ] = kc_ref[...]
    k_scr[C_BLOCK + ODD_TQ:] = kn_ref[...]
    v_scr[0:C_BLOCK] = vp_ref[...]
    v_scr[C_BLOCK:C_BLOCK + ODD_TQ] = vc_ref[...]
    v_scr[C_BLOCK + ODD_TQ:] = vn_ref[...]
    lane = lax.broadcasted_iota(jnp.int32, (C_BLOCK, V7X_LANES), 1)

    def body(s, carry):
        r0 = pl.multiple_of(s * C_BLOCK, C_BLOCK)
        gb = b * nsub + s
        sel = jnp.where(gb == 0, 0, jnp.where(gb == nblk - 1, 2, 1))
        lse_tile = jnp.zeros((C_BLOCK, V7X_LANES), F32)
        for hd in range(C_HEADS):
            c0 = hd * C_HEAD_DIM
            q = q_ref[pl.ds(r0, C_BLOCK), c0:c0 + C_HEAD_DIM]
            kw = k_scr[pl.ds(r0, 3 * C_BLOCK), c0:c0 + C_HEAD_DIM]
            vw = v_scr[pl.ds(r0, 3 * C_BLOCK), c0:c0 + C_HEAD_DIM]
            lg = _dot_nt(q, kw) * (C_HEAD_DIM ** -0.5) + bias_ref[sel, hd]
            m = jnp.max(lg, axis=-1, keepdims=True)
            p = jnp.exp(lg - m)
            den = jnp.sum(p, axis=-1, keepdims=True)
            o = _dot(p.astype(BF16), vw) / den
            o_ref[pl.ds(r0, C_BLOCK), c0:c0 + C_HEAD_DIM] = o.astype(o_ref.dtype)
            lse_tile = jnp.where(lane == hd, m + jnp.log(den), lse_tile)
        lse_ref[pl.ds(r0, C_BLOCK), :] = lse_tile
        return carry

    lax.fori_loop(0, nsub, body, 0)


def _dil_attn(z, bias, gi, dil):
    length = SEQ // dil
    zv = z.reshape(length, dil * ODD_IN)
    nsub = ODD_TQ // C_BLOCK
    nblk = length // C_BLOCK
    cb = ODD_IN // C_WIDTH

    def cur(j):
        return pl.BlockSpec((ODD_TQ, C_WIDTH), lambda r, b: (b, r * cb + gi * 3 + j))

    def prev(j):
        return pl.BlockSpec((C_BLOCK, C_WIDTH),
                            lambda r, b: (jnp.maximum(b * nsub - 1, 0), r * cb + gi * 3 + j))

    def nxt(j):
        return pl.BlockSpec((C_BLOCK, C_WIDTH),
                            lambda r, b: (jnp.minimum((b + 1) * nsub, nblk - 1), r * cb + gi * 3 + j))

    o, lse = pl.pallas_call(
        functools.partial(_dil_attn_kernel, length=length),
        grid=(dil, length // ODD_TQ),
        in_specs=[cur(0), prev(1), cur(1), nxt(1), prev(2), cur(2), nxt(2),
                  pl.BlockSpec((3, C_HEADS, C_BLOCK, 3 * C_BLOCK), lambda r, b: (0, 0, 0, 0))],
        out_specs=[pl.BlockSpec((ODD_TQ, C_WIDTH), lambda r, b: (b, r)),
                   pl.BlockSpec((ODD_TQ, V7X_LANES), lambda r, b: (b, r))],
        out_shape=[jax.ShapeDtypeStruct((length, dil * C_WIDTH), BF16),
                   jax.ShapeDtypeStruct((length, dil * V7X_LANES), F32)],
        scratch_shapes=[pltpu.VMEM((ODD_TQ + 2 * C_BLOCK, C_WIDTH), BF16),
                        pltpu.VMEM((ODD_TQ + 2 * C_BLOCK, C_WIDTH), BF16)],
        compiler_params=_cparams("parallel", "parallel"),
        name=f"dilated_attn_d{dil}",
    )(zv, zv, zv, zv, zv, zv, zv, bias)
    return o.reshape(SEQ, C_WIDTH), lse.reshape(SEQ, V7X_LANES)


def _combine_kernel(o0_ref, o1_ref, o2_ref, l0_ref, l1_ref, l2_ref, h_ref, wout_ref, out_ref, y_scr):
    l0 = l0_ref[...]
    l1 = l1_ref[...]
    l2 = l2_ref[...]
    m = jnp.maximum(jnp.maximum(l0, l1), l2)
    e0 = jnp.exp(l0 - m)
    e1 = jnp.exp(l1 - m)
    e2 = jnp.exp(l2 - m)
    tot = e0 + e1 + e2
    w0 = e0 / tot
    w1 = e1 / tot
    w2 = e2 / tot
    for hd in range(C_HEADS):
        c0 = hd * C_HEAD_DIM
        y = (w0[:, hd:hd + 1] * o0_ref[:, c0:c0 + C_HEAD_DIM].astype(F32)
             + w1[:, hd:hd + 1] * o1_ref[:, c0:c0 + C_HEAD_DIM].astype(F32)
             + w2[:, hd:hd + 1] * o2_ref[:, c0:c0 + C_HEAD_DIM].astype(F32))
        y_scr[:, c0:c0 + C_HEAD_DIM] = y.astype(BF16)
    out_ref[...] = h_ref[...] + _dot(y_scr[...], wout_ref[...])


def _combine(outs, lses, h, w_out):
    blk_o = pl.BlockSpec((COMB_TB, C_WIDTH), lambda i: (i, 0))
    blk_l = pl.BlockSpec((COMB_TB, V7X_LANES), lambda i: (i, 0))
    return pl.pallas_call(
        _combine_kernel,
        grid=(SEQ // COMB_TB,),
        in_specs=[blk_o, blk_o, blk_o, blk_l, blk_l, blk_l,
                  pl.BlockSpec((COMB_TB, D_MODEL), lambda i: (i, 0)),
                  pl.BlockSpec((C_WIDTH, D_MODEL), lambda i: (0, 0))],
        out_specs=pl.BlockSpec((COMB_TB, D_MODEL), lambda i: (i, 0)),
        out_shape=jax.ShapeDtypeStruct((SEQ, D_MODEL), F32),
        scratch_shapes=[pltpu.VMEM((COMB_TB, C_WIDTH), BF16)],
        compiler_params=_cparams("parallel"),
        name="group_combine_proj",
    )(*outs, *lses, h, w_out)


def _cross_kernel(h_ref, gx_ref, wq_ref, kv_ref, wo_ref, gf_ref, wr_ref, br_ref,
                  hx_ref, meta_ref, o_scr):
    h = h_ref[...]
    q = _dot(_rms(h, gx_ref[...]).astype(BF16), wq_ref[...]).astype(BF16)
    for hd in range(X_HEADS):
        c0 = hd * X_HEAD_DIM
        lg = _dot_nt(q[:, c0:c0 + X_HEAD_DIM], kv_ref[:, c0:c0 + X_HEAD_DIM]) * (X_HEAD_DIM ** -0.5)
        m = jnp.max(lg, axis=-1, keepdims=True)
        p = jnp.exp(lg - m)
        den = jnp.sum(p, axis=-1, keepdims=True)
        o = _dot(p.astype(BF16), kv_ref[:, X_WIDTH + c0:X_WIDTH + c0 + X_HEAD_DIM]) / den
        o_scr[:, c0:c0 + X_HEAD_DIM] = o.astype(BF16)
    h2 = h + _dot(o_scr[...], wo_ref[...])
    hx_ref[:, :D_MODEL] = h2

    t = _rms(h2, gf_ref[...])
    lt = lax.dot_general(wr_ref[...], t, (((1,), (1,)), ((), ())), preferred_element_type=F32,
                         precision=lax.Precision.HIGHEST) + br_ref[...]
    g = [lt[k:k + 1, :] for k in range(MOE_GROUPS)]
    gmax = jnp.maximum(jnp.maximum(g[0], g[1]), jnp.maximum(g[2], g[3]))
    grp = jnp.where(g[0] == gmax, 0, jnp.where(g[1] == gmax, 1, jnp.where(g[2] == gmax, 2, 3)))
    g_gate = 1.0 / (jnp.exp(g[0] - gmax) + jnp.exp(g[1] - gmax) + jnp.exp(g[2] - gmax)
                    + jnp.exp(g[3] - gmax))
    e = []
    for k in range(MOE_EPG):
        rows = [lt[MOE_GROUPS + gi * MOE_EPG + k:MOE_GROUPS + gi * MOE_EPG + k + 1, :]
                for gi in range(MOE_GROUPS)]
        e.append(jnp.where(grp == 0, rows[0], jnp.where(grp == 1, rows[1],
                                                         jnp.where(grp == 2, rows[2], rows[3]))))
    v1 = jnp.maximum(jnp.maximum(e[0], e[1]), jnp.maximum(e[2], e[3]))
    i1 = jnp.where(e[0] == v1, 0, jnp.where(e[1] == v1, 1, jnp.where(e[2] == v1, 2, 3)))
    r = [jnp.where(i1 == k, -jnp.inf, e[k]) for k in range(MOE_EPG)]
    v2 = jnp.maximum(jnp.maximum(r[0], r[1]), jnp.maximum(r[2], r[3]))
    i2 = jnp.where(r[0] == v2, 0, jnp.where(r[1] == v2, 1, jnp.where(r[2] == v2, 2, 3)))
    d = jnp.exp(v2 - v1)
    w1 = g_gate / (1.0 + d)
    w2 = g_gate * d / (1.0 + d)
    first_lo = i1 < i2
    lo = jnp.where(first_lo, i1, i2)
    hi = jnp.where(first_lo, i2, i1)
    w_lo = jnp.where(first_lo, w1, w2)
    w_hi = jnp.where(first_lo, w2, w1)
    pair = jnp.where(lo == 0, hi - 1, jnp.where(lo == 1, hi + 1, 5))
    bucket = (grp * N_PAIRS + pair).astype(F32)
    row = lax.broadcasted_iota(jnp.int32, (META_W, CROSS_TB), 0)
    meta = jnp.where(row == 0, bucket, jnp.where(row == 1, w_lo, jnp.where(row == 2, w_hi, 0.0)))
    meta_ref[...] = meta[0:8, :]
    hx_ref[:, D_MODEL:] = jnp.transpose(meta)


def _cross_router(h, g_cross, wq, kv, wo, g_ffn, wr_t, br):
    full = lambda shape: pl.BlockSpec(shape, lambda i: tuple(0 for _ in shape))
    return pl.pallas_call(
        _cross_kernel,
        grid=(SEQ // CROSS_TB,),
        in_specs=[
            pl.BlockSpec((CROSS_TB, D_MODEL), lambda i: (i, 0)),
            full((1, D_MODEL)),
            full((D_MODEL, X_WIDTH)),
            full((MEM_LEN, 2 * X_WIDTH)),
            full((X_WIDTH, D_MODEL)),
            full((1, D_MODEL)),
            full((ROUTER_ROWS, D_MODEL)),
            full((ROUTER_ROWS, 1)),
        ],
        out_specs=[pl.BlockSpec((CROSS_TB, ROW_W), lambda i: (i, 0)),
                   pl.BlockSpec((8, CROSS_TB), lambda i: (0, i))],
        out_shape=[jax.ShapeDtypeStruct((SEQ, ROW_W), F32),
                   jax.ShapeDtypeStruct((8, SEQ), F32)],
        scratch_shapes=[pltpu.VMEM((CROSS_TB, X_WIDTH), BF16)],
        compiler_params=_cparams("parallel"),
        name="cross_attn_router",
    )(h, g_cross.reshape(1, D_MODEL), wq, kv, wo, g_ffn.reshape(1, D_MODEL), wr_t, br)


def _moe_kernel(src_ref, dst_ref, ea_ref, eb_ref, nused_ref,
                hx_hbm, gf_ref, wga_ref, wua_ref, wda_ref, wgb_ref, wub_ref, wdb_ref,
                out_hbm, xbuf, obuf, gsem, ssem):
    k = pl.program_id(0)
    nt = pl.num_programs(0)
    slot = k % 2

    def gather(tile, sl):
        base = tile * MOE_TM

        def body(r, c):
            pltpu.make_async_copy(hx_hbm.at[pl.ds(src_ref[base + r], 1)],
                                  xbuf.at[sl, pl.ds(r, 1)], gsem.at[sl]).start()
            return c

        lax.fori_loop(0, MOE_TM, body, 0, unroll=8)

    def scatter(tile, sl):
        base = tile * MOE_TM

        def body(r, c):
            pltpu.make_async_copy(obuf.at[sl, pl.ds(r, 1)],
                                  out_hbm.at[pl.ds(dst_ref[base + r], 1)], ssem.at[sl]).start()
            return c

        lax.fori_loop(0, MOE_TM, body, 0, unroll=8)

    def wait_gather(sl):
        pltpu.make_async_copy(hx_hbm.at[pl.ds(0, MOE_TM)], xbuf.at[sl], gsem.at[sl]).wait()

    def wait_scatter(sl):
        pltpu.make_async_copy(obuf.at[sl], out_hbm.at[pl.ds(0, MOE_TM)], ssem.at[sl]).wait()

    @pl.when(k == 0)
    def _():
        gather(0, 0)

    @pl.when(k + 1 < nt)
    def _():
        gather(k + 1, 1 - slot)

    wait_gather(slot)

    @pl.when(k >= 2)
    def _():
        wait_scatter(slot)

    @pl.when(k < nused_ref[0])
    def _():
        x = xbuf[slot]
        h2 = x[:, :D_MODEL]
        t = _rms(h2, gf_ref[...]).astype(BF16)
        y = jnp.zeros((MOE_TM, D_MODEL), F32)
        for col, wg, wu, wd in ((1, wga_ref, wua_ref, wda_ref), (2, wgb_ref, wub_ref, wdb_ref)):
            gate = x[:, D_MODEL + col:D_MODEL + col + 1]
            hid = jax.nn.silu(_dot(t, wg[0])) * _dot(t, wu[0]) * gate
            y = y + _dot(hid.astype(BF16), wd[0])
        obuf[slot] = h2 + y

    @pl.when(k >= nused_ref[0])
    def _():
        obuf[slot] = jnp.zeros((MOE_TM, D_MODEL), F32)

    scatter(k, slot)

    @pl.when(k == nt - 1)
    def _():
        wait_scatter(1 - slot)
        wait_scatter(slot)


def _moe(hx, g_ffn, w_gate, w_up, w_down, src, dst, ea, eb, nused):
    def wspec(shape, which):
        if which == 0:
            return pl.BlockSpec((1,) + shape, lambda k, s, d, a, b, n: (a[k], 0, 0))
        return pl.BlockSpec((1,) + shape, lambda k, s, d, a, b, n: (b[k], 0, 0))

    up_shape = (D_MODEL, D_EXPERT)
    down_shape = (D_EXPERT, D_MODEL)
    grid_spec = pltpu.PrefetchScalarGridSpec(
        num_scalar_prefetch=5,
        grid=(MOE_TILES,),
        in_specs=[
            pl.BlockSpec(memory_space=pl.ANY),
            pl.BlockSpec((1, D_MODEL), lambda k, s, d, a, b, n: (0, 0)),
            wspec(up_shape, 0), wspec(up_shape, 0), wspec(down_shape, 0),
            wspec(up_shape, 1), wspec(up_shape, 1), wspec(down_shape, 1),
        ],
        out_specs=pl.BlockSpec(memory_space=pl.ANY),
        scratch_shapes=[
            pltpu.VMEM((2, MOE_TM, ROW_W), F32),
            pltpu.VMEM((2, MOE_TM, D_MODEL), F32),
            pltpu.SemaphoreType.DMA((2,)),
            pltpu.SemaphoreType.DMA((2,)),
        ],
    )
    return pl.pallas_call(
        _moe_kernel,
        grid_spec=grid_spec,
        out_shape=jax.ShapeDtypeStruct((H_ROWS, D_MODEL), F32),
        compiler_params=_cparams("arbitrary"),
        name="routed_moe",
    )(src, dst, ea, eb, nused, hx, g_ffn.reshape(1, D_MODEL),
      w_gate, w_up, w_down, w_gate, w_up, w_down)


def _route_tables(bucket):
    order = jnp.argsort(bucket, stable=True).astype(jnp.int32)
    counts = jnp.sum((bucket[:, None] == jnp.arange(N_BUCKETS, dtype=jnp.int32)[None, :])
                     .astype(jnp.int32), axis=0)
    ntile = (counts + MOE_TM - 1) // MOE_TM
    tile_end = jnp.cumsum(ntile)
    tile_start = tile_end - ntile
    cstart = jnp.cumsum(counts) - counts
    nused = tile_end[-1]
    tiles = jnp.arange(MOE_TILES, dtype=jnp.int32)
    tile_bucket = jnp.minimum(jnp.sum((tiles[:, None] >= tile_end[None, :]).astype(jnp.int32), axis=1),
                              N_BUCKETS - 1)
    slot = jnp.arange(MOE_TILES * MOE_TM, dtype=jnp.int32)
    st = slot // MOE_TM
    sb = tile_bucket[st]
    off = slot - tile_start[sb] * MOE_TM
    valid = (off < counts[sb]) & (st < nused)
    tok = order[jnp.clip(cstart[sb] + off, 0, SEQ - 1)]
    src = jnp.where(valid, tok, 0).astype(jnp.int32)
    dump = SEQ + (st % 2) * MOE_TM + slot % MOE_TM
    dst = jnp.where(valid, tok, dump).astype(jnp.int32)
    grp = tile_bucket // N_PAIRS
    pair = tile_bucket % N_PAIRS
    ea = grp * MOE_EPG + jnp.asarray(PAIR_LO, jnp.int32)[pair]
    eb = grp * MOE_EPG + jnp.asarray(PAIR_HI, jnp.int32)[pair]
    return src, dst, ea.astype(jnp.int32), eb.astype(jnp.int32), nused.reshape(1).astype(jnp.int32)


def _final_norm_kernel(h_ref, g_ref, o_ref):
    o_ref[...] = _rms(h_ref[...], g_ref[...])


def _final_norm(h, g):
    tb = PROJ_TM
    return pl.pallas_call(
        _final_norm_kernel,
        grid=(SEQ // tb,),
        in_specs=[pl.BlockSpec((tb, D_MODEL), lambda i: (i, 0)),
                  pl.BlockSpec((1, D_MODEL), lambda i: (0, 0))],
        out_specs=pl.BlockSpec((tb, D_MODEL), lambda i: (i, 0)),
        out_shape=jax.ShapeDtypeStruct((SEQ, D_MODEL), F32),
        compiler_params=_cparams("parallel"),
        name="final_norm",
    )(h, g.reshape(1, D_MODEL))


def _t5_bucket_np(rel):
    nb = REL_BUCKETS // 2
    max_exact = nb // 2
    ret = np.where(rel > 0, nb, 0)
    n = np.abs(rel)
    nf = np.maximum(n, 1).astype(np.float32)
    large = max_exact + (np.log(nf / np.float32(max_exact)) / np.float32(math.log(REL_MAX_DIST / max_exact))
                         * np.float32(nb - max_exact)).astype(np.int32)
    large = np.minimum(large, nb - 1)
    return (ret + np.where(n < max_exact, n, large)).astype(np.int32)


def _band_bias(table, block, half, dil):
    rel = np.arange(3 * block)[None, :] - block - np.arange(block)[:, None]
    bias = jnp.moveaxis(table[_t5_bucket_np(rel * dil)], -1, 0).astype(F32)
    band = np.abs(rel) <= half
    col = np.arange(3 * block)[None, :]
    masks = np.stack([band & (col >= block), band, band & (col < 2 * block)])
    add = np.where(masks, 0.0, NEG_INF).astype(np.float32)
    return bias[None] + jnp.asarray(add)[:, None]


def kernel(x, mem, ln_mix, ln_cross, ln_mem, ln_ffn, ln_final, rel_table, even_w_in, even_w_out,
           sgu_ln_g, sgu_ln_b, sgu_w, sgu_b, attn_sink, odd_w_in, odd_w_out, xq_w, xkv_w, xo_w,
           router_group_w, router_group_b, router_expert_w, router_expert_b,
           expert_w_gate, expert_w_up, expert_w_down):
    h = x.reshape(SEQ, D_MODEL)
    mem2 = mem.reshape(MEM_LEN, D_MODEL)
    bias_even = _band_bias(rel_table, B_BLOCK, B_HALF_WINDOW, 1)
    bias_odd = [_band_bias(rel_table, C_BLOCK, window // 2 // dil, dil) for window, dil in C_PAIRS]

    for layer in range(DEPTH):
        i = layer // 2
        if layer % 2 == 0:
            z = _proj(h, ln_mix[layer], even_w_in[i].astype(BF16), rows=SEQ, tm=EVEN_TB, tn=EVEN_IN,
                      gelu_cols=2 * A_WIDTH)
            h = _even_mix(z, h, sgu_ln_g[i], sgu_ln_b[i], sgu_w[i], sgu_b[i], bias_even,
                          attn_sink[i], even_w_out[i].astype(BF16))
        else:
            z = _proj(h, ln_mix[layer], odd_w_in[i].astype(BF16), rows=SEQ, tm=PROJ_TM, tn=PROJ_TN)
            outs, lses = [], []
            for gi, (_, dil) in enumerate(C_PAIRS):
                o, lse = _dil_attn(z, bias_odd[gi], gi, dil)
                outs.append(o)
                lses.append(lse)
            h = _combine(outs, lses, h, odd_w_out[i].astype(BF16))

        kv = _proj(mem2, ln_mem[layer], xkv_w[layer].astype(BF16), rows=MEM_LEN, tm=MEM_LEN,
                   tn=2 * X_WIDTH)
        wr_t = jnp.zeros((ROUTER_ROWS, D_MODEL), F32)
        wr_t = wr_t.at[:MOE_GROUPS].set(router_group_w[layer].T)
        wr_t = wr_t.at[MOE_GROUPS:MOE_GROUPS + N_EXPERTS].set(
            router_expert_w[layer].reshape(D_MODEL, N_EXPERTS).T)
        br = jnp.zeros((ROUTER_ROWS, 1), F32)
        br = br.at[:MOE_GROUPS, 0].set(router_group_b[layer])
        br = br.at[MOE_GROUPS:MOE_GROUPS + N_EXPERTS, 0].set(router_expert_b[layer].reshape(N_EXPERTS))
        hx, meta = _cross_router(h, ln_cross[layer], xq_w[layer].astype(BF16), kv,
                                 xo_w[layer].astype(BF16), ln_ffn[layer], wr_t, br)

        src, dst, ea, eb, nused = _route_tables(meta[0].astype(jnp.int32))
        h = _moe(hx, ln_ffn[layer],
                 expert_w_gate[layer].reshape(N_EXPERTS, D_MODEL, D_EXPERT).astype(BF16),
                 expert_w_up[layer].reshape(N_EXPERTS, D_MODEL, D_EXPERT).astype(BF16),
                 expert_w_down[layer].reshape(N_EXPERTS, D_EXPERT, D_MODEL).astype(BF16),
                 src, dst, ea, eb, nused)

    return _final_norm(h, ln_final).reshape(1, SEQ, D_MODEL)
```

```python
import functools
import math

import numpy as np
import jax
import jax.numpy as jnp
from jax import lax
from jax.experimental import pallas as pl
from jax.experimental.pallas import tpu as pltpu

F32 = jnp.float32
BF16 = jnp.bfloat16

D_MODEL = 1024
SEQ = 16384
DEPTH = 4
MEM_LEN = 256
EPS = 1e-6
NEG_INF = -1e30

A_GROUPS = 4
A_CH = 128
A_WIDTH = A_GROUPS * A_CH
A_CHUNK = 128
B_HEADS = 8
B_KV_HEADS = 2
B_Q_PER_KV = B_HEADS // B_KV_HEADS
B_HEAD_DIM = 64
B_WIDTH = B_HEADS * B_HEAD_DIM
B_KV_WIDTH = B_KV_HEADS * B_HEAD_DIM
B_HALF_WINDOW = 128
B_BLOCK = 128
EVEN_IN = 2 * A_WIDTH + B_WIDTH + 2 * B_KV_WIDTH
EVEN_Q0 = 2 * A_WIDTH
EVEN_K0 = EVEN_Q0 + B_WIDTH
EVEN_V0 = EVEN_K0 + B_KV_WIDTH

C_PAIRS = ((128, 1), (512, 4), (2048, 16))
C_GROUPS = len(C_PAIRS)
C_HEADS = 8
C_HEAD_DIM = 128
C_WIDTH = C_HEADS * C_HEAD_DIM
C_BLOCK = 64
ODD_IN = C_GROUPS * 3 * C_WIDTH

REL_BUCKETS = 32
REL_MAX_DIST = 1024
REL_HEADS = 8

X_HEADS = 4
X_HEAD_DIM = 128
X_WIDTH = X_HEADS * X_HEAD_DIM

MOE_GROUPS = 4
MOE_EPG = 4
N_EXPERTS = MOE_GROUPS * MOE_EPG
D_EXPERT = 512
SLOT_A = (0, 0, 0, 1, 1, 3)
SLOT_B = (1, 2, 3, 3, 2, 2)
N_PAIRS = len(SLOT_A)
N_BUCKETS = MOE_GROUPS * N_PAIRS

V7X_LANES = 128
V7X_VMEM_BYTES = 64 * 1024 * 1024
VMEM_LIMIT = 56 * 1024 * 1024

PROJ_TM = 1024
PROJ_TN = 1024
EVEN_TB = 512
ODD_TILE = PROJ_TM
ODD_BLOCKS = ODD_TILE // C_BLOCK
COMB_TB = 512
CROSS_TB = 512
MOE_TM = 256
META_W = V7X_LANES
ROW_W = D_MODEL + META_W
ROUTER_ROWS = 32
MOE_TILES = (SEQ + N_BUCKETS * (MOE_TM - 1)) // MOE_TM
H_ROWS = SEQ + 2 * MOE_TM


def _cparams(*sem):
    return pltpu.CompilerParams(dimension_semantics=sem, vmem_limit_bytes=VMEM_LIMIT)


def _rms(x, g):
    return x * lax.rsqrt(jnp.mean(x * x, axis=-1, keepdims=True) + EPS) * g


def _dot(a, b):
    return jnp.dot(a, b, preferred_element_type=F32)


def _dot_nt(a, b):
    return lax.dot_general(a, b, (((1,), (1,)), ((), ())), preferred_element_type=F32)


def _proj_kernel(h_ref, g_ref, w_ref, o_ref, xn_ref, *scratch, gelu_cols, dil):
    tm = xn_ref.shape[0]
    seg = tm // dil

    @pl.when(pl.program_id(1) == 0)
    def _():
        xf = _rms(h_ref[...], g_ref[...])
        if dil == 1:
            xn_ref[...] = xf.astype(BF16)
        else:
            xs_ref, = scratch
            for c in range(D_MODEL // V7X_LANES):
                xs_ref[c] = xf[:, c * V7X_LANES:(c + 1) * V7X_LANES]
            for r in range(dil):
                for c in range(D_MODEL // V7X_LANES):
                    xn_ref[r * seg:(r + 1) * seg, c * V7X_LANES:(c + 1) * V7X_LANES] = (
                        xs_ref[c, pl.ds(r, seg, stride=dil), :].astype(BF16))

    acc = _dot(xn_ref[...], w_ref[...])
    if gelu_cols:
        o_ref[0, :, :gelu_cols] = jax.nn.gelu(acc[:, :gelu_cols]).astype(o_ref.dtype)
        o_ref[0, :, gelu_cols:] = acc[:, gelu_cols:].astype(o_ref.dtype)
    else:
        for r in range(dil):
            o_ref[r] = acc[r * seg:(r + 1) * seg].astype(o_ref.dtype)


def _proj(h, g, w, *, rows, tm, tn, gelu_cols=0, dil=1):
    n = w.shape[1]
    seg = tm // dil
    return pl.pallas_call(
        functools.partial(_proj_kernel, gelu_cols=gelu_cols, dil=dil),
        grid=(rows // tm, n // tn),
        in_specs=[
            pl.BlockSpec((tm, D_MODEL), lambda i, j: (i, 0)),
            pl.BlockSpec((1, D_MODEL), lambda i, j: (0, 0)),
            pl.BlockSpec((D_MODEL, tn), lambda i, j: (0, j)),
        ],
        out_specs=pl.BlockSpec((dil, seg, tn), lambda i, j: (0, i, j)),
        out_shape=jax.ShapeDtypeStruct((dil, rows // dil, n), BF16),
        scratch_shapes=[pltpu.VMEM((tm, D_MODEL), BF16)] + (
            [pltpu.VMEM((D_MODEL // V7X_LANES, tm, V7X_LANES), F32)] if dil > 1 else []),
        compiler_params=_cparams("parallel", "arbitrary"),
        name=f"norm_proj_d{dil}",
    )(h, g.reshape(1, D_MODEL), w)


def _t5_bucket_np(rel):
    nb = REL_BUCKETS // 2
    max_exact = nb // 2
    ret = np.where(rel > 0, nb, 0)
    n = np.abs(rel)
    nf = np.maximum(n, 1).astype(np.float32)
    large = max_exact + (np.log(nf / np.float32(max_exact)) / np.float32(math.log(REL_MAX_DIST / max_exact))
                         * np.float32(nb - max_exact)).astype(np.int32)
    large = np.minimum(large, nb - 1)
    return (ret + np.where(n < max_exact, n, large)).astype(np.int32)


def _bias_kernel(table_ref, idx_ref, mask_ref, o_ref, *, block):
    idx = idx_ref[...]
    for h in range(REL_HEADS):
        acc = jnp.zeros(idx.shape, F32)
        for b in range(REL_BUCKETS):
            acc = jnp.where(idx == b, table_ref[b, h], acc)
        for v in range(3):
            o_ref[v, h * block:(h + 1) * block, :] = acc + mask_ref[v]


def _band_bias(table, block, half, dil):
    rel = np.arange(3 * block)[None, :] - block - np.arange(block)[:, None]
    band = np.abs(rel) <= half
    col = np.arange(3 * block)[None, :]
    masks = np.stack([band & (col >= block), band, band & (col < 2 * block)])
    add = np.where(masks, 0.0, NEG_INF).astype(np.float32)
    return pl.pallas_call(
        functools.partial(_bias_kernel, block=block),
        in_specs=[pl.BlockSpec(memory_space=pltpu.SMEM),
                  pl.BlockSpec(memory_space=pltpu.VMEM),
                  pl.BlockSpec(memory_space=pltpu.VMEM)],
        out_specs=pl.BlockSpec(memory_space=pltpu.VMEM),
        out_shape=jax.ShapeDtypeStruct((3, REL_HEADS * block, 3 * block), F32),
        name=f"rel_bias_d{dil}",
    )(table, jnp.asarray(_t5_bucket_np(rel * dil)), jnp.asarray(add))


def _even_mix_kernel(z_ref, kvp_ref, kvn_ref, h_ref, lng_ref, lnb_ref, ws_ref, bs_ref, bias_ref,
                     sink_ref, wout_ref, o_ref, kv_scr, y_scr):
    i = pl.program_id(0)
    nsub = EVEN_TB // B_BLOCK
    nblk = SEQ // B_BLOCK
    qrows = B_Q_PER_KV * B_BLOCK
    kv_scr[0:B_BLOCK] = kvp_ref[...]
    kv_scr[B_BLOCK:B_BLOCK + EVEN_TB] = z_ref[0, :, EVEN_K0:EVEN_IN]
    kv_scr[B_BLOCK + EVEN_TB:] = kvn_ref[...]
    lng = lng_ref[...]
    lnb = lnb_ref[...]
    for s in range(nsub):
        r0 = s * B_BLOCK
        gb = i * nsub + s
        sel = jnp.where(gb == 0, 0, jnp.where(gb == nblk - 1, 2, 1))
        u = z_ref[0, r0:r0 + A_CHUNK, 0:A_WIDTH].astype(F32)
        va = z_ref[0, r0:r0 + A_CHUNK, A_WIDTH:2 * A_WIDTH].astype(F32)
        mu = jnp.mean(va, axis=-1, keepdims=True)
        vc = va - mu
        var = jnp.mean(vc * vc, axis=-1, keepdims=True)
        vn = (vc * lax.rsqrt(var + EPS) * lng + lnb).astype(BF16)
        for g in range(A_GROUPS):
            c0 = g * A_CH
            mixed = _dot(ws_ref[g], vn[:, c0:c0 + A_CH]) + bs_ref[g]
            y_scr[r0:r0 + A_CHUNK, c0:c0 + A_CH] = (u[:, c0:c0 + A_CH] * mixed).astype(BF16)
        for kh in range(B_KV_HEADS):
            kw = kv_scr[r0:r0 + 3 * B_BLOCK, kh * B_HEAD_DIM:(kh + 1) * B_HEAD_DIM]
            vw = kv_scr[r0:r0 + 3 * B_BLOCK,
                        B_KV_WIDTH + kh * B_HEAD_DIM:B_KV_WIDTH + (kh + 1) * B_HEAD_DIM]
            q0 = EVEN_Q0 + kh * B_Q_PER_KV * B_HEAD_DIM
            q = jnp.concatenate(
                [z_ref[0, r0:r0 + B_BLOCK, q0 + g * B_HEAD_DIM:q0 + (g + 1) * B_HEAD_DIM]
                 for g in range(B_Q_PER_KV)], axis=0)
            lg = (_dot_nt(q, kw) * (B_HEAD_DIM ** -0.5)
                  + bias_ref[sel, kh * qrows:(kh + 1) * qrows, :])
            sk = sink_ref[kh]
            m = jnp.maximum(jnp.max(lg, axis=-1, keepdims=True), sk)
            p = jnp.exp(lg - m)
            den = jnp.sum(p, axis=-1, keepdims=True) + jnp.exp(sk - m)
            o = _dot(p.astype(BF16), vw) * (1.0 / den)
            for g in range(B_Q_PER_KV):
                c0 = A_WIDTH + (kh * B_Q_PER_KV + g) * B_HEAD_DIM
                y_scr[r0:r0 + B_BLOCK, c0:c0 + B_HEAD_DIM] = (
                    o[g * B_BLOCK:(g + 1) * B_BLOCK].astype(BF16))
    o_ref[...] = h_ref[...] + _dot(y_scr[...], wout_ref[...])


def _even_mix(z, h, ln_g, ln_b, w_s, b_s, bias, sink, w_out):
    nsub = EVEN_TB // B_BLOCK
    nblk = SEQ // B_BLOCK
    kv_cb = EVEN_K0 // (2 * B_KV_WIDTH)
    qrows = B_Q_PER_KV * B_BLOCK
    sink_col = jnp.broadcast_to(sink.reshape(B_KV_HEADS, B_Q_PER_KV, 1, 1),
                                (B_KV_HEADS, B_Q_PER_KV, B_BLOCK, 1)).reshape(B_KV_HEADS, qrows, 1)
    return pl.pallas_call(
        _even_mix_kernel,
        grid=(SEQ // EVEN_TB,),
        in_specs=[
            pl.BlockSpec((1, EVEN_TB, EVEN_IN), lambda i: (0, i, 0)),
            pl.BlockSpec((None, B_BLOCK, 2 * B_KV_WIDTH),
                         lambda i: (0, jnp.maximum(i * nsub - 1, 0), kv_cb)),
            pl.BlockSpec((None, B_BLOCK, 2 * B_KV_WIDTH),
                         lambda i: (0, jnp.minimum((i + 1) * nsub, nblk - 1), kv_cb)),
            pl.BlockSpec((EVEN_TB, D_MODEL), lambda i: (i, 0)),
            pl.BlockSpec((1, A_WIDTH), lambda i: (0, 0)),
            pl.BlockSpec((1, A_WIDTH), lambda i: (0, 0)),
            pl.BlockSpec((A_GROUPS, A_CHUNK, A_CHUNK), lambda i: (0, 0, 0)),
            pl.BlockSpec((A_GROUPS, A_CHUNK, A_CH), lambda i: (0, 0, 0)),
            pl.BlockSpec((3, B_HEADS * B_BLOCK, 3 * B_BLOCK), lambda i: (0, 0, 0)),
            pl.BlockSpec((B_KV_HEADS, qrows, 1), lambda i: (0, 0, 0)),
            pl.BlockSpec((A_WIDTH + B_WIDTH, D_MODEL), lambda i: (0, 0)),
        ],
        out_specs=pl.BlockSpec((EVEN_TB, D_MODEL), lambda i: (i, 0)),
        out_shape=jax.ShapeDtypeStruct((SEQ, D_MODEL), F32),
        scratch_shapes=[
            pltpu.VMEM((EVEN_TB + 2 * B_BLOCK, 2 * B_KV_WIDTH), BF16),
            pltpu.VMEM((EVEN_TB, A_WIDTH + B_WIDTH), BF16),
        ],
        compiler_params=_cparams("parallel"),
        name="even_mixer",
    )(z, z, z, h, ln_g.reshape(1, A_WIDTH), ln_b.reshape(1, A_WIDTH), w_s.astype(BF16),
      jnp.broadcast_to(b_s[:, :, None], (A_GROUPS, A_CHUNK, A_CH)), bias, sink_col, w_out)


def _dil_attn_kernel(zc_ref, kp_ref, kn_ref, vp_ref, vn_ref, bias_ref, o_ref, lse_ref,
                     k_scr, v_scr, o_scr, lse_scr, *, dil):
    t = pl.program_id(0)
    seg = ODD_TILE // dil
    nsb = seg // C_BLOCK
    nblk = SEQ // dil // C_BLOCK
    k_scr[:, 0:C_BLOCK] = kp_ref[...]
    k_scr[:, C_BLOCK:C_BLOCK + seg] = zc_ref[:, :, C_WIDTH:2 * C_WIDTH]
    k_scr[:, C_BLOCK + seg:] = kn_ref[...]
    v_scr[:, 0:C_BLOCK] = vp_ref[...]
    v_scr[:, C_BLOCK:C_BLOCK + seg] = zc_ref[:, :, 2 * C_WIDTH:3 * C_WIDTH]
    v_scr[:, C_BLOCK + seg:] = vn_ref[...]
    lane = lax.broadcasted_iota(jnp.int32, (C_BLOCK, V7X_LANES), 1)

    def body(n, carry):
        r = n // nsb
        s = n % nsb
        r0 = pl.multiple_of(s * C_BLOCK, C_BLOCK)
        gb = t * nsb + s
        sel = jnp.where(gb == 0, 0, jnp.where(gb == nblk - 1, 2, 1))
        lgs = []
        for hd in range(C_HEADS):
            c0 = hd * C_HEAD_DIM
            q = zc_ref[r, pl.ds(r0, C_BLOCK), c0:c0 + C_HEAD_DIM]
            kw = k_scr[r, pl.ds(r0, 3 * C_BLOCK), c0:c0 + C_HEAD_DIM]
            lgs.append(_dot_nt(q, kw))
        lg = jnp.concatenate(lgs, axis=0) * (C_HEAD_DIM ** -0.5) + bias_ref[sel]
        m = jnp.max(lg, axis=-1, keepdims=True)
        p = jnp.exp(lg - m)
        den = jnp.sum(p, axis=-1, keepdims=True)
        inv = 1.0 / den
        lse = m + jnp.log(den)
        pb = p.astype(BF16)
        rows = pl.ds(s * (C_BLOCK * dil) + r, C_BLOCK, stride=dil) if dil > 1 else pl.ds(r0, C_BLOCK)
        lse_tile = jnp.zeros((C_BLOCK, V7X_LANES), F32)
        for hd in range(C_HEADS):
            c0 = hd * C_HEAD_DIM
            vw = v_scr[r, pl.ds(r0, 3 * C_BLOCK), c0:c0 + C_HEAD_DIM]
            o = _dot(pb[hd * C_BLOCK:(hd + 1) * C_BLOCK], vw) * inv[hd * C_BLOCK:(hd + 1) * C_BLOCK]
            o_scr[hd, rows, :] = o
            lse_tile = jnp.where(lane == hd, lse[hd * C_BLOCK:(hd + 1) * C_BLOCK], lse_tile)
        lse_scr[rows, :] = lse_tile
        return carry

    lax.fori_loop(0, ODD_BLOCKS, body, 0)
    for hd in range(C_HEADS):
        o_ref[:, hd * C_HEAD_DIM:(hd + 1) * C_HEAD_DIM] = o_scr[hd].astype(o_ref.dtype)
    lse_ref[...] = lse_scr[...]


def _dil_attn(zg, bias, dil):
    seg = ODD_TILE // dil
    nsb = seg // C_BLOCK
    last = SEQ // dil // C_BLOCK - 1

    def halo(j, nxt):
        if nxt:
            return pl.BlockSpec((dil, C_BLOCK, C_WIDTH),
                                lambda t: (0, jnp.minimum((t + 1) * nsb, last), j))
        return pl.BlockSpec((dil, C_BLOCK, C_WIDTH), lambda t: (0, jnp.maximum(t * nsb - 1, 0), j))

    return pl.pallas_call(
        functools.partial(_dil_attn_kernel, dil=dil),
        grid=(SEQ // ODD_TILE,),
        in_specs=[pl.BlockSpec((dil, seg, 3 * C_WIDTH), lambda t: (0, t, 0)),
                  halo(1, False), halo(1, True), halo(2, False), halo(2, True),
                  pl.BlockSpec((3, C_HEADS * C_BLOCK, 3 * C_BLOCK), lambda t: (0, 0, 0))],
        out_specs=[pl.BlockSpec((ODD_TILE, C_WIDTH), lambda t: (t, 0)),
                   pl.BlockSpec((ODD_TILE, V7X_LANES), lambda t: (t, 0))],
        out_shape=[jax.ShapeDtypeStruct((SEQ, C_WIDTH), BF16),
                   jax.ShapeDtypeStruct((SEQ, V7X_LANES), F32)],
        scratch_shapes=[pltpu.VMEM((dil, seg + 2 * C_BLOCK, C_WIDTH), BF16),
                        pltpu.VMEM((dil, seg + 2 * C_BLOCK, C_WIDTH), BF16),
                        pltpu.VMEM((C_HEADS, ODD_TILE, C_HEAD_DIM), F32),
                        pltpu.VMEM((ODD_TILE, V7X_LANES), F32)],
        compiler_params=_cparams("parallel"),
        name=f"dilated_attn_d{dil}",
    )(zg, zg, zg, zg, zg, bias)


def _combine_kernel(o0_ref, o1_ref, o2_ref, l0_ref, l1_ref, l2_ref, h_ref, wout_ref, out_ref, y_scr):
    l0 = l0_ref[...]
    l1 = l1_ref[...]
    l2 = l2_ref[...]
    m = jnp.maximum(jnp.maximum(l0, l1), l2)
    e0 = jnp.exp(l0 - m)
    e1 = jnp.exp(l1 - m)
    e2 = jnp.exp(l2 - m)
    tot = e0 + e1 + e2
    w0 = e0 / tot
    w1 = e1 / tot
    w2 = e2 / tot
    for hd in range(C_HEADS):
        c0 = hd * C_HEAD_DIM
        y = (w0[:, hd:hd + 1] * o0_ref[:, c0:c0 + C_HEAD_DIM].astype(F32)
             + w1[:, hd:hd + 1] * o1_ref[:, c0:c0 + C_HEAD_DIM].astype(F32)
             + w2[:, hd:hd + 1] * o2_ref[:, c0:c0 + C_HEAD_DIM].astype(F32))
        y_scr[:, c0:c0 + C_HEAD_DIM] = y.astype(BF16)
    out_ref[...] = h_ref[...] + _dot(y_scr[...], wout_ref[...])


def _combine(outs, lses, h, w_out):
    blk_o = pl.BlockSpec((COMB_TB, C_WIDTH), lambda i: (i, 0))
    blk_l = pl.BlockSpec((COMB_TB, V7X_LANES), lambda i: (i, 0))
    return pl.pallas_call(
        _combine_kernel,
        grid=(SEQ // COMB_TB,),
        in_specs=[blk_o, blk_o, blk_o, blk_l, blk_l, blk_l,
                  pl.BlockSpec((COMB_TB, D_MODEL), lambda i: (i, 0)),
                  pl.BlockSpec((C_WIDTH, D_MODEL), lambda i: (0, 0))],
        out_specs=pl.BlockSpec((COMB_TB, D_MODEL), lambda i: (i, 0)),
        out_shape=jax.ShapeDtypeStruct((SEQ, D_MODEL), F32),
        scratch_shapes=[pltpu.VMEM((COMB_TB, C_WIDTH), BF16)],
        compiler_params=_cparams("parallel"),
        name="group_combine_proj",
    )(*outs, *lses, h, w_out)


def _cross_kernel(h_ref, gx_ref, wq_ref, kv_ref, wo_ref, gf_ref, wr_ref, br_ref,
                  hx_ref, meta_ref, o_scr):
    h = h_ref[...]
    q = _dot(_rms(h, gx_ref[...]).astype(BF16), wq_ref[...]).astype(BF16)
    for hd in range(X_HEADS):
        c0 = hd * X_HEAD_DIM
        lg = _dot_nt(q[:, c0:c0 + X_HEAD_DIM], kv_ref[0, :, c0:c0 + X_HEAD_DIM]) * (X_HEAD_DIM ** -0.5)
        m = jnp.max(lg, axis=-1, keepdims=True)
        p = jnp.exp(lg - m)
        den = jnp.sum(p, axis=-1, keepdims=True)
        o = _dot(p.astype(BF16), kv_ref[0, :, X_WIDTH + c0:X_WIDTH + c0 + X_HEAD_DIM]) / den
        o_scr[:, c0:c0 + X_HEAD_DIM] = o.astype(BF16)
    h2 = h + _dot(o_scr[...], wo_ref[...])
    hx_ref[:, :D_MODEL] = h2

    t = _rms(h2, gf_ref[...])
    lt = lax.dot_general(wr_ref[...], t, (((1,), (1,)), ((), ())), preferred_element_type=F32,
                         precision=lax.Precision.HIGHEST) + br_ref[...]
    g = [lt[k:k + 1, :] for k in range(MOE_GROUPS)]
    gmax = jnp.maximum(jnp.maximum(g[0], g[1]), jnp.maximum(g[2], g[3]))
    grp = jnp.where(g[0] == gmax, 0, jnp.where(g[1] == gmax, 1, jnp.where(g[2] == gmax, 2, 3)))
    g_gate = 1.0 / (jnp.exp(g[0] - gmax) + jnp.exp(g[1] - gmax) + jnp.exp(g[2] - gmax)
                    + jnp.exp(g[3] - gmax))
    e = []
    for k in range(MOE_EPG):
        rows = [lt[MOE_GROUPS + gi * MOE_EPG + k:MOE_GROUPS + gi * MOE_EPG + k + 1, :]
                for gi in range(MOE_GROUPS)]
        e.append(jnp.where(grp == 0, rows[0], jnp.where(grp == 1, rows[1],
                                                         jnp.where(grp == 2, rows[2], rows[3]))))
    v1 = jnp.maximum(jnp.maximum(e[0], e[1]), jnp.maximum(e[2], e[3]))
    i1 = jnp.where(e[0] == v1, 0, jnp.where(e[1] == v1, 1, jnp.where(e[2] == v1, 2, 3)))
    r = [jnp.where(i1 == k, -jnp.inf, e[k]) for k in range(MOE_EPG)]
    v2 = jnp.maximum(jnp.maximum(r[0], r[1]), jnp.maximum(r[2], r[3]))
    i2 = jnp.where(r[0] == v2, 0, jnp.where(r[1] == v2, 1, jnp.where(r[2] == v2, 2, 3)))
    d = jnp.exp(v2 - v1)
    w1 = g_gate / (1.0 + d)
    w2 = g_gate * d / (1.0 + d)
    first_lo = i1 < i2
    lo = jnp.where(first_lo, i1, i2)
    hi = jnp.where(first_lo, i2, i1)
    w_lo = jnp.where(first_lo, w1, w2)
    w_hi = jnp.where(first_lo, w2, w1)
    pair = jnp.where(lo == 0, hi - 1, jnp.where(lo == 1, jnp.where(hi == 3, 3, 4), 5))
    w_a = jnp.where(lo == 2, w_hi, w_lo)
    w_b = jnp.where(lo == 2, w_lo, w_hi)
    bucket = (grp * N_PAIRS + pair).astype(F32)
    row = lax.broadcasted_iota(jnp.int32, (META_W, CROSS_TB), 0)
    meta = jnp.where(row == 0, bucket, jnp.where(row == 1, w_a, jnp.where(row == 2, w_b, 0.0)))
    meta_ref[...] = meta[0:8, :]
    hx_ref[:, D_MODEL:] = jnp.transpose(meta)


def _cross_router(h, g_cross, wq, kv, wo, g_ffn, wr_t, br):
    full = lambda shape: pl.BlockSpec(shape, lambda i: tuple(0 for _ in shape))
    return pl.pallas_call(
        _cross_kernel,
        grid=(SEQ // CROSS_TB,),
        in_specs=[
            pl.BlockSpec((CROSS_TB, D_MODEL), lambda i: (i, 0)),
            full((1, D_MODEL)),
            full((D_MODEL, X_WIDTH)),
            full((1, MEM_LEN, 2 * X_WIDTH)),
            full((X_WIDTH, D_MODEL)),
            full((1, D_MODEL)),
            full((ROUTER_ROWS, D_MODEL)),
            full((ROUTER_ROWS, 1)),
        ],
        out_specs=[pl.BlockSpec((CROSS_TB, ROW_W), lambda i: (i, 0)),
                   pl.BlockSpec((8, CROSS_TB), lambda i: (0, i))],
        out_shape=[jax.ShapeDtypeStruct((SEQ, ROW_W), F32),
                   jax.ShapeDtypeStruct((8, SEQ), F32)],
        scratch_shapes=[pltpu.VMEM((CROSS_TB, X_WIDTH), BF16)],
        compiler_params=_cparams("parallel"),
        name="cross_attn_router",
    )(h, g_cross.reshape(1, D_MODEL), wq, kv, wo, g_ffn.reshape(1, D_MODEL), wr_t, br)


def _moe_kernel(src_ref, dst_ref, ea_ref, eb_ref, nused_ref,
                hx_hbm, gf_ref, wga_ref, wua_ref, wda_ref, wgb_ref, wub_ref, wdb_ref,
                out_hbm, xbuf, obuf, wup_a, wdn_a, wup_b, wdn_b, gsem, ssem):
    k = pl.program_id(0)
    nused = nused_ref[0]
    slot = k % 2
    other = 1 - slot

    def start_gather(tile, sl):
        for r in range(MOE_TM):
            pltpu.make_async_copy(hx_hbm.at[pl.ds(src_ref[tile * MOE_TM + r], 1)],
                                  xbuf.at[sl, pl.ds(r, 1)], gsem.at[sl]).start()

    def start_scatter(entry, sl):
        for r in range(MOE_TM):
            pltpu.make_async_copy(obuf.at[sl, pl.ds(r, 1)],
                                  out_hbm.at[pl.ds(dst_ref[entry * MOE_TM + r], 1)],
                                  ssem.at[sl]).start()

    def wait_gather(sl):
        pltpu.make_async_copy(hx_hbm.at[pl.ds(0, MOE_TM)], xbuf.at[sl], gsem.at[sl]).wait()

    def wait_scatter(sl):
        pltpu.make_async_copy(obuf.at[sl], out_hbm.at[pl.ds(0, MOE_TM)], ssem.at[sl]).wait()

    @pl.when(k == 0)
    def _():
        start_gather(0, 0)
        obuf[1] = jnp.zeros((MOE_TM, D_MODEL), F32)
        init = pltpu.make_async_copy(obuf.at[1], out_hbm.at[pl.ds(SEQ, MOE_TM)], ssem.at[0])
        init.start()
        init.wait()

    prev = jnp.maximum(k - 1, 0)

    @pl.when((k < nused) & ((k == 0) | (ea_ref[k] != ea_ref[prev])))
    def _():
        wup_a[:, :D_EXPERT] = wga_ref[0].astype(BF16)
        wup_a[:, D_EXPERT:] = wua_ref[0].astype(BF16)
        wdn_a[...] = wda_ref[0].astype(BF16)

    @pl.when((k < nused) & ((k == 0) | (eb_ref[k] != eb_ref[prev])))
    def _():
        wup_b[:, :D_EXPERT] = wgb_ref[0].astype(BF16)
        wup_b[:, D_EXPERT:] = wub_ref[0].astype(BF16)
        wdn_b[...] = wdb_ref[0].astype(BF16)

    @pl.when(k < nused)
    def _():
        wait_gather(slot)

        @pl.when(k >= 1)
        def _():
            wait_scatter(slot)

        x = xbuf[slot]
        start_gather(k + 1, other)
        start_scatter(k, other)
        h2 = x[:, :D_MODEL]
        t = _rms(h2, gf_ref[...]).astype(BF16)
        y = jnp.zeros((MOE_TM, D_MODEL), F32)
        for col, wup, wdn in ((1, wup_a, wdn_a), (2, wup_b, wdn_b)):
            gate = x[:, D_MODEL + col:D_MODEL + col + 1]
            gu = _dot(t, wup[...])
            hid = jax.nn.silu(gu[:, :D_EXPERT]) * gu[:, D_EXPERT:] * gate
            y = y + _dot(hid.astype(BF16), wdn[...])
        obuf[slot] = h2 + y

    @pl.when(k == nused - 1)
    def _():
        start_scatter(k + 1, slot)
        wait_gather(other)
        wait_scatter(other)
        wait_scatter(slot)


def _moe(hx, g_ffn, w_gate, w_up, w_down, src, dst, ea, eb, nused):
    def wspec(shape, which):
        if which == 0:
            return pl.BlockSpec((1,) + shape, lambda k, s, d, a, b, n: (a[k], 0, 0))
        return pl.BlockSpec((1,) + shape, lambda k, s, d, a, b, n: (b[k], 0, 0))

    up_shape = (D_MODEL, D_EXPERT)
    down_shape = (D_EXPERT, D_MODEL)
    grid_spec = pltpu.PrefetchScalarGridSpec(
        num_scalar_prefetch=5,
        grid=(MOE_TILES,),
        in_specs=[
            pl.BlockSpec(memory_space=pl.ANY),
            pl.BlockSpec((1, D_MODEL), lambda k, s, d, a, b, n: (0, 0)),
            wspec(up_shape, 0), wspec(up_shape, 0), wspec(down_shape, 0),
            wspec(up_shape, 1), wspec(up_shape, 1), wspec(down_shape, 1),
        ],
        out_specs=pl.BlockSpec(memory_space=pl.ANY),
        scratch_shapes=[
            pltpu.VMEM((2, MOE_TM, ROW_W), F32),
            pltpu.VMEM((2, MOE_TM, D_MODEL), F32),
            pltpu.VMEM((D_MODEL, 2 * D_EXPERT), BF16),
            pltpu.VMEM((D_EXPERT, D_MODEL), BF16),
            pltpu.VMEM((D_MODEL, 2 * D_EXPERT), BF16),
            pltpu.VMEM((D_EXPERT, D_MODEL), BF16),
            pltpu.SemaphoreType.DMA((2,)),
            pltpu.SemaphoreType.DMA((2,)),
        ],
    )
    return pl.pallas_call(
        _moe_kernel,
        grid_spec=grid_spec,
        out_shape=jax.ShapeDtypeStruct((H_ROWS, D_MODEL), F32),
        compiler_params=_cparams("arbitrary"),
        name="routed_moe",
    )(src, dst, ea, eb, nused, hx, g_ffn.reshape(1, D_MODEL),
      w_gate, w_up, w_down, w_gate, w_up, w_down)


def _route_tables(bucket):
    ids = jnp.arange(N_BUCKETS, dtype=jnp.int32)
    counts = jnp.sum((bucket[:, None] == ids[None, :]).astype(jnp.int32), axis=0)
    ntile = (counts + MOE_TM - 1) // MOE_TM
    pad = ntile * MOE_TM - counts
    tile_end = jnp.cumsum(ntile)
    nused = tile_end[-1]
    dummy_key = jnp.where(jnp.arange(MOE_TM - 1, dtype=jnp.int32)[None, :] < pad[:, None],
                          ids[:, None], N_BUCKETS)
    keys = jnp.concatenate([bucket, dummy_key.reshape(-1)])
    vals = jnp.concatenate([jnp.arange(SEQ, dtype=jnp.int32),
                            jnp.full((N_BUCKETS * (MOE_TM - 1),), SEQ, jnp.int32)])
    _, tok = lax.sort((keys, vals), num_keys=1, is_stable=True)
    tok = tok[:MOE_TILES * MOE_TM]
    valid = tok < SEQ
    slot = jnp.arange(MOE_TILES * MOE_TM, dtype=jnp.int32)
    dump = SEQ + ((slot // MOE_TM) % 2) * MOE_TM + slot % MOE_TM
    src = jnp.concatenate([jnp.where(valid, tok, 0), jnp.zeros((MOE_TM,), jnp.int32)])
    pseudo = SEQ + MOE_TM + jnp.arange(MOE_TM, dtype=jnp.int32)
    dst = jnp.concatenate([pseudo, jnp.where(valid, tok, dump)])
    tiles = jnp.arange(MOE_TILES, dtype=jnp.int32)
    tile_bucket = jnp.minimum(jnp.sum((tiles[:, None] >= tile_end[None, :]).astype(jnp.int32), axis=1),
                              N_BUCKETS - 1)
    onehot = (tile_bucket[:, None] == ids[None, :]).astype(jnp.int32)
    base = (np.arange(N_BUCKETS) // N_PAIRS) * MOE_EPG
    ea = jnp.sum(onehot * jnp.asarray(base + np.asarray(SLOT_A)[np.arange(N_BUCKETS) % N_PAIRS],
                                      jnp.int32)[None, :], axis=1)
    eb = jnp.sum(onehot * jnp.asarray(base + np.asarray(SLOT_B)[np.arange(N_BUCKETS) % N_PAIRS],
                                      jnp.int32)[None, :], axis=1)
    return (src.astype(jnp.int32), dst.astype(jnp.int32), ea.astype(jnp.int32), eb.astype(jnp.int32),
            nused.reshape(1).astype(jnp.int32))


def _final_norm_kernel(h_ref, g_ref, o_ref):
    o_ref[...] = _rms(h_ref[...], g_ref[...])


def _final_norm(h, g):
    tb = PROJ_TM
    return pl.pallas_call(
        _final_norm_kernel,
        grid=(SEQ // tb,),
        in_specs=[pl.BlockSpec((tb, D_MODEL), lambda i: (i, 0)),
                  pl.BlockSpec((1, D_MODEL), lambda i: (0, 0))],
        out_specs=pl.BlockSpec((tb, D_MODEL), lambda i: (i, 0)),
        out_shape=jax.ShapeDtypeStruct((SEQ, D_MODEL), F32),
        compiler_params=_cparams("parallel"),
        name="final_norm",
    )(h, g.reshape(1, D_MODEL))


def kernel(x, mem, ln_mix, ln_cross, ln_mem, ln_ffn, ln_final, rel_table, even_w_in, even_w_out,
           sgu_ln_g, sgu_ln_b, sgu_w, sgu_b, attn_sink, odd_w_in, odd_w_out, xq_w, xkv_w, xo_w,
           router_group_w, router_group_b, router_expert_w, router_expert_b,
           expert_w_gate, expert_w_up, expert_w_down):
    h = x.reshape(SEQ, D_MODEL)
    mem2 = mem.reshape(MEM_LEN, D_MODEL)
    bias_even = _band_bias(rel_table, B_BLOCK, B_HALF_WINDOW, 1)
    bias_odd = [_band_bias(rel_table, C_BLOCK, window // 2 // dil, dil) for window, dil in C_PAIRS]

    for layer in range(DEPTH):
        i = layer // 2
        if layer % 2 == 0:
            z = _proj(h, ln_mix[layer], even_w_in[i].astype(BF16), rows=SEQ, tm=EVEN_TB, tn=EVEN_IN,
                      gelu_cols=2 * A_WIDTH)
            h = _even_mix(z, h, sgu_ln_g[i], sgu_ln_b[i], sgu_w[i], sgu_b[i], bias_even,
                          attn_sink[i], even_w_out[i].astype(BF16))
        else:
            w_in = odd_w_in[i].astype(BF16)
            outs, lses = [], []
            for gi, (_, dil) in enumerate(C_PAIRS):
                zg = _proj(h, ln_mix[layer], w_in[:, gi * 3 * C_WIDTH:(gi + 1) * 3 * C_WIDTH],
                           rows=SEQ, tm=PROJ_TM, tn=PROJ_TN, dil=dil)
                o, lse = _dil_attn(zg, bias_odd[gi], dil)
                outs.append(o)
                lses.append(lse)
            h = _combine(outs, lses, h, odd_w_out[i].astype(BF16))

        kv = _proj(mem2, ln_mem[layer], xkv_w[layer].astype(BF16), rows=MEM_LEN, tm=MEM_LEN,
                   tn=2 * X_WIDTH)
        wr_t = jnp.zeros((ROUTER_ROWS, D_MODEL), F32)
        wr_t = wr_t.at[:MOE_GROUPS].set(router_group_w[layer].T)
        wr_t = wr_t.at[MOE_GROUPS:MOE_GROUPS + N_EXPERTS].set(
            router_expert_w[layer].reshape(D_MODEL, N_EXPERTS).T)
        br = jnp.zeros((ROUTER_ROWS, 1), F32)
        br = br.at[:MOE_GROUPS, 0].set(router_group_b[layer])
        br = br.at[MOE_GROUPS:MOE_GROUPS + N_EXPERTS, 0].set(router_expert_b[layer].reshape(N_EXPERTS))
        hx, meta = _cross_router(h, ln_cross[layer], xq_w[layer].astype(BF16), kv,
                                 xo_w[layer].astype(BF16), ln_ffn[layer], wr_t, br)

        src, dst, ea, eb, nused = _route_tables(meta[0].astype(jnp.int32))
        h = _moe(hx, ln_ffn[layer],
                 expert_w_gate[layer].reshape(N_EXPERTS, D_MODEL, D_EXPERT),
                 expert_w_up[layer].reshape(N_EXPERTS, D_MODEL, D_EXPERT),
                 expert_w_down[layer].reshape(N_EXPERTS, D_EXPERT, D_MODEL),
                 src, dst, ea, eb, nused)

    return _final_norm(h, ln_final).reshape(1, SEQ, D_MODEL)
```

```python
import functools
import math

import numpy as np
import jax
import jax.numpy as jnp
from jax import lax
from jax.experimental import pallas as pl
from jax.experimental.pallas import tpu as pltpu

F32 = jnp.float32
BF16 = jnp.bfloat16

D_MODEL = 1024
SEQ = 16384
DEPTH = 4
MEM_LEN = 256
EPS = 1e-6
NEG_INF = -1e30

A_GROUPS = 4
A_CH = 128
A_WIDTH = A_GROUPS * A_CH
A_CHUNK = 128
B_HEADS = 8
B_KV_HEADS = 2
B_Q_PER_KV = B_HEADS // B_KV_HEADS
B_HEAD_DIM = 64
B_WIDTH = B_HEADS * B_HEAD_DIM
B_KV_WIDTH = B_KV_HEADS * B_HEAD_DIM
B_HALF_WINDOW = 128
B_BLOCK = 128
EVEN_IN = 2 * A_WIDTH + B_WIDTH + 2 * B_KV_WIDTH
EVEN_Q0 = 2 * A_WIDTH
EVEN_K0 = EVEN_Q0 + B_WIDTH
EVEN_V0 = EVEN_K0 + B_KV_WIDTH

C_PAIRS = ((128, 1), (512, 4), (2048, 16))
C_GROUPS = len(C_PAIRS)
C_HEADS = 8
C_HEAD_DIM = 128
C_WIDTH = C_HEADS * C_HEAD_DIM
C_BLOCK = 64
ODD_IN = C_GROUPS * 3 * C_WIDTH

REL_BUCKETS = 32
REL_MAX_DIST = 1024
REL_HEADS = 8

X_HEADS = 4
X_HEAD_DIM = 128
X_WIDTH = X_HEADS * X_HEAD_DIM

MOE_GROUPS = 4
MOE_EPG = 4
N_EXPERTS = MOE_GROUPS * MOE_EPG
D_EXPERT = 512
SLOT_A = (0, 0, 0, 1, 1, 3)
SLOT_B = (1, 2, 3, 3, 2, 2)
N_PAIRS = len(SLOT_A)
N_BUCKETS = MOE_GROUPS * N_PAIRS

V7X_LANES = 128
ROW_CHUNKS = D_MODEL // V7X_LANES
V7X_VMEM_BYTES = 64 * 1024 * 1024
VMEM_LIMIT = 56 * 1024 * 1024

PROJ_TM = 1024
PROJ_TN = 1024
EVEN_TB = 512
EVEN_STACK = 2
ODD_TILE = PROJ_TM
ODD_BLOCKS = ODD_TILE // C_BLOCK
COMB_TB = 512
CROSS_TB = 512
MOE_TM = 256
ROUTER_ROWS = 32
MOE_TILES = (SEQ + N_BUCKETS * (MOE_TM - 1)) // MOE_TM
H_ROWS = SEQ + 2 * MOE_TM


def _cparams(*sem):
    return pltpu.CompilerParams(dimension_semantics=sem, vmem_limit_bytes=VMEM_LIMIT)


def _rms(x, g):
    return x * lax.rsqrt(jnp.mean(x * x, axis=-1, keepdims=True) + EPS) * g


def _dot(a, b):
    return jnp.dot(a, b, preferred_element_type=F32)


def _dot_nt(a, b):
    return lax.dot_general(a, b, (((1,), (1,)), ((), ())), preferred_element_type=F32)


def _load_rows(ref, n, lead=()):
    return jnp.concatenate([ref[lead + (pl.ds(c, n, stride=ROW_CHUNKS), slice(None))]
                            for c in range(ROW_CHUNKS)], axis=1)


def _store_rows(ref, val, lead=()):
    n = val.shape[0]
    for c in range(ROW_CHUNKS):
        ref[lead + (pl.ds(c, n, stride=ROW_CHUNKS), slice(None))] = val[:, c * V7X_LANES:(c + 1) * V7X_LANES]


def _h_spec(tb, tiled, index=lambda i: i):
    if tiled:
        return pl.BlockSpec((tb * ROW_CHUNKS, V7X_LANES), lambda i, *_: (index(i), 0))
    return pl.BlockSpec((tb, D_MODEL), lambda i, *_: (index(i), 0))


def _h_load(ref, tb, tiled):
    return _load_rows(ref, tb) if tiled else ref[...]


def _proj_kernel(h_ref, g_ref, w_ref, o_ref, xn_ref, *scratch, gelu_cols, dil, tiled):
    tm = xn_ref.shape[0]
    seg = tm // dil

    @pl.when(pl.program_id(1) == 0)
    def _():
        xf = _rms(_h_load(h_ref, tm, tiled), g_ref[...])
        if dil == 1:
            xn_ref[...] = xf.astype(BF16)
        else:
            xs_ref, = scratch
            for c in range(D_MODEL // V7X_LANES):
                xs_ref[c] = xf[:, c * V7X_LANES:(c + 1) * V7X_LANES]
            for r in range(dil):
                for c in range(D_MODEL // V7X_LANES):
                    xn_ref[r * seg:(r + 1) * seg, c * V7X_LANES:(c + 1) * V7X_LANES] = (
                        xs_ref[c, pl.ds(r, seg, stride=dil), :].astype(BF16))

    acc = _dot(xn_ref[...], w_ref[...])
    if gelu_cols:
        o_ref[0, :, :gelu_cols] = jax.nn.gelu(acc[:, :gelu_cols]).astype(o_ref.dtype)
        o_ref[0, :, gelu_cols:] = acc[:, gelu_cols:].astype(o_ref.dtype)
    else:
        for r in range(dil):
            o_ref[r] = acc[r * seg:(r + 1) * seg].astype(o_ref.dtype)


def _proj(h, g, w, *, rows, tm, tn, gelu_cols=0, dil=1, tiled=False):
    n = w.shape[1]
    seg = tm // dil
    return pl.pallas_call(
        functools.partial(_proj_kernel, gelu_cols=gelu_cols, dil=dil, tiled=tiled),
        grid=(rows // tm, n // tn),
        in_specs=[
            _h_spec(tm, tiled),
            pl.BlockSpec((1, D_MODEL), lambda i, j: (0, 0)),
            pl.BlockSpec((D_MODEL, tn), lambda i, j: (0, j)),
        ],
        out_specs=pl.BlockSpec((dil, seg, tn), lambda i, j: (0, i, j)),
        out_shape=jax.ShapeDtypeStruct((dil, rows // dil, n), BF16),
        scratch_shapes=[pltpu.VMEM((tm, D_MODEL), BF16)] + (
            [pltpu.VMEM((D_MODEL // V7X_LANES, tm, V7X_LANES), F32)] if dil > 1 else []),
        compiler_params=_cparams("parallel", "arbitrary"),
        name=f"norm_proj_d{dil}",
    )(h, g.reshape(1, D_MODEL), w)


def _t5_bucket_np(rel):
    nb = REL_BUCKETS // 2
    max_exact = nb // 2
    ret = np.where(rel > 0, nb, 0)
    n = np.abs(rel)
    nf = np.maximum(n, 1).astype(np.float32)
    large = max_exact + (np.log(nf / np.float32(max_exact)) / np.float32(math.log(REL_MAX_DIST / max_exact))
                         * np.float32(nb - max_exact)).astype(np.int32)
    large = np.minimum(large, nb - 1)
    return (ret + np.where(n < max_exact, n, large)).astype(np.int32)


def _bias_kernel(table_ref, idx_ref, mask_ref, o_ref, *, block):
    idx = idx_ref[...]
    for h in range(REL_HEADS):
        acc = jnp.zeros(idx.shape, F32)
        for b in range(REL_BUCKETS):
            acc = jnp.where(idx == b, table_ref[b, h], acc)
        for v in range(3):
            o_ref[v, h * block:(h + 1) * block, :] = acc + mask_ref[v]


def _band_bias(table, block, half, dil):
    rel = np.arange(3 * block)[None, :] - block - np.arange(block)[:, None]
    band = np.abs(rel) <= half
    col = np.arange(3 * block)[None, :]
    masks = np.stack([band & (col >= block), band, band & (col < 2 * block)])
    add = np.where(masks, 0.0, NEG_INF).astype(np.float32)
    return pl.pallas_call(
        functools.partial(_bias_kernel, block=block),
        in_specs=[pl.BlockSpec(memory_space=pltpu.SMEM),
                  pl.BlockSpec(memory_space=pltpu.VMEM),
                  pl.BlockSpec(memory_space=pltpu.VMEM)],
        out_specs=pl.BlockSpec(memory_space=pltpu.VMEM),
        out_shape=jax.ShapeDtypeStruct((3, REL_HEADS * block, 3 * block), F32),
        name=f"rel_bias_d{dil}",
    )(table, jnp.asarray(_t5_bucket_np(rel * dil)), jnp.asarray(add))


def _even_mix_kernel(z_ref, kvp_ref, kvn_ref, h_ref, lng_ref, lnb_ref, ws_ref, bs_ref, bias_ref,
                     sink_ref, wout_ref, o_ref, kv_scr, y_scr, *, tiled):
    i = pl.program_id(0)
    nsub = EVEN_TB // B_BLOCK
    nblk = SEQ // B_BLOCK
    kv_scr[0:B_BLOCK] = kvp_ref[...]
    kv_scr[B_BLOCK:B_BLOCK + EVEN_TB] = z_ref[0, :, EVEN_K0:EVEN_IN]
    kv_scr[B_BLOCK + EVEN_TB:] = kvn_ref[...]
    lng = lng_ref[...]
    lnb = lnb_ref[...]
    for s in range(nsub):
        r0 = s * B_BLOCK
        gb = i * nsub + s
        sel = jnp.where(gb == 0, 0, jnp.where(gb == nblk - 1, 2, 1))
        u = z_ref[0, r0:r0 + A_CHUNK, 0:A_WIDTH].astype(F32)
        va = z_ref[0, r0:r0 + A_CHUNK, A_WIDTH:2 * A_WIDTH].astype(F32)
        mu = jnp.mean(va, axis=-1, keepdims=True)
        vc = va - mu
        var = jnp.mean(vc * vc, axis=-1, keepdims=True)
        vn = (vc * lax.rsqrt(var + EPS) * lng + lnb).astype(BF16)
        for g in range(A_GROUPS):
            c0 = g * A_CH
            mixed = _dot(ws_ref[g], vn[:, c0:c0 + A_CH]) + bs_ref[g]
            y_scr[r0:r0 + A_CHUNK, c0:c0 + A_CH] = (u[:, c0:c0 + A_CH] * mixed).astype(BF16)
        for kh in range(B_KV_HEADS):
            kw = kv_scr[r0:r0 + 3 * B_BLOCK, kh * B_HEAD_DIM:(kh + 1) * B_HEAD_DIM]
            vw = kv_scr[r0:r0 + 3 * B_BLOCK,
                        B_KV_WIDTH + kh * B_HEAD_DIM:B_KV_WIDTH + (kh + 1) * B_HEAD_DIM]
            for half in range(B_Q_PER_KV // EVEN_STACK):
                hd0 = kh * B_Q_PER_KV + half * EVEN_STACK
                q = jnp.concatenate(
                    [z_ref[0, r0:r0 + B_BLOCK,
                           EVEN_Q0 + (hd0 + g) * B_HEAD_DIM:EVEN_Q0 + (hd0 + g + 1) * B_HEAD_DIM]
                     for g in range(EVEN_STACK)], axis=0)
                lg = (_dot_nt(q, kw) * (B_HEAD_DIM ** -0.5)
                      + bias_ref[sel, hd0 * B_BLOCK:(hd0 + EVEN_STACK) * B_BLOCK, :])
                sk = sink_ref[hd0 * B_BLOCK:(hd0 + EVEN_STACK) * B_BLOCK, :]
                m = jnp.maximum(jnp.max(lg, axis=-1, keepdims=True), sk)
                p = jnp.exp(lg - m)
                den = jnp.sum(p, axis=-1, keepdims=True) + jnp.exp(sk - m)
                o = _dot(p.astype(BF16), vw) * (1.0 / den)
                for g in range(EVEN_STACK):
                    c0 = A_WIDTH + (hd0 + g) * B_HEAD_DIM
                    y_scr[r0:r0 + B_BLOCK, c0:c0 + B_HEAD_DIM] = (
                        o[g * B_BLOCK:(g + 1) * B_BLOCK].astype(BF16))
    o_ref[...] = _h_load(h_ref, EVEN_TB, tiled) + _dot(y_scr[...], wout_ref[...])


def _even_mix(z, h, ln_g, ln_b, w_s, b_s, bias, sink, w_out, *, tiled):
    nsub = EVEN_TB // B_BLOCK
    nblk = SEQ // B_BLOCK
    kv_cb = EVEN_K0 // (2 * B_KV_WIDTH)
    sink_col = jnp.broadcast_to(sink.reshape(B_HEADS, 1, 1), (B_HEADS, B_BLOCK, 1)).reshape(
        B_HEADS * B_BLOCK, 1)
    return pl.pallas_call(
        functools.partial(_even_mix_kernel, tiled=tiled),
        grid=(SEQ // EVEN_TB,),
        in_specs=[
            pl.BlockSpec((1, EVEN_TB, EVEN_IN), lambda i: (0, i, 0)),
            pl.BlockSpec((None, B_BLOCK, 2 * B_KV_WIDTH),
                         lambda i: (0, jnp.maximum(i * nsub - 1, 0), kv_cb)),
            pl.BlockSpec((None, B_BLOCK, 2 * B_KV_WIDTH),
                         lambda i: (0, jnp.minimum((i + 1) * nsub, nblk - 1), kv_cb)),
            _h_spec(EVEN_TB, tiled),
            pl.BlockSpec((1, A_WIDTH), lambda i: (0, 0)),
            pl.BlockSpec((1, A_WIDTH), lambda i: (0, 0)),
            pl.BlockSpec((A_GROUPS, A_CHUNK, A_CHUNK), lambda i: (0, 0, 0)),
            pl.BlockSpec((A_GROUPS, A_CHUNK, A_CH), lambda i: (0, 0, 0)),
            pl.BlockSpec((3, B_HEADS * B_BLOCK, 3 * B_BLOCK), lambda i: (0, 0, 0)),
            pl.BlockSpec((B_HEADS * B_BLOCK, 1), lambda i: (0, 0)),
            pl.BlockSpec((A_WIDTH + B_WIDTH, D_MODEL), lambda i: (0, 0)),
        ],
        out_specs=pl.BlockSpec((EVEN_TB, D_MODEL), lambda i: (i, 0)),
        out_shape=jax.ShapeDtypeStruct((SEQ, D_MODEL), F32),
        scratch_shapes=[
            pltpu.VMEM((EVEN_TB + 2 * B_BLOCK, 2 * B_KV_WIDTH), BF16),
            pltpu.VMEM((EVEN_TB, A_WIDTH + B_WIDTH), BF16),
        ],
        compiler_params=_cparams("parallel"),
        name="even_mixer",
    )(z, z, z, h, ln_g.reshape(1, A_WIDTH), ln_b.reshape(1, A_WIDTH), w_s.astype(BF16),
      jnp.broadcast_to(b_s[:, :, None], (A_GROUPS, A_CHUNK, A_CH)), bias, sink_col, w_out)


def _dil_attn_kernel(zc_ref, kp_ref, kn_ref, vp_ref, vn_ref, bias_ref, o_ref, lse_ref,
                     k_scr, v_scr, o_scr, lse_scr, *, dil):
    t = pl.program_id(0)
    seg = ODD_TILE // dil
    nsb = seg // C_BLOCK
    nblk = SEQ // dil // C_BLOCK
    k_scr[:, 0:C_BLOCK] = kp_ref[...]
    k_scr[:, C_BLOCK:C_BLOCK + seg] = zc_ref[:, :, C_WIDTH:2 * C_WIDTH]
    k_scr[:, C_BLOCK + seg:] = kn_ref[...]
    v_scr[:, 0:C_BLOCK] = vp_ref[...]
    v_scr[:, C_BLOCK:C_BLOCK + seg] = zc_ref[:, :, 2 * C_WIDTH:3 * C_WIDTH]
    v_scr[:, C_BLOCK + seg:] = vn_ref[...]
    lane = lax.broadcasted_iota(jnp.int32, (C_BLOCK, V7X_LANES), 1)

    def body(n, carry):
        r = n // nsb
        s = n % nsb
        r0 = pl.multiple_of(s * C_BLOCK, C_BLOCK)
        gb = t * nsb + s
        sel = jnp.where(gb == 0, 0, jnp.where(gb == nblk - 1, 2, 1))
        lgs = []
        for hd in range(C_HEADS):
            c0 = hd * C_HEAD_DIM
            q = zc_ref[r, pl.ds(r0, C_BLOCK), c0:c0 + C_HEAD_DIM]
            kw = k_scr[r, pl.ds(r0, 3 * C_BLOCK), c0:c0 + C_HEAD_DIM]
            lgs.append(_dot_nt(q, kw))
        lg = jnp.concatenate(lgs, axis=0) * (C_HEAD_DIM ** -0.5) + bias_ref[sel]
        m = jnp.max(lg, axis=-1, keepdims=True)
        p = jnp.exp(lg - m)
        den = jnp.sum(p, axis=-1, keepdims=True)
        inv = 1.0 / den
        lse = m + jnp.log(den)
        pb = p.astype(BF16)
        rows = pl.ds(s * (C_BLOCK * dil) + r, C_BLOCK, stride=dil) if dil > 1 else pl.ds(r0, C_BLOCK)
        lse_tile = jnp.zeros((C_BLOCK, V7X_LANES), F32)
        for hd in range(C_HEADS):
            c0 = hd * C_HEAD_DIM
            vw = v_scr[r, pl.ds(r0, 3 * C_BLOCK), c0:c0 + C_HEAD_DIM]
            o = _dot(pb[hd * C_BLOCK:(hd + 1) * C_BLOCK], vw) * inv[hd * C_BLOCK:(hd + 1) * C_BLOCK]
            o_scr[hd, rows, :] = o
            lse_tile = jnp.where(lane == hd, lse[hd * C_BLOCK:(hd + 1) * C_BLOCK], lse_tile)
        lse_scr[rows, :] = lse_tile
        return carry

    lax.fori_loop(0, ODD_BLOCKS, body, 0, unroll=2)
    for hd in range(C_HEADS):
        o_ref[:, hd * C_HEAD_DIM:(hd + 1) * C_HEAD_DIM] = o_scr[hd].astype(o_ref.dtype)
    lse_ref[...] = lse_scr[...]


def _dil_attn(zg, bias, dil):
    seg = ODD_TILE // dil
    nsb = seg // C_BLOCK
    last = SEQ // dil // C_BLOCK - 1

    def halo(j, nxt):
        if nxt:
            return pl.BlockSpec((dil, C_BLOCK, C_WIDTH),
                                lambda t: (0, jnp.minimum((t + 1) * nsb, last), j))
        return pl.BlockSpec((dil, C_BLOCK, C_WIDTH), lambda t: (0, jnp.maximum(t * nsb - 1, 0), j))

    return pl.pallas_call(
        functools.partial(_dil_attn_kernel, dil=dil),
        grid=(SEQ // ODD_TILE,),
        in_specs=[pl.BlockSpec((dil, seg, 3 * C_WIDTH), lambda t: (0, t, 0)),
                  halo(1, False), halo(1, True), halo(2, False), halo(2, True),
                  pl.BlockSpec((3, C_HEADS * C_BLOCK, 3 * C_BLOCK), lambda t: (0, 0, 0))],
        out_specs=[pl.BlockSpec((ODD_TILE, C_WIDTH), lambda t: (t, 0)),
                   pl.BlockSpec((ODD_TILE, V7X_LANES), lambda t: (t, 0))],
        out_shape=[jax.ShapeDtypeStruct((SEQ, C_WIDTH), BF16),
                   jax.ShapeDtypeStruct((SEQ, V7X_LANES), F32)],
        scratch_shapes=[pltpu.VMEM((dil, seg + 2 * C_BLOCK, C_WIDTH), BF16),
                        pltpu.VMEM((dil, seg + 2 * C_BLOCK, C_WIDTH), BF16),
                        pltpu.VMEM((C_HEADS, ODD_TILE, C_HEAD_DIM), F32),
                        pltpu.VMEM((ODD_TILE, V7X_LANES), F32)],
        compiler_params=_cparams("parallel"),
        name=f"dilated_attn_d{dil}",
    )(zg, zg, zg, zg, zg, bias)


def _combine_kernel(o0_ref, o1_ref, o2_ref, l0_ref, l1_ref, l2_ref, h_ref, wout_ref, out_ref, y_scr,
                    *, tiled):
    l0 = l0_ref[...]
    l1 = l1_ref[...]
    l2 = l2_ref[...]
    m = jnp.maximum(jnp.maximum(l0, l1), l2)
    e0 = jnp.exp(l0 - m)
    e1 = jnp.exp(l1 - m)
    e2 = jnp.exp(l2 - m)
    tot = e0 + e1 + e2
    w0 = e0 / tot
    w1 = e1 / tot
    w2 = e2 / tot
    for hd in range(C_HEADS):
        c0 = hd * C_HEAD_DIM
        y = (w0[:, hd:hd + 1] * o0_ref[:, c0:c0 + C_HEAD_DIM].astype(F32)
             + w1[:, hd:hd + 1] * o1_ref[:, c0:c0 + C_HEAD_DIM].astype(F32)
             + w2[:, hd:hd + 1] * o2_ref[:, c0:c0 + C_HEAD_DIM].astype(F32))
        y_scr[:, c0:c0 + C_HEAD_DIM] = y.astype(BF16)
    out_ref[...] = _h_load(h_ref, COMB_TB, tiled) + _dot(y_scr[...], wout_ref[...])


def _combine(outs, lses, h, w_out, *, tiled):
    blk_o = pl.BlockSpec((COMB_TB, C_WIDTH), lambda i: (i, 0))
    blk_l = pl.BlockSpec((COMB_TB, V7X_LANES), lambda i: (i, 0))
    return pl.pallas_call(
        functools.partial(_combine_kernel, tiled=tiled),
        grid=(SEQ // COMB_TB,),
        in_specs=[blk_o, blk_o, blk_o, blk_l, blk_l, blk_l,
                  _h_spec(COMB_TB, tiled),
                  pl.BlockSpec((C_WIDTH, D_MODEL), lambda i: (0, 0))],
        out_specs=pl.BlockSpec((COMB_TB, D_MODEL), lambda i: (i, 0)),
        out_shape=jax.ShapeDtypeStruct((SEQ, D_MODEL), F32),
        scratch_shapes=[pltpu.VMEM((COMB_TB, C_WIDTH), BF16)],
        compiler_params=_cparams("parallel"),
        name="group_combine_proj",
    )(*outs, *lses, h, w_out)


def _cross_kernel(h_ref, gx_ref, wq_ref, kv_ref, wo_ref, gf_ref, wr_ref, br_ref,
                  hx_ref, meta_ref, o_scr):
    h = h_ref[...]
    q = _dot(_rms(h, gx_ref[...]).astype(BF16), wq_ref[...]).astype(BF16)
    for hd in range(X_HEADS):
        c0 = hd * X_HEAD_DIM
        lg = _dot_nt(q[:, c0:c0 + X_HEAD_DIM], kv_ref[0, :, c0:c0 + X_HEAD_DIM]) * (X_HEAD_DIM ** -0.5)
        m = jnp.max(lg, axis=-1, keepdims=True)
        p = jnp.exp(lg - m)
        den = jnp.sum(p, axis=-1, keepdims=True)
        o = _dot(p.astype(BF16), kv_ref[0, :, X_WIDTH + c0:X_WIDTH + c0 + X_HEAD_DIM]) / den
        o_scr[:, c0:c0 + X_HEAD_DIM] = o.astype(BF16)
    h2 = h + _dot(o_scr[...], wo_ref[...])
    _store_rows(hx_ref, h2)

    t = _rms(h2, gf_ref[...])
    t_hi = t.astype(BF16)
    t_lo = (t - t_hi.astype(F32)).astype(BF16)
    lt = (_dot_nt(wr_ref[0], t_hi) + _dot_nt(wr_ref[0], t_lo) + _dot_nt(wr_ref[1], t_hi)) + br_ref[...]
    g = [lt[k:k + 1, :] for k in range(MOE_GROUPS)]
    gmax = jnp.maximum(jnp.maximum(g[0], g[1]), jnp.maximum(g[2], g[3]))
    grp = jnp.where(g[0] == gmax, 0, jnp.where(g[1] == gmax, 1, jnp.where(g[2] == gmax, 2, 3)))
    g_gate = 1.0 / (jnp.exp(g[0] - gmax) + jnp.exp(g[1] - gmax) + jnp.exp(g[2] - gmax)
                    + jnp.exp(g[3] - gmax))
    e = []
    for k in range(MOE_EPG):
        rows = [lt[MOE_GROUPS + gi * MOE_EPG + k:MOE_GROUPS + gi * MOE_EPG + k + 1, :]
                for gi in range(MOE_GROUPS)]
        e.append(jnp.where(grp == 0, rows[0], jnp.where(grp == 1, rows[1],
                                                         jnp.where(grp == 2, rows[2], rows[3]))))
    v1 = jnp.maximum(jnp.maximum(e[0], e[1]), jnp.maximum(e[2], e[3]))
    i1 = jnp.where(e[0] == v1, 0, jnp.where(e[1] == v1, 1, jnp.where(e[2] == v1, 2, 3)))
    r = [jnp.where(i1 == k, -jnp.inf, e[k]) for k in range(MOE_EPG)]
    v2 = jnp.maximum(jnp.maximum(r[0], r[1]), jnp.maximum(r[2], r[3]))
    i2 = jnp.where(r[0] == v2, 0, jnp.where(r[1] == v2, 1, jnp.where(r[2] == v2, 2, 3)))
    d = jnp.exp(v2 - v1)
    w1 = g_gate / (1.0 + d)
    w2 = g_gate * d / (1.0 + d)
    first_lo = i1 < i2
    lo = jnp.where(first_lo, i1, i2)
    hi = jnp.where(first_lo, i2, i1)
    w_lo = jnp.where(first_lo, w1, w2)
    w_hi = jnp.where(first_lo, w2, w1)
    pair = jnp.where(lo == 0, hi - 1, jnp.where(lo == 1, jnp.where(hi == 3, 3, 4), 5))
    w_a = jnp.where(lo == 2, w_hi, w_lo)
    w_b = jnp.where(lo == 2, w_lo, w_hi)
    bucket = (grp * N_PAIRS + pair).astype(F32)
    row = lax.broadcasted_iota(jnp.int32, (8, CROSS_TB), 0)
    meta_ref[...] = jnp.where(row == 0, bucket, jnp.where(row == 1, w_a, jnp.where(row == 2, w_b, 0.0)))


def _cross_router(h, g_cross, wq, kv, wo, g_ffn, wr_t, br):
    full = lambda shape: pl.BlockSpec(shape, lambda i: tuple(0 for _ in shape))
    return pl.pallas_call(
        _cross_kernel,
        grid=(SEQ // CROSS_TB,),
        in_specs=[
            pl.BlockSpec((CROSS_TB, D_MODEL), lambda i: (i, 0)),
            full((1, D_MODEL)),
            full((D_MODEL, X_WIDTH)),
            full((1, MEM_LEN, 2 * X_WIDTH)),
            full((X_WIDTH, D_MODEL)),
            full((1, D_MODEL)),
            full((2, ROUTER_ROWS, D_MODEL)),
            full((ROUTER_ROWS, 1)),
        ],
        out_specs=[_h_spec(CROSS_TB, True),
                   pl.BlockSpec((8, CROSS_TB), lambda i: (0, i))],
        out_shape=[jax.ShapeDtypeStruct((SEQ * ROW_CHUNKS, V7X_LANES), F32),
                   jax.ShapeDtypeStruct((8, SEQ), F32)],
        scratch_shapes=[pltpu.VMEM((CROSS_TB, X_WIDTH), BF16)],
        compiler_params=_cparams("parallel"),
        name="cross_attn_router",
    )(h, g_cross.reshape(1, D_MODEL), wq, kv, wo, g_ffn.reshape(1, D_MODEL), wr_t, br)


def _moe_kernel(src_ref, dst_ref, ea_ref, eb_ref, nused_ref,
                hx_hbm, gates_ref, gf_ref, wga_ref, wua_ref, wda_ref, wgb_ref, wub_ref, wdb_ref,
                out_hbm, xbuf, obuf, wup_a, wdn_a, wup_b, wdn_b, gsem, ssem):
    k = pl.program_id(0)
    nused = nused_ref[0]
    slot = k % 2
    other = 1 - slot

    def row_tile(idx):
        return pl.ds(pl.multiple_of(idx * ROW_CHUNKS, ROW_CHUNKS), ROW_CHUNKS)

    def start_gather(tile, sl):
        for r in range(MOE_TM):
            pltpu.make_async_copy(hx_hbm.at[row_tile(src_ref[tile * MOE_TM + r])],
                                  xbuf.at[sl, pl.ds(r * ROW_CHUNKS, ROW_CHUNKS)], gsem.at[sl]).start()

    def start_scatter(entry, sl):
        for r in range(MOE_TM):
            pltpu.make_async_copy(obuf.at[sl, pl.ds(r * ROW_CHUNKS, ROW_CHUNKS)],
                                  out_hbm.at[row_tile(dst_ref[entry * MOE_TM + r])], ssem.at[sl]).start()

    def wait_gather(sl):
        pltpu.make_async_copy(hx_hbm.at[pl.ds(0, MOE_TM * ROW_CHUNKS)], xbuf.at[sl], gsem.at[sl]).wait()

    def wait_scatter(sl):
        pltpu.make_async_copy(obuf.at[sl], out_hbm.at[pl.ds(0, MOE_TM * ROW_CHUNKS)], ssem.at[sl]).wait()

    @pl.when(k == 0)
    def _():
        start_gather(0, 0)
        obuf[1] = jnp.zeros((MOE_TM * ROW_CHUNKS, V7X_LANES), F32)
        init = pltpu.make_async_copy(obuf.at[1], out_hbm.at[pl.ds(SEQ * ROW_CHUNKS, MOE_TM * ROW_CHUNKS)],
                                     ssem.at[0])
        init.start()
        init.wait()

    prev = jnp.maximum(k - 1, 0)

    @pl.when((k < nused) & ((k == 0) | (ea_ref[k] != ea_ref[prev])))
    def _():
        wup_a[:, :D_EXPERT] = wga_ref[0].astype(BF16)
        wup_a[:, D_EXPERT:] = wua_ref[0].astype(BF16)
        wdn_a[...] = wda_ref[0].astype(BF16)

    @pl.when((k < nused) & ((k == 0) | (eb_ref[k] != eb_ref[prev])))
    def _():
        wup_b[:, :D_EXPERT] = wgb_ref[0].astype(BF16)
        wup_b[:, D_EXPERT:] = wub_ref[0].astype(BF16)
        wdn_b[...] = wdb_ref[0].astype(BF16)

    @pl.when(k < nused)
    def _():
        wait_gather(slot)

        @pl.when(k >= 1)
        def _():
            wait_scatter(slot)

        h2 = _load_rows(xbuf, MOE_TM, (slot,))
        start_gather(k + 1, other)
        start_scatter(k, other)
        t = _rms(h2, gf_ref[...]).astype(BF16)
        y = jnp.zeros((MOE_TM, D_MODEL), F32)
        for col, wup, wdn in ((0, wup_a, wdn_a), (1, wup_b, wdn_b)):
            gate = gates_ref[:, col:col + 1]
            gu = _dot(t, wup[...])
            hid = jax.nn.silu(gu[:, :D_EXPERT]) * gu[:, D_EXPERT:] * gate
            y = y + _dot(hid.astype(BF16), wdn[...])
        _store_rows(obuf, h2 + y, (slot,))

    @pl.when(k == nused - 1)
    def _():
        start_scatter(k + 1, slot)
        wait_gather(other)
        wait_scatter(other)
        wait_scatter(slot)


def _moe(hx, gates, g_ffn, w_gate, w_up, w_down, src, dst, ea, eb, nused):
    def wspec(shape, which):
        if which == 0:
            return pl.BlockSpec((1,) + shape, lambda k, s, d, a, b, n: (a[k], 0, 0))
        return pl.BlockSpec((1,) + shape, lambda k, s, d, a, b, n: (b[k], 0, 0))

    up_shape = (D_MODEL, D_EXPERT)
    down_shape = (D_EXPERT, D_MODEL)
    grid_spec = pltpu.PrefetchScalarGridSpec(
        num_scalar_prefetch=5,
        grid=(MOE_TILES,),
        in_specs=[
            pl.BlockSpec(memory_space=pl.ANY),
            pl.BlockSpec((MOE_TM, V7X_LANES), lambda k, s, d, a, b, n: (k, 0)),
            pl.BlockSpec((1, D_MODEL), lambda k, s, d, a, b, n: (0, 0)),
            wspec(up_shape, 0), wspec(up_shape, 0), wspec(down_shape, 0),
            wspec(up_shape, 1), wspec(up_shape, 1), wspec(down_shape, 1),
        ],
        out_specs=pl.BlockSpec(memory_space=pl.ANY),
        scratch_shapes=[
            pltpu.VMEM((2, MOE_TM * ROW_CHUNKS, V7X_LANES), F32),
            pltpu.VMEM((2, MOE_TM * ROW_CHUNKS, V7X_LANES), F32),
            pltpu.VMEM((D_MODEL, 2 * D_EXPERT), BF16),
            pltpu.VMEM((D_EXPERT, D_MODEL), BF16),
            pltpu.VMEM((D_MODEL, 2 * D_EXPERT), BF16),
            pltpu.VMEM((D_EXPERT, D_MODEL), BF16),
            pltpu.SemaphoreType.DMA((2,)),
            pltpu.SemaphoreType.DMA((2,)),
        ],
    )
    return pl.pallas_call(
        _moe_kernel,
        grid_spec=grid_spec,
        out_shape=jax.ShapeDtypeStruct((H_ROWS * ROW_CHUNKS, V7X_LANES), F32),
        compiler_params=_cparams("arbitrary"),
        name="routed_moe",
    )(src, dst, ea, eb, nused, hx, gates, g_ffn.reshape(1, D_MODEL),
      w_gate, w_up, w_down, w_gate, w_up, w_down)


def _route_tables(meta):
    bucket = meta[0].astype(jnp.int32)
    ids = jnp.arange(N_BUCKETS, dtype=jnp.int32)
    counts = jnp.sum((bucket[:, None] == ids[None, :]).astype(jnp.int32), axis=0)
    ntile = (counts + MOE_TM - 1) // MOE_TM
    pad = ntile * MOE_TM - counts
    tile_end = jnp.cumsum(ntile)
    nused = tile_end[-1]
    dummy_key = jnp.where(jnp.arange(MOE_TM - 1, dtype=jnp.int32)[None, :] < pad[:, None],
                          ids[:, None], N_BUCKETS)
    keys = jnp.concatenate([bucket, dummy_key.reshape(-1)])
    vals = jnp.concatenate([jnp.arange(SEQ, dtype=jnp.int32),
                            jnp.full((N_BUCKETS * (MOE_TM - 1),), SEQ, jnp.int32)])
    nslot = MOE_TILES * MOE_TM
    zpad = jnp.zeros((N_BUCKETS * (MOE_TM - 1),), F32)
    _, tok, ga, gb = lax.sort((keys, vals, jnp.concatenate([meta[1], zpad]), jnp.concatenate([meta[2], zpad])),
                              num_keys=1, is_stable=True)
    tok = tok[:nslot]
    gates = jnp.pad(jnp.stack([ga[:nslot], gb[:nslot]], axis=1), ((0, 0), (0, V7X_LANES - 2)))
    valid = tok < SEQ
    slot = jnp.arange(MOE_TILES * MOE_TM, dtype=jnp.int32)
    dump = SEQ + ((slot // MOE_TM) % 2) * MOE_TM + slot % MOE_TM
    src = jnp.concatenate([jnp.where(valid, tok, 0), jnp.zeros((MOE_TM,), jnp.int32)])
    pseudo = SEQ + MOE_TM + jnp.arange(MOE_TM, dtype=jnp.int32)
    dst = jnp.concatenate([pseudo, jnp.where(valid, tok, dump)])
    tiles = jnp.arange(MOE_TILES, dtype=jnp.int32)
    tile_bucket = jnp.minimum(jnp.sum((tiles[:, None] >= tile_end[None, :]).astype(jnp.int32), axis=1),
                              N_BUCKETS - 1)
    onehot = (tile_bucket[:, None] == ids[None, :]).astype(jnp.int32)
    base = (np.arange(N_BUCKETS) // N_PAIRS) * MOE_EPG
    ea = jnp.sum(onehot * jnp.asarray(base + np.asarray(SLOT_A)[np.arange(N_BUCKETS) % N_PAIRS],
                                      jnp.int32)[None, :], axis=1)
    eb = jnp.sum(onehot * jnp.asarray(base + np.asarray(SLOT_B)[np.arange(N_BUCKETS) % N_PAIRS],
                                      jnp.int32)[None, :], axis=1)
    return (src.astype(jnp.int32), dst.astype(jnp.int32), ea.astype(jnp.int32), eb.astype(jnp.int32),
            nused.reshape(1).astype(jnp.int32), gates)


def _final_norm_kernel(h_ref, g_ref, o_ref):
    o_ref[...] = _rms(_load_rows(h_ref, o_ref.shape[0]), g_ref[...])


def _final_norm(h, g):
    tb = PROJ_TM
    return pl.pallas_call(
        _final_norm_kernel,
        grid=(SEQ // tb,),
        in_specs=[_h_spec(tb, True),
                  pl.BlockSpec((1, D_MODEL), lambda i: (0, 0))],
        out_specs=pl.BlockSpec((tb, D_MODEL), lambda i: (i, 0)),
        out_shape=jax.ShapeDtypeStruct((SEQ, D_MODEL), F32),
        compiler_params=_cparams("parallel"),
        name="final_norm",
    )(h, g.reshape(1, D_MODEL))


def kernel(x, mem, ln_mix, ln_cross, ln_mem, ln_ffn, ln_final, rel_table, even_w_in, even_w_out,
           sgu_ln_g, sgu_ln_b, sgu_w, sgu_b, attn_sink, odd_w_in, odd_w_out, xq_w, xkv_w, xo_w,
           router_group_w, router_group_b, router_expert_w, router_expert_b,
           expert_w_gate, expert_w_up, expert_w_down):
    h = x.reshape(SEQ, D_MODEL)
    mem2 = mem.reshape(MEM_LEN, D_MODEL)
    bias_even = _band_bias(rel_table, B_BLOCK, B_HALF_WINDOW, 1)
    bias_odd = [_band_bias(rel_table, C_BLOCK, window // 2 // dil, dil) for window, dil in C_PAIRS]

    for layer in range(DEPTH):
        i = layer // 2
        tiled = layer > 0
        if layer % 2 == 0:
            z = _proj(h, ln_mix[layer], even_w_in[i].astype(BF16), rows=SEQ, tm=EVEN_TB, tn=EVEN_IN,
                      gelu_cols=2 * A_WIDTH, tiled=tiled)
            h = _even_mix(z, h, sgu_ln_g[i], sgu_ln_b[i], sgu_w[i], sgu_b[i], bias_even,
                          attn_sink[i], even_w_out[i].astype(BF16), tiled=tiled)
        else:
            w_in = odd_w_in[i].astype(BF16)
            outs, lses = [], []
            for gi, (_, dil) in enumerate(C_PAIRS):
                zg = _proj(h, ln_mix[layer], w_in[:, gi * 3 * C_WIDTH:(gi + 1) * 3 * C_WIDTH],
                           rows=SEQ, tm=PROJ_TM, tn=PROJ_TN, dil=dil, tiled=tiled)
                o, lse = _dil_attn(zg, bias_odd[gi], dil)
                outs.append(o)
                lses.append(lse)
            h = _combine(outs, lses, h, odd_w_out[i].astype(BF16), tiled=tiled)

        kv = _proj(mem2, ln_mem[layer], xkv_w[layer].astype(BF16), rows=MEM_LEN, tm=MEM_LEN,
                   tn=2 * X_WIDTH)
        wr_t = jnp.zeros((ROUTER_ROWS, D_MODEL), F32)
        wr_t = wr_t.at[:MOE_GROUPS].set(router_group_w[layer].T)
        wr_t = wr_t.at[MOE_GROUPS:MOE_GROUPS + N_EXPERTS].set(
            router_expert_w[layer].reshape(D_MODEL, N_EXPERTS).T)
        br = jnp.zeros((ROUTER_ROWS, 1), F32)
        br = br.at[:MOE_GROUPS, 0].set(router_group_b[layer])
        br = br.at[MOE_GROUPS:MOE_GROUPS + N_EXPERTS, 0].set(router_expert_b[layer].reshape(N_EXPERTS))
        wr_hi = wr_t.astype(BF16)
        wr_split = jnp.stack([wr_hi, (wr_t - wr_hi.astype(F32)).astype(BF16)])
        hx, meta = _cross_router(h, ln_cross[layer], xq_w[layer].astype(BF16), kv,
                                 xo_w[layer].astype(BF16), ln_ffn[layer], wr_split, br)

        src, dst, ea, eb, nused, gates = _route_tables(meta)
        h = _moe(hx, gates, ln_ffn[layer],
                 expert_w_gate[layer].reshape(N_EXPERTS, D_MODEL, D_EXPERT),
                 expert_w_up[layer].reshape(N_EXPERTS, D_MODEL, D_EXPERT),
                 expert_w_down[layer].reshape(N_EXPERTS, D_EXPERT, D_MODEL),
                 src, dst, ea, eb, nused)

    return _final_norm(h, ln_final).reshape(1, SEQ, D_MODEL)
```

```python
import functools
import math

import numpy as np
import jax
import jax.numpy as jnp
from jax import lax
from jax.experimental import pallas as pl
from jax.experimental.pallas import tpu as pltpu

F32 = jnp.float32
BF16 = jnp.bfloat16

D_MODEL = 1024
SEQ = 16384
DEPTH = 4
MEM_LEN = 256
EPS = 1e-6
NEG_INF = -1e30

A_GROUPS = 4
A_CH = 128
A_WIDTH = A_GROUPS * A_CH
A_CHUNK = 128
B_HEADS = 8
B_KV_HEADS = 2
B_Q_PER_KV = B_HEADS // B_KV_HEADS
B_HEAD_DIM = 64
B_WIDTH = B_HEADS * B_HEAD_DIM
B_KV_WIDTH = B_KV_HEADS * B_HEAD_DIM
B_HALF_WINDOW = 128
B_BLOCK = 128
EVEN_IN = 2 * A_WIDTH + B_WIDTH + 2 * B_KV_WIDTH
EVEN_Q0 = 2 * A_WIDTH
EVEN_K0 = EVEN_Q0 + B_WIDTH
EVEN_V0 = EVEN_K0 + B_KV_WIDTH

C_PAIRS = ((128, 1), (512, 4), (2048, 16))
C_GROUPS = len(C_PAIRS)
C_HEADS = 8
C_HEAD_DIM = 128
C_WIDTH = C_HEADS * C_HEAD_DIM
C_BLOCK = 64
ODD_IN = C_GROUPS * 3 * C_WIDTH

REL_BUCKETS = 32
REL_MAX_DIST = 1024
REL_HEADS = 8

X_HEADS = 4
X_HEAD_DIM = 128
X_WIDTH = X_HEADS * X_HEAD_DIM

MOE_GROUPS = 4
MOE_EPG = 4
N_EXPERTS = MOE_GROUPS * MOE_EPG
D_EXPERT = 512
SLOT_A = (0, 0, 0, 1, 1, 3)
SLOT_B = (1, 2, 3, 3, 2, 2)
N_PAIRS = len(SLOT_A)
N_BUCKETS = MOE_GROUPS * N_PAIRS

V7X_LANES = 128
ROW_CHUNKS = D_MODEL // V7X_LANES
V7X_VMEM_BYTES = 64 * 1024 * 1024
VMEM_LIMIT = 56 * 1024 * 1024

PROJ_TM = 1024
PROJ_TN = 1024
EVEN_TB = 512
EVEN_STACK = 2
ODD_TILE = PROJ_TM
ODD_BLOCKS = ODD_TILE // C_BLOCK
COMB_TB = 512
CROSS_TB = 512
MOE_TM = 256
SCATTER_DMA_PRIORITY = 1
ROUTER_ROWS = 32
MOE_TILES = (SEQ + N_BUCKETS * (MOE_TM - 1)) // MOE_TM
H_ROWS = SEQ + 2 * MOE_TM


def _cparams(*sem):
    return pltpu.CompilerParams(dimension_semantics=sem, vmem_limit_bytes=VMEM_LIMIT)


def _rms(x, g):
    return x * lax.rsqrt(jnp.mean(x * x, axis=-1, keepdims=True) + EPS) * g


def _dot(a, b):
    return jnp.dot(a, b, preferred_element_type=F32)


def _dot_nt(a, b):
    return lax.dot_general(a, b, (((1,), (1,)), ((), ())), preferred_element_type=F32)


def _load_rows(ref, n, lead=()):
    return jnp.concatenate([ref[lead + (pl.ds(c, n, stride=ROW_CHUNKS), slice(None))]
                            for c in range(ROW_CHUNKS)], axis=1)


def _store_rows(ref, val, lead=()):
    n = val.shape[0]
    for c in range(ROW_CHUNKS):
        ref[lead + (pl.ds(c, n, stride=ROW_CHUNKS), slice(None))] = val[:, c * V7X_LANES:(c + 1) * V7X_LANES]


def _h_spec(tb, tiled, index=lambda i: i):
    if tiled:
        return pl.BlockSpec((tb * ROW_CHUNKS, V7X_LANES), lambda i, *_: (index(i), 0))
    return pl.BlockSpec((tb, D_MODEL), lambda i, *_: (index(i), 0))


def _h_load(ref, tb, tiled):
    return _load_rows(ref, tb) if tiled else ref[...]


def _proj_kernel(h_ref, g_ref, w_ref, o_ref, xn_ref, *scratch, gelu_cols, dil, tiled):
    tm = xn_ref.shape[0]
    seg = tm // dil

    @pl.when(pl.program_id(1) == 0)
    def _():
        xf = _rms(_h_load(h_ref, tm, tiled), g_ref[...])
        if dil == 1:
            xn_ref[...] = xf.astype(BF16)
        else:
            xs_ref, = scratch
            for c in range(D_MODEL // V7X_LANES):
                xs_ref[c] = xf[:, c * V7X_LANES:(c + 1) * V7X_LANES]
            for r in range(dil):
                for c in range(D_MODEL // V7X_LANES):
                    xn_ref[r * seg:(r + 1) * seg, c * V7X_LANES:(c + 1) * V7X_LANES] = (
                        xs_ref[c, pl.ds(r, seg, stride=dil), :].astype(BF16))

    acc = _dot(xn_ref[...], w_ref[...])
    if gelu_cols:
        o_ref[0, :, :gelu_cols] = jax.nn.gelu(acc[:, :gelu_cols]).astype(o_ref.dtype)
        o_ref[0, :, gelu_cols:] = acc[:, gelu_cols:].astype(o_ref.dtype)
    else:
        for r in range(dil):
            o_ref[r] = acc[r * seg:(r + 1) * seg].astype(o_ref.dtype)


def _proj(h, g, w, *, rows, tm, tn, gelu_cols=0, dil=1, tiled=False):
    n = w.shape[1]
    seg = tm // dil
    return pl.pallas_call(
        functools.partial(_proj_kernel, gelu_cols=gelu_cols, dil=dil, tiled=tiled),
        grid=(rows // tm, n // tn),
        in_specs=[
            _h_spec(tm, tiled),
            pl.BlockSpec((1, D_MODEL), lambda i, j: (0, 0)),
            pl.BlockSpec((D_MODEL, tn), lambda i, j: (0, j)),
        ],
        out_specs=pl.BlockSpec((dil, seg, tn), lambda i, j: (0, i, j)),
        out_shape=jax.ShapeDtypeStruct((dil, rows // dil, n), BF16),
        scratch_shapes=[pltpu.VMEM((tm, D_MODEL), BF16)] + (
            [pltpu.VMEM((D_MODEL // V7X_LANES, tm, V7X_LANES), F32)] if dil > 1 else []),
        compiler_params=_cparams("parallel", "arbitrary"),
        name=f"norm_proj_d{dil}",
    )(h, g.reshape(1, D_MODEL), w)


def _t5_bucket_np(rel):
    nb = REL_BUCKETS // 2
    max_exact = nb // 2
    ret = np.where(rel > 0, nb, 0)
    n = np.abs(rel)
    nf = np.maximum(n, 1).astype(np.float32)
    large = max_exact + (np.log(nf / np.float32(max_exact)) / np.float32(math.log(REL_MAX_DIST / max_exact))
                         * np.float32(nb - max_exact)).astype(np.int32)
    large = np.minimum(large, nb - 1)
    return (ret + np.where(n < max_exact, n, large)).astype(np.int32)


def _bias_kernel(table_ref, idx_ref, mask_ref, o_ref, *, block):
    idx = idx_ref[...]
    for h in range(REL_HEADS):
        acc = jnp.zeros(idx.shape, F32)
        for b in range(REL_BUCKETS):
            acc = jnp.where(idx == b, table_ref[b, h], acc)
        for v in range(3):
            o_ref[v, h * block:(h + 1) * block, :] = acc + mask_ref[v]


def _band_bias(table, block, half, dil):
    rel = np.arange(3 * block)[None, :] - block - np.arange(block)[:, None]
    band = np.abs(rel) <= half
    col = np.arange(3 * block)[None, :]
    masks = np.stack([band & (col >= block), band, band & (col < 2 * block)])
    add = np.where(masks, 0.0, NEG_INF).astype(np.float32)
    return pl.pallas_call(
        functools.partial(_bias_kernel, block=block),
        in_specs=[pl.BlockSpec(memory_space=pltpu.SMEM),
                  pl.BlockSpec(memory_space=pltpu.VMEM),
                  pl.BlockSpec(memory_space=pltpu.VMEM)],
        out_specs=pl.BlockSpec(memory_space=pltpu.VMEM),
        out_shape=jax.ShapeDtypeStruct((3, REL_HEADS * block, 3 * block), F32),
        name=f"rel_bias_d{dil}",
    )(table, jnp.asarray(_t5_bucket_np(rel * dil)), jnp.asarray(add))


def _even_mix_kernel(z_ref, kvp_ref, kvn_ref, h_ref, lng_ref, lnb_ref, ws_ref, bs_ref, bias_ref,
                     sink_ref, wout_ref, o_ref, kv_scr, y_scr, *, tiled):
    i = pl.program_id(0)
    nsub = EVEN_TB // B_BLOCK
    nblk = SEQ // B_BLOCK
    kv_scr[0:B_BLOCK] = kvp_ref[...]
    kv_scr[B_BLOCK:B_BLOCK + EVEN_TB] = z_ref[0, :, EVEN_K0:EVEN_IN]
    kv_scr[B_BLOCK + EVEN_TB:] = kvn_ref[...]
    lng = lng_ref[...]
    lnb = lnb_ref[...]
    for s in range(nsub):
        r0 = s * B_BLOCK
        gb = i * nsub + s
        sel = jnp.where(gb == 0, 0, jnp.where(gb == nblk - 1, 2, 1))
        u = z_ref[0, r0:r0 + A_CHUNK, 0:A_WIDTH].astype(F32)
        va = z_ref[0, r0:r0 + A_CHUNK, A_WIDTH:2 * A_WIDTH].astype(F32)
        mu = jnp.mean(va, axis=-1, keepdims=True)
        vc = va - mu
        var = jnp.mean(vc * vc, axis=-1, keepdims=True)
        vn = (vc * lax.rsqrt(var + EPS) * lng + lnb).astype(BF16)
        for g in range(A_GROUPS):
            c0 = g * A_CH
            mixed = _dot(ws_ref[g], vn[:, c0:c0 + A_CH]) + bs_ref[g]
            y_scr[r0:r0 + A_CHUNK, c0:c0 + A_CH] = (u[:, c0:c0 + A_CH] * mixed).astype(BF16)
        for kh in range(B_KV_HEADS):
            kw = kv_scr[r0:r0 + 3 * B_BLOCK, kh * B_HEAD_DIM:(kh + 1) * B_HEAD_DIM]
            vw = kv_scr[r0:r0 + 3 * B_BLOCK,
                        B_KV_WIDTH + kh * B_HEAD_DIM:B_KV_WIDTH + (kh + 1) * B_HEAD_DIM]
            for half in range(B_Q_PER_KV // EVEN_STACK):
                hd0 = kh * B_Q_PER_KV + half * EVEN_STACK
                q = jnp.concatenate(
                    [z_ref[0, r0:r0 + B_BLOCK,
                           EVEN_Q0 + (hd0 + g) * B_HEAD_DIM:EVEN_Q0 + (hd0 + g + 1) * B_HEAD_DIM]
                     for g in range(EVEN_STACK)], axis=0)
                lg = (_dot_nt(q, kw) * (B_HEAD_DIM ** -0.5)
                      + bias_ref[sel, hd0 * B_BLOCK:(hd0 + EVEN_STACK) * B_BLOCK, :])
                sk = sink_ref[hd0 * B_BLOCK:(hd0 + EVEN_STACK) * B_BLOCK, :]
                m = jnp.maximum(jnp.max(lg, axis=-1, keepdims=True), sk)
                p = jnp.exp(lg - m)
                den = jnp.sum(p, axis=-1, keepdims=True) + jnp.exp(sk - m)
                o = _dot(p.astype(BF16), vw) * (1.0 / den)
                for g in range(EVEN_STACK):
                    c0 = A_WIDTH + (hd0 + g) * B_HEAD_DIM
                    y_scr[r0:r0 + B_BLOCK, c0:c0 + B_HEAD_DIM] = (
                        o[g * B_BLOCK:(g + 1) * B_BLOCK].astype(BF16))
    o_ref[...] = _h_load(h_ref, EVEN_TB, tiled) + _dot(y_scr[...], wout_ref[...])


def _even_mix(z, h, ln_g, ln_b, w_s, b_s, bias, sink, w_out, *, tiled):
    nsub = EVEN_TB // B_BLOCK
    nblk = SEQ // B_BLOCK
    kv_cb = EVEN_K0 // (2 * B_KV_WIDTH)
    sink_col = jnp.broadcast_to(sink.reshape(B_HEADS, 1, 1), (B_HEADS, B_BLOCK, 1)).reshape(
        B_HEADS * B_BLOCK, 1)
    return pl.pallas_call(
        functools.partial(_even_mix_kernel, tiled=tiled),
        grid=(SEQ // EVEN_TB,),
        in_specs=[
            pl.BlockSpec((1, EVEN_TB, EVEN_IN), lambda i: (0, i, 0)),
            pl.BlockSpec((None, B_BLOCK, 2 * B_KV_WIDTH),
                         lambda i: (0, jnp.maximum(i * nsub - 1, 0), kv_cb)),
            pl.BlockSpec((None, B_BLOCK, 2 * B_KV_WIDTH),
                         lambda i: (0, jnp.minimum((i + 1) * nsub, nblk - 1), kv_cb)),
            _h_spec(EVEN_TB, tiled),
            pl.BlockSpec((1, A_WIDTH), lambda i: (0, 0)),
            pl.BlockSpec((1, A_WIDTH), lambda i: (0, 0)),
            pl.BlockSpec((A_GROUPS, A_CHUNK, A_CHUNK), lambda i: (0, 0, 0)),
            pl.BlockSpec((A_GROUPS, A_CHUNK, A_CH), lambda i: (0, 0, 0)),
            pl.BlockSpec((3, B_HEADS * B_BLOCK, 3 * B_BLOCK), lambda i: (0, 0, 0)),
            pl.BlockSpec((B_HEADS * B_BLOCK, 1), lambda i: (0, 0)),
            pl.BlockSpec((A_WIDTH + B_WIDTH, D_MODEL), lambda i: (0, 0)),
        ],
        out_specs=pl.BlockSpec((EVEN_TB, D_MODEL), lambda i: (i, 0)),
        out_shape=jax.ShapeDtypeStruct((SEQ, D_MODEL), F32),
        scratch_shapes=[
            pltpu.VMEM((EVEN_TB + 2 * B_BLOCK, 2 * B_KV_WIDTH), BF16),
            pltpu.VMEM((EVEN_TB, A_WIDTH + B_WIDTH), BF16),
        ],
        compiler_params=_cparams("parallel"),
        name="even_mixer",
    )(z, z, z, h, ln_g.reshape(1, A_WIDTH), ln_b.reshape(1, A_WIDTH), w_s.astype(BF16),
      jnp.broadcast_to(b_s[:, :, None], (A_GROUPS, A_CHUNK, A_CH)), bias, sink_col, w_out)


def _dil_attn_kernel(zc_ref, kp_ref, kn_ref, vp_ref, vn_ref, bias_ref, o_ref, lse_ref,
                     k_scr, v_scr, o_scr, lse_scr, *, dil):
    t = pl.program_id(0)
    seg = ODD_TILE // dil
    nsb = seg // C_BLOCK
    nblk = SEQ // dil // C_BLOCK
    k_scr[:, 0:C_BLOCK] = kp_ref[...]
    k_scr[:, C_BLOCK:C_BLOCK + seg] = zc_ref[:, :, C_WIDTH:2 * C_WIDTH]
    k_scr[:, C_BLOCK + seg:] = kn_ref[...]
    v_scr[:, 0:C_BLOCK] = vp_ref[...]
    v_scr[:, C_BLOCK:C_BLOCK + seg] = zc_ref[:, :, 2 * C_WIDTH:3 * C_WIDTH]
    v_scr[:, C_BLOCK + seg:] = vn_ref[...]
    lane = lax.broadcasted_iota(jnp.int32, (C_BLOCK, V7X_LANES), 1)

    def body(n, carry):
        r = n // nsb
        s = n % nsb
        r0 = pl.multiple_of(s * C_BLOCK, C_BLOCK)
        gb = t * nsb + s
        sel = jnp.where(gb == 0, 0, jnp.where(gb == nblk - 1, 2, 1))
        lgs = []
        for hd in range(C_HEADS):
            c0 = hd * C_HEAD_DIM
            q = zc_ref[r, pl.ds(r0, C_BLOCK), c0:c0 + C_HEAD_DIM]
            kw = k_scr[r, pl.ds(r0, 3 * C_BLOCK), c0:c0 + C_HEAD_DIM]
            lgs.append(_dot_nt(q, kw))
        lg = jnp.concatenate(lgs, axis=0) * (C_HEAD_DIM ** -0.5) + bias_ref[sel]
        m = jnp.max(lg, axis=-1, keepdims=True)
        p = jnp.exp(lg - m)
        den = jnp.sum(p, axis=-1, keepdims=True)
        inv = 1.0 / den
        lse = m + jnp.log(den)
        pb = p.astype(BF16)
        rows = pl.ds(s * (C_BLOCK * dil) + r, C_BLOCK, stride=dil) if dil > 1 else pl.ds(r0, C_BLOCK)
        lse_tile = jnp.zeros((C_BLOCK, V7X_LANES), F32)
        for hd in range(C_HEADS):
            c0 = hd * C_HEAD_DIM
            vw = v_scr[r, pl.ds(r0, 3 * C_BLOCK), c0:c0 + C_HEAD_DIM]
            o = _dot(pb[hd * C_BLOCK:(hd + 1) * C_BLOCK], vw) * inv[hd * C_BLOCK:(hd + 1) * C_BLOCK]
            o_scr[hd, rows, :] = o
            lse_tile = jnp.where(lane == hd, lse[hd * C_BLOCK:(hd + 1) * C_BLOCK], lse_tile)
        lse_scr[rows, :] = lse_tile
        return carry

    lax.fori_loop(0, ODD_BLOCKS, body, 0, unroll=2)
    for hd in range(C_HEADS):
        o_ref[:, hd * C_HEAD_DIM:(hd + 1) * C_HEAD_DIM] = o_scr[hd].astype(o_ref.dtype)
    lse_ref[...] = lse_scr[...]


def _dil_attn(zg, bias, dil):
    seg = ODD_TILE // dil
    nsb = seg // C_BLOCK
    last = SEQ // dil // C_BLOCK - 1

    def halo(j, nxt):
        if nxt:
            return pl.BlockSpec((dil, C_BLOCK, C_WIDTH),
                                lambda t: (0, jnp.minimum((t + 1) * nsb, last), j))
        return pl.BlockSpec((dil, C_BLOCK, C_WIDTH), lambda t: (0, jnp.maximum(t * nsb - 1, 0), j))

    return pl.pallas_call(
        functools.partial(_dil_attn_kernel, dil=dil),
        grid=(SEQ // ODD_TILE,),
        in_specs=[pl.BlockSpec((dil, seg, 3 * C_WIDTH), lambda t: (0, t, 0)),
                  halo(1, False), halo(1, True), halo(2, False), halo(2, True),
                  pl.BlockSpec((3, C_HEADS * C_BLOCK, 3 * C_BLOCK), lambda t: (0, 0, 0))],
        out_specs=[pl.BlockSpec((ODD_TILE, C_WIDTH), lambda t: (t, 0)),
                   pl.BlockSpec((ODD_TILE, V7X_LANES), lambda t: (t, 0))],
        out_shape=[jax.ShapeDtypeStruct((SEQ, C_WIDTH), BF16),
                   jax.ShapeDtypeStruct((SEQ, V7X_LANES), F32)],
        scratch_shapes=[pltpu.VMEM((dil, seg + 2 * C_BLOCK, C_WIDTH), BF16),
                        pltpu.VMEM((dil, seg + 2 * C_BLOCK, C_WIDTH), BF16),
                        pltpu.VMEM((C_HEADS, ODD_TILE, C_HEAD_DIM), F32),
                        pltpu.VMEM((ODD_TILE, V7X_LANES), F32)],
        compiler_params=_cparams("parallel"),
        name=f"dilated_attn_d{dil}",
    )(zg, zg, zg, zg, zg, bias)


def _combine_kernel(o0_ref, o1_ref, o2_ref, l0_ref, l1_ref, l2_ref, h_ref, wout_ref, out_ref, y_scr,
                    *, tiled):
    l0 = l0_ref[...]
    l1 = l1_ref[...]
    l2 = l2_ref[...]
    m = jnp.maximum(jnp.maximum(l0, l1), l2)
    e0 = jnp.exp(l0 - m)
    e1 = jnp.exp(l1 - m)
    e2 = jnp.exp(l2 - m)
    tot = e0 + e1 + e2
    w0 = e0 / tot
    w1 = e1 / tot
    w2 = e2 / tot
    for hd in range(C_HEADS):
        c0 = hd * C_HEAD_DIM
        y = (w0[:, hd:hd + 1] * o0_ref[:, c0:c0 + C_HEAD_DIM].astype(F32)
             + w1[:, hd:hd + 1] * o1_ref[:, c0:c0 + C_HEAD_DIM].astype(F32)
             + w2[:, hd:hd + 1] * o2_ref[:, c0:c0 + C_HEAD_DIM].astype(F32))
        y_scr[:, c0:c0 + C_HEAD_DIM] = y.astype(BF16)
    out_ref[...] = _h_load(h_ref, COMB_TB, tiled) + _dot(y_scr[...], wout_ref[...])


def _combine(outs, lses, h, w_out, *, tiled):
    blk_o = pl.BlockSpec((COMB_TB, C_WIDTH), lambda i: (i, 0))
    blk_l = pl.BlockSpec((COMB_TB, V7X_LANES), lambda i: (i, 0))
    return pl.pallas_call(
        functools.partial(_combine_kernel, tiled=tiled),
        grid=(SEQ // COMB_TB,),
        in_specs=[blk_o, blk_o, blk_o, blk_l, blk_l, blk_l,
                  _h_spec(COMB_TB, tiled),
                  pl.BlockSpec((C_WIDTH, D_MODEL), lambda i: (0, 0))],
        out_specs=pl.BlockSpec((COMB_TB, D_MODEL), lambda i: (i, 0)),
        out_shape=jax.ShapeDtypeStruct((SEQ, D_MODEL), F32),
        scratch_shapes=[pltpu.VMEM((COMB_TB, C_WIDTH), BF16)],
        compiler_params=_cparams("parallel"),
        name="group_combine_proj",
    )(*outs, *lses, h, w_out)


def _cross_kernel(h_ref, gx_ref, wq_ref, kv_ref, wo_ref, gf_ref, wr_ref, br_ref,
                  hx_ref, meta_ref, o_scr):
    h = h_ref[...]
    q = _dot(_rms(h, gx_ref[...]).astype(BF16), wq_ref[...]).astype(BF16)
    for hd in range(X_HEADS):
        c0 = hd * X_HEAD_DIM
        lg = _dot_nt(q[:, c0:c0 + X_HEAD_DIM], kv_ref[0, :, c0:c0 + X_HEAD_DIM]) * (X_HEAD_DIM ** -0.5)
        m = jnp.max(lg, axis=-1, keepdims=True)
        p = jnp.exp(lg - m)
        den = jnp.sum(p, axis=-1, keepdims=True)
        o = _dot(p.astype(BF16), kv_ref[0, :, X_WIDTH + c0:X_WIDTH + c0 + X_HEAD_DIM]) / den
        o_scr[:, c0:c0 + X_HEAD_DIM] = o.astype(BF16)
    h2 = h + _dot(o_scr[...], wo_ref[...])
    _store_rows(hx_ref, h2)

    t = _rms(h2, gf_ref[...])
    t_hi = t.astype(BF16)
    t_lo = (t - t_hi.astype(F32)).astype(BF16)
    lt = (_dot_nt(wr_ref[0], t_hi) + _dot_nt(wr_ref[0], t_lo) + _dot_nt(wr_ref[1], t_hi)) + br_ref[...]
    g = [lt[k:k + 1, :] for k in range(MOE_GROUPS)]
    gmax = jnp.maximum(jnp.maximum(g[0], g[1]), jnp.maximum(g[2], g[3]))
    grp = jnp.where(g[0] == gmax, 0, jnp.where(g[1] == gmax, 1, jnp.where(g[2] == gmax, 2, 3)))
    g_gate = 1.0 / (jnp.exp(g[0] - gmax) + jnp.exp(g[1] - gmax) + jnp.exp(g[2] - gmax)
                    + jnp.exp(g[3] - gmax))
    e = []
    for k in range(MOE_EPG):
        rows = [lt[MOE_GROUPS + gi * MOE_EPG + k:MOE_GROUPS + gi * MOE_EPG + k + 1, :]
                for gi in range(MOE_GROUPS)]
        e.append(jnp.where(grp == 0, rows[0], jnp.where(grp == 1, rows[1],
                                                         jnp.where(grp == 2, rows[2], rows[3]))))
    v1 = jnp.maximum(jnp.maximum(e[0], e[1]), jnp.maximum(e[2], e[3]))
    i1 = jnp.where(e[0] == v1, 0, jnp.where(e[1] == v1, 1, jnp.where(e[2] == v1, 2, 3)))
    r = [jnp.where(i1 == k, -jnp.inf, e[k]) for k in range(MOE_EPG)]
    v2 = jnp.maximum(jnp.maximum(r[0], r[1]), jnp.maximum(r[2], r[3]))
    i2 = jnp.where(r[0] == v2, 0, jnp.where(r[1] == v2, 1, jnp.where(r[2] == v2, 2, 3)))
    d = jnp.exp(v2 - v1)
    w1 = g_gate / (1.0 + d)
    w2 = g_gate * d / (1.0 + d)
    first_lo = i1 < i2
    lo = jnp.where(first_lo, i1, i2)
    hi = jnp.where(first_lo, i2, i1)
    w_lo = jnp.where(first_lo, w1, w2)
    w_hi = jnp.where(first_lo, w2, w1)
    pair = jnp.where(lo == 0, hi - 1, jnp.where(lo == 1, jnp.where(hi == 3, 3, 4), 5))
    w_a = jnp.where(lo == 2, w_hi, w_lo)
    w_b = jnp.where(lo == 2, w_lo, w_hi)
    bucket = (grp * N_PAIRS + pair).astype(F32)
    row = lax.broadcasted_iota(jnp.int32, (8, CROSS_TB), 0)
    meta_ref[...] = jnp.where(row == 0, bucket, jnp.where(row == 1, w_a, jnp.where(row == 2, w_b, 0.0)))


def _cross_router(h, g_cross, wq, kv, wo, g_ffn, wr_t, br):
    full = lambda shape: pl.BlockSpec(shape, lambda i: tuple(0 for _ in shape))
    return pl.pallas_call(
        _cross_kernel,
        grid=(SEQ // CROSS_TB,),
        in_specs=[
            pl.BlockSpec((CROSS_TB, D_MODEL), lambda i: (i, 0)),
            full((1, D_MODEL)),
            full((D_MODEL, X_WIDTH)),
            full((1, MEM_LEN, 2 * X_WIDTH)),
            full((X_WIDTH, D_MODEL)),
            full((1, D_MODEL)),
            full((2, ROUTER_ROWS, D_MODEL)),
            full((ROUTER_ROWS, 1)),
        ],
        out_specs=[_h_spec(CROSS_TB, True),
                   pl.BlockSpec((8, CROSS_TB), lambda i: (0, i))],
        out_shape=[jax.ShapeDtypeStruct((SEQ * ROW_CHUNKS, V7X_LANES), F32),
                   jax.ShapeDtypeStruct((8, SEQ), F32)],
        scratch_shapes=[pltpu.VMEM((CROSS_TB, X_WIDTH), BF16)],
        compiler_params=_cparams("parallel"),
        name="cross_attn_router",
    )(h, g_cross.reshape(1, D_MODEL), wq, kv, wo, g_ffn.reshape(1, D_MODEL), wr_t, br)


def _moe_kernel(src_ref, dst_ref, ea_ref, eb_ref, nused_ref,
                hx_hbm, gates_ref, gf_ref, wga_ref, wua_ref, wda_ref, wgb_ref, wub_ref, wdb_ref,
                out_hbm, xbuf, obuf, wup_a, wdn_a, wup_b, wdn_b, gsem, ssem):
    k = pl.program_id(0)
    nused = nused_ref[0]
    slot = k % 2
    other = 1 - slot
    xslot = k % 3

    def row_tile(idx):
        return pl.ds(pl.multiple_of(idx * ROW_CHUNKS, ROW_CHUNKS), ROW_CHUNKS)

    def start_gather(tile, sl):
        for r in range(MOE_TM):
            pltpu.make_async_copy(hx_hbm.at[row_tile(src_ref[tile * MOE_TM + r])],
                                  xbuf.at[sl, pl.ds(r * ROW_CHUNKS, ROW_CHUNKS)], gsem.at[sl]).start()

    def start_scatter(entry, sl):
        for r in range(MOE_TM):
            pltpu.make_async_copy(obuf.at[sl, pl.ds(r * ROW_CHUNKS, ROW_CHUNKS)],
                                  out_hbm.at[row_tile(dst_ref[entry * MOE_TM + r])],
                                  ssem.at[sl]).start(priority=SCATTER_DMA_PRIORITY)

    def wait_gather(sl):
        pltpu.make_async_copy(hx_hbm.at[pl.ds(0, MOE_TM * ROW_CHUNKS)], xbuf.at[sl], gsem.at[sl]).wait()

    def wait_scatter(sl):
        pltpu.make_async_copy(obuf.at[sl], out_hbm.at[pl.ds(0, MOE_TM * ROW_CHUNKS)], ssem.at[sl]).wait()

    @pl.when(k == 0)
    def _():
        start_gather(0, 0)
        start_gather(1, 1)
        obuf[1] = jnp.zeros((MOE_TM * ROW_CHUNKS, V7X_LANES), F32)
        pltpu.make_async_copy(obuf.at[1], out_hbm.at[pl.ds(SEQ * ROW_CHUNKS, MOE_TM * ROW_CHUNKS)],
                              ssem.at[0]).start()

    prev = jnp.maximum(k - 1, 0)

    @pl.when((k < nused) & ((k == 0) | (ea_ref[k] != ea_ref[prev])))
    def _():
        wup_a[:, :D_EXPERT] = wga_ref[0].astype(BF16)
        wup_a[:, D_EXPERT:] = wua_ref[0].astype(BF16)
        wdn_a[...] = wda_ref[0].astype(BF16)

    @pl.when((k < nused) & ((k == 0) | (eb_ref[k] != eb_ref[prev])))
    def _():
        wup_b[:, :D_EXPERT] = wgb_ref[0].astype(BF16)
        wup_b[:, D_EXPERT:] = wub_ref[0].astype(BF16)
        wdn_b[...] = wdb_ref[0].astype(BF16)

    @pl.when(k < nused)
    def _():
        wait_gather(xslot)
        h2 = _load_rows(xbuf, MOE_TM, (xslot,))
        start_gather(k + 2, (k + 2) % 3)
        start_scatter(k, other)
        t = _rms(h2, gf_ref[...]).astype(BF16)
        y = jnp.zeros((MOE_TM, D_MODEL), F32)
        for col, wup, wdn in ((0, wup_a, wdn_a), (1, wup_b, wdn_b)):
            gate = gates_ref[:, col:col + 1]
            gu = _dot(t, wup[...])
            hid = jax.nn.silu(gu[:, :D_EXPERT]) * gu[:, D_EXPERT:] * gate
            y = y + _dot(hid.astype(BF16), wdn[...])
        wait_scatter(slot)
        _store_rows(obuf, h2 + y, (slot,))

    @pl.when(k == nused - 1)
    def _():
        start_scatter(k + 1, slot)
        wait_gather((k + 1) % 3)
        wait_gather((k + 2) % 3)
        wait_scatter(other)
        wait_scatter(slot)


def _moe(hx, gates, g_ffn, w_gate, w_up, w_down, src, dst, ea, eb, nused):
    def wspec(shape, which):
        if which == 0:
            return pl.BlockSpec((1,) + shape, lambda k, s, d, a, b, n: (a[k], 0, 0))
        return pl.BlockSpec((1,) + shape, lambda k, s, d, a, b, n: (b[k], 0, 0))

    up_shape = (D_MODEL, D_EXPERT)
    down_shape = (D_EXPERT, D_MODEL)
    grid_spec = pltpu.PrefetchScalarGridSpec(
        num_scalar_prefetch=5,
        grid=(MOE_TILES,),
        in_specs=[
            pl.BlockSpec(memory_space=pl.ANY),
            pl.BlockSpec((MOE_TM, V7X_LANES), lambda k, s, d, a, b, n: (k, 0)),
            pl.BlockSpec((1, D_MODEL), lambda k, s, d, a, b, n: (0, 0)),
            wspec(up_shape, 0), wspec(up_shape, 0), wspec(down_shape, 0),
            wspec(up_shape, 1), wspec(up_shape, 1), wspec(down_shape, 1),
        ],
        out_specs=pl.BlockSpec(memory_space=pl.ANY),
        scratch_shapes=[
            pltpu.VMEM((3, MOE_TM * ROW_CHUNKS, V7X_LANES), F32),
            pltpu.VMEM((2, MOE_TM * ROW_CHUNKS, V7X_LANES), F32),
            pltpu.VMEM((D_MODEL, 2 * D_EXPERT), BF16),
            pltpu.VMEM((D_EXPERT, D_MODEL), BF16),
            pltpu.VMEM((D_MODEL, 2 * D_EXPERT), BF16),
            pltpu.VMEM((D_EXPERT, D_MODEL), BF16),
            pltpu.SemaphoreType.DMA((3,)),
            pltpu.SemaphoreType.DMA((2,)),
        ],
    )
    return pl.pallas_call(
        _moe_kernel,
        grid_spec=grid_spec,
        out_shape=jax.ShapeDtypeStruct((H_ROWS * ROW_CHUNKS, V7X_LANES), F32),
        compiler_params=_cparams("arbitrary"),
        name="routed_moe",
    )(src, dst, ea, eb, nused, hx, gates, g_ffn.reshape(1, D_MODEL),
      w_gate, w_up, w_down, w_gate, w_up, w_down)


def _route_tables(meta):
    bucket = meta[0].astype(jnp.int32)
    ids = jnp.arange(N_BUCKETS, dtype=jnp.int32)
    counts = jnp.sum((bucket[:, None] == ids[None, :]).astype(jnp.int32), axis=0)
    ntile = (counts + MOE_TM - 1) // MOE_TM
    pad = ntile * MOE_TM - counts
    tile_end = jnp.cumsum(ntile)
    nused = tile_end[-1]
    dummy_key = jnp.where(jnp.arange(MOE_TM - 1, dtype=jnp.int32)[None, :] < pad[:, None],
                          ids[:, None], N_BUCKETS)
    keys = jnp.concatenate([bucket, dummy_key.reshape(-1)])
    vals = jnp.concatenate([jnp.arange(SEQ, dtype=jnp.int32),
                            jnp.full((N_BUCKETS * (MOE_TM - 1),), SEQ, jnp.int32)])
    nslot = MOE_TILES * MOE_TM
    zpad = jnp.zeros((N_BUCKETS * (MOE_TM - 1),), F32)
    _, tok, ga, gb = lax.sort((keys, vals, jnp.concatenate([meta[1], zpad]), jnp.concatenate([meta[2], zpad])),
                              num_keys=1, is_stable=True)
    tok = tok[:nslot]
    gates = jnp.pad(jnp.stack([ga[:nslot], gb[:nslot]], axis=1), ((0, 0), (0, V7X_LANES - 2)))
    valid = tok < SEQ
    slot = jnp.arange(MOE_TILES * MOE_TM, dtype=jnp.int32)
    dump = SEQ + ((slot // MOE_TM) % 2) * MOE_TM + slot % MOE_TM
    src = jnp.concatenate([jnp.where(valid, tok, 0), jnp.zeros((2 * MOE_TM,), jnp.int32)])
    pseudo = SEQ + MOE_TM + jnp.arange(MOE_TM, dtype=jnp.int32)
    dst = jnp.concatenate([pseudo, jnp.where(valid, tok, dump)])
    tiles = jnp.arange(MOE_TILES, dtype=jnp.int32)
    tile_bucket = jnp.minimum(jnp.sum((tiles[:, None] >= tile_end[None, :]).astype(jnp.int32), axis=1),
                              N_BUCKETS - 1)
    onehot = (tile_bucket[:, None] == ids[None, :]).astype(jnp.int32)
    base = (np.arange(N_BUCKETS) // N_PAIRS) * MOE_EPG
    ea = jnp.sum(onehot * jnp.asarray(base + np.asarray(SLOT_A)[np.arange(N_BUCKETS) % N_PAIRS],
                                      jnp.int32)[None, :], axis=1)
    eb = jnp.sum(onehot * jnp.asarray(base + np.asarray(SLOT_B)[np.arange(N_BUCKETS) % N_PAIRS],
                                      jnp.int32)[None, :], axis=1)
    return (src.astype(jnp.int32), dst.astype(jnp.int32), ea.astype(jnp.int32), eb.astype(jnp.int32),
            nused.reshape(1).astype(jnp.int32), gates)


def _final_norm_kernel(h_ref, g_ref, o_ref):
    o_ref[...] = _rms(_load_rows(h_ref, o_ref.shape[0]), g_ref[...])


def _final_norm(h, g):
    tb = PROJ_TM
    return pl.pallas_call(
        _final_norm_kernel,
        grid=(SEQ // tb,),
        in_specs=[_h_spec(tb, True),
                  pl.BlockSpec((1, D_MODEL), lambda i: (0, 0))],
        out_specs=pl.BlockSpec((tb, D_MODEL), lambda i: (i, 0)),
        out_shape=jax.ShapeDtypeStruct((SEQ, D_MODEL), F32),
        compiler_params=_cparams("parallel"),
        name="final_norm",
    )(h, g.reshape(1, D_MODEL))


def kernel(x, mem, ln_mix, ln_cross, ln_mem, ln_ffn, ln_final, rel_table, even_w_in, even_w_out,
           sgu_ln_g, sgu_ln_b, sgu_w, sgu_b, attn_sink, odd_w_in, odd_w_out, xq_w, xkv_w, xo_w,
           router_group_w, router_group_b, router_expert_w, router_expert_b,
           expert_w_gate, expert_w_up, expert_w_down):
    h = x.reshape(SEQ, D_MODEL)
    mem2 = mem.reshape(MEM_LEN, D_MODEL)
    bias_even = _band_bias(rel_table, B_BLOCK, B_HALF_WINDOW, 1)
    bias_odd = [_band_bias(rel_table, C_BLOCK, window // 2 // dil, dil) for window, dil in C_PAIRS]

    for layer in range(DEPTH):
        i = layer // 2
        tiled = layer > 0
        if layer % 2 == 0:
            z = _proj(h, ln_mix[layer], even_w_in[i].astype(BF16), rows=SEQ, tm=EVEN_TB, tn=EVEN_IN,
                      gelu_cols=2 * A_WIDTH, tiled=tiled)
            h = _even_mix(z, h, sgu_ln_g[i], sgu_ln_b[i], sgu_w[i], sgu_b[i], bias_even,
                          attn_sink[i], even_w_out[i].astype(BF16), tiled=tiled)
        else:
            w_in = odd_w_in[i].astype(BF16)
            outs, lses = [], []
            for gi, (_, dil) in enumerate(C_PAIRS):
                zg = _proj(h, ln_mix[layer], w_in[:, gi * 3 * C_WIDTH:(gi + 1) * 3 * C_WIDTH],
                           rows=SEQ, tm=PROJ_TM, tn=PROJ_TN, dil=dil, tiled=tiled)
                o, lse = _dil_attn(zg, bias_odd[gi], dil)
                outs.append(o)
                lses.append(lse)
            h = _combine(outs, lses, h, odd_w_out[i].astype(BF16), tiled=tiled)

        kv = _proj(mem2, ln_mem[layer], xkv_w[layer].astype(BF16), rows=MEM_LEN, tm=MEM_LEN,
                   tn=2 * X_WIDTH)
        wr_t = jnp.zeros((ROUTER_ROWS, D_MODEL), F32)
        wr_t = wr_t.at[:MOE_GROUPS].set(router_group_w[layer].T)
        wr_t = wr_t.at[MOE_GROUPS:MOE_GROUPS + N_EXPERTS].set(
            router_expert_w[layer].reshape(D_MODEL, N_EXPERTS).T)
        br = jnp.zeros((ROUTER_ROWS, 1), F32)
        br = br.at[:MOE_GROUPS, 0].set(router_group_b[layer])
        br = br.at[MOE_GROUPS:MOE_GROUPS + N_EXPERTS, 0].set(router_expert_b[layer].reshape(N_EXPERTS))
        wr_hi = wr_t.astype(BF16)
        wr_split = jnp.stack([wr_hi, (wr_t - wr_hi.astype(F32)).astype(BF16)])
        hx, meta = _cross_router(h, ln_cross[layer], xq_w[layer].astype(BF16), kv,
                                 xo_w[layer].astype(BF16), ln_ffn[layer], wr_split, br)

        src, dst, ea, eb, nused, gates = _route_tables(meta)
        h = _moe(hx, gates, ln_ffn[layer],
                 expert_w_gate[layer].reshape(N_EXPERTS, D_MODEL, D_EXPERT),
                 expert_w_up[layer].reshape(N_EXPERTS, D_MODEL, D_EXPERT),
                 expert_w_down[layer].reshape(N_EXPERTS, D_EXPERT, D_MODEL),
                 src, dst, ea, eb, nused)

    return _final_norm(h, ln_final).reshape(1, SEQ, D_MODEL)
```

```python
import functools
import math

import numpy as np
import jax
import jax.numpy as jnp
from jax import lax
from jax.experimental import pallas as pl
from jax.experimental.pallas import tpu as pltpu

F32 = jnp.float32
BF16 = jnp.bfloat16

D_MODEL = 1024
SEQ = 16384
DEPTH = 4
MEM_LEN = 256
EPS = 1e-6
NEG_INF = -1e30

A_GROUPS = 4
A_CH = 128
A_WIDTH = A_GROUPS * A_CH
A_CHUNK = 128
B_HEADS = 8
B_KV_HEADS = 2
B_Q_PER_KV = B_HEADS // B_KV_HEADS
B_HEAD_DIM = 64
B_WIDTH = B_HEADS * B_HEAD_DIM
B_KV_WIDTH = B_KV_HEADS * B_HEAD_DIM
B_HALF_WINDOW = 128
B_BLOCK = 128
EVEN_IN = 2 * A_WIDTH + B_WIDTH + 2 * B_KV_WIDTH
EVEN_Q0 = 2 * A_WIDTH
EVEN_K0 = EVEN_Q0 + B_WIDTH
EVEN_V0 = EVEN_K0 + B_KV_WIDTH

C_PAIRS = ((128, 1), (512, 4), (2048, 16))
C_GROUPS = len(C_PAIRS)
C_HEADS = 8
C_HEAD_DIM = 128
C_WIDTH = C_HEADS * C_HEAD_DIM
C_BLOCK = 64
ODD_IN = C_GROUPS * 3 * C_WIDTH

REL_BUCKETS = 32
REL_MAX_DIST = 1024
REL_HEADS = 8

X_HEADS = 4
X_HEAD_DIM = 128
X_WIDTH = X_HEADS * X_HEAD_DIM

MOE_GROUPS = 4
MOE_EPG = 4
N_EXPERTS = MOE_GROUPS * MOE_EPG
D_EXPERT = 512
SLOT_A = (0, 0, 0, 1, 1, 3)
SLOT_B = (1, 2, 3, 3, 2, 2)
N_PAIRS = len(SLOT_A)
N_BUCKETS = MOE_GROUPS * N_PAIRS

V7X_LANES = 128
ROW_CHUNKS = D_MODEL // V7X_LANES
V7X_VMEM_BYTES = 64 * 1024 * 1024
VMEM_LIMIT = 56 * 1024 * 1024

PROJ_TM = 1024
PROJ_TN = 1024
EVEN_TB = 512
EVEN_STACK = 2
ODD_TILE = PROJ_TM
ODD_BLOCKS = ODD_TILE // C_BLOCK
ODD_INTERLEAVE = 4
COMB_TB = 512
CROSS_TB = 512
MOE_TM = 256
SCATTER_DMA_PRIORITY = 1
ROUTER_ROWS = 32
MOE_TILES = (SEQ + N_BUCKETS * (MOE_TM - 1)) // MOE_TM
H_ROWS = SEQ + 2 * MOE_TM


def _cparams(*sem):
    return pltpu.CompilerParams(dimension_semantics=sem, vmem_limit_bytes=VMEM_LIMIT)


def _rms(x, g):
    return x * lax.rsqrt(jnp.mean(x * x, axis=-1, keepdims=True) + EPS) * g


def _dot(a, b):
    return jnp.dot(a, b, preferred_element_type=F32)


def _dot_nt(a, b):
    return lax.dot_general(a, b, (((1,), (1,)), ((), ())), preferred_element_type=F32)


def _load_rows(ref, n, lead=()):
    return jnp.concatenate([ref[lead + (pl.ds(c, n, stride=ROW_CHUNKS), slice(None))]
                            for c in range(ROW_CHUNKS)], axis=1)


def _store_rows(ref, val, lead=()):
    n = val.shape[0]
    for c in range(ROW_CHUNKS):
        ref[lead + (pl.ds(c, n, stride=ROW_CHUNKS), slice(None))] = val[:, c * V7X_LANES:(c + 1) * V7X_LANES]


def _h_spec(tb, tiled, index=lambda i: i):
    if tiled:
        return pl.BlockSpec((tb * ROW_CHUNKS, V7X_LANES), lambda i, *_: (index(i), 0))
    return pl.BlockSpec((tb, D_MODEL), lambda i, *_: (index(i), 0))


def _h_load(ref, tb, tiled):
    return _load_rows(ref, tb) if tiled else ref[...]


def _proj_kernel(h_ref, g_ref, w_ref, o_ref, xn_ref, *scratch, gelu_cols, dil, tiled):
    tm = xn_ref.shape[0]
    seg = tm // dil

    @pl.when(pl.program_id(1) == 0)
    def _():
        xf = _rms(_h_load(h_ref, tm, tiled), g_ref[...])
        if dil == 1:
            xn_ref[...] = xf.astype(BF16)
        else:
            xs_ref, = scratch
            for c in range(D_MODEL // V7X_LANES):
                xs_ref[c] = xf[:, c * V7X_LANES:(c + 1) * V7X_LANES]
            for r in range(dil):
                for c in range(D_MODEL // V7X_LANES):
                    xn_ref[r * seg:(r + 1) * seg, c * V7X_LANES:(c + 1) * V7X_LANES] = (
                        xs_ref[c, pl.ds(r, seg, stride=dil), :].astype(BF16))

    acc = _dot(xn_ref[...], w_ref[...].astype(BF16))
    if gelu_cols:
        o_ref[0, :, :gelu_cols] = jax.nn.gelu(acc[:, :gelu_cols]).astype(o_ref.dtype)
        o_ref[0, :, gelu_cols:] = acc[:, gelu_cols:].astype(o_ref.dtype)
    else:
        for r in range(dil):
            o_ref[r] = acc[r * seg:(r + 1) * seg].astype(o_ref.dtype)


def _proj(h, g, w, *, rows, tm, tn, n, w_lead=0, w_col0=0, gelu_cols=0, dil=1, tiled=False):
    seg = tm // dil
    return pl.pallas_call(
        functools.partial(_proj_kernel, gelu_cols=gelu_cols, dil=dil, tiled=tiled),
        grid=(rows // tm, n // tn),
        in_specs=[
            _h_spec(tm, tiled),
            pl.BlockSpec((1, D_MODEL), lambda i, j: (0, 0)),
            pl.BlockSpec((None, D_MODEL, tn), lambda i, j: (w_lead, 0, w_col0 + j)),
        ],
        out_specs=pl.BlockSpec((dil, seg, tn), lambda i, j: (0, i, j)),
        out_shape=jax.ShapeDtypeStruct((dil, rows // dil, n), BF16),
        scratch_shapes=[pltpu.VMEM((tm, D_MODEL), BF16)] + (
            [pltpu.VMEM((D_MODEL // V7X_LANES, tm, V7X_LANES), F32)] if dil > 1 else []),
        compiler_params=_cparams("parallel", "arbitrary"),
        name=f"norm_proj_d{dil}",
    )(h, g.reshape(1, D_MODEL), w)


def _t5_bucket_np(rel):
    nb = REL_BUCKETS // 2
    max_exact = nb // 2
    ret = np.where(rel > 0, nb, 0)
    n = np.abs(rel)
    nf = np.maximum(n, 1).astype(np.float32)
    large = max_exact + (np.log(nf / np.float32(max_exact)) / np.float32(math.log(REL_MAX_DIST / max_exact))
                         * np.float32(nb - max_exact)).astype(np.int32)
    large = np.minimum(large, nb - 1)
    return (ret + np.where(n < max_exact, n, large)).astype(np.int32)


def _bias_kernel(table_ref, idx_ref, mask_ref, o_ref, *, block):
    idx = idx_ref[...]
    for h in range(REL_HEADS):
        acc = jnp.zeros(idx.shape, F32)
        for b in range(REL_BUCKETS):
            acc = jnp.where(idx == b, table_ref[b, h], acc)
        for v in range(3):
            o_ref[v, h * block:(h + 1) * block, :] = acc + mask_ref[v]


def _band_bias(table, block, half, dil):
    rel = np.arange(3 * block)[None, :] - block - np.arange(block)[:, None]
    band = np.abs(rel) <= half
    col = np.arange(3 * block)[None, :]
    masks = np.stack([band & (col >= block), band, band & (col < 2 * block)])
    add = np.where(masks, 0.0, NEG_INF).astype(np.float32)
    return pl.pallas_call(
        functools.partial(_bias_kernel, block=block),
        in_specs=[pl.BlockSpec(memory_space=pltpu.SMEM),
                  pl.BlockSpec(memory_space=pltpu.VMEM),
                  pl.BlockSpec(memory_space=pltpu.VMEM)],
        out_specs=pl.BlockSpec(memory_space=pltpu.VMEM),
        out_shape=jax.ShapeDtypeStruct((3, REL_HEADS * block, 3 * block), F32),
        name=f"rel_bias_d{dil}",
    )(table, jnp.asarray(_t5_bucket_np(rel * dil)), jnp.asarray(add))


def _even_mix_kernel(z_ref, kvp_ref, kvn_ref, h_ref, lng_ref, lnb_ref, ws_ref, bs_ref, bias_ref,
                     sink_ref, wout_ref, o_ref, kv_scr, y_scr, *, tiled):
    i = pl.program_id(0)
    nsub = EVEN_TB // B_BLOCK
    nblk = SEQ // B_BLOCK
    kv_scr[0:B_BLOCK] = kvp_ref[...]
    kv_scr[B_BLOCK:B_BLOCK + EVEN_TB] = z_ref[0, :, EVEN_K0:EVEN_IN]
    kv_scr[B_BLOCK + EVEN_TB:] = kvn_ref[...]
    lng = lng_ref[...]
    lnb = lnb_ref[...]
    for s in range(nsub):
        r0 = s * B_BLOCK
        gb = i * nsub + s
        sel = jnp.where(gb == 0, 0, jnp.where(gb == nblk - 1, 2, 1))
        u = z_ref[0, r0:r0 + A_CHUNK, 0:A_WIDTH].astype(F32)
        va = z_ref[0, r0:r0 + A_CHUNK, A_WIDTH:2 * A_WIDTH].astype(F32)
        mu = jnp.mean(va, axis=-1, keepdims=True)
        vc = va - mu
        var = jnp.mean(vc * vc, axis=-1, keepdims=True)
        vn = (vc * lax.rsqrt(var + EPS) * lng + lnb).astype(BF16)
        for g in range(A_GROUPS):
            c0 = g * A_CH
            mixed = _dot(ws_ref[g], vn[:, c0:c0 + A_CH]) + bs_ref[g]
            y_scr[r0:r0 + A_CHUNK, c0:c0 + A_CH] = (u[:, c0:c0 + A_CH] * mixed).astype(BF16)
        units = [(kh, kh * B_Q_PER_KV + half * EVEN_STACK)
                 for kh in range(B_KV_HEADS) for half in range(B_Q_PER_KV // EVEN_STACK)]
        lgs = []
        for kh, hd0 in units:
            kw = kv_scr[r0:r0 + 3 * B_BLOCK, kh * B_HEAD_DIM:(kh + 1) * B_HEAD_DIM]
            q = jnp.concatenate(
                [z_ref[0, r0:r0 + B_BLOCK,
                       EVEN_Q0 + (hd0 + g) * B_HEAD_DIM:EVEN_Q0 + (hd0 + g + 1) * B_HEAD_DIM]
                 for g in range(EVEN_STACK)], axis=0)
            lgs.append(_dot_nt(q, kw))
        lgs = [lg * (B_HEAD_DIM ** -0.5) + bias_ref[sel, hd0 * B_BLOCK:(hd0 + EVEN_STACK) * B_BLOCK, :]
               for lg, (_, hd0) in zip(lgs, units)]
        sks = [sink_ref[hd0 * B_BLOCK:(hd0 + EVEN_STACK) * B_BLOCK, :] for _, hd0 in units]
        ms = [jnp.maximum(jnp.max(lg, axis=-1, keepdims=True), sk) for lg, sk in zip(lgs, sks)]
        ps = [jnp.exp(lg - m) for lg, m in zip(lgs, ms)]
        dens = [jnp.sum(p, axis=-1, keepdims=True) + jnp.exp(sk - m) for p, sk, m in zip(ps, sks, ms)]
        os_ = []
        for p, (kh, _) in zip(ps, units):
            vw = kv_scr[r0:r0 + 3 * B_BLOCK,
                        B_KV_WIDTH + kh * B_HEAD_DIM:B_KV_WIDTH + (kh + 1) * B_HEAD_DIM]
            os_.append(_dot(p.astype(BF16), vw))
        for o, den, (_, hd0) in zip(os_, dens, units):
            o = o * (1.0 / den)
            for g in range(EVEN_STACK):
                c0 = A_WIDTH + (hd0 + g) * B_HEAD_DIM
                y_scr[r0:r0 + B_BLOCK, c0:c0 + B_HEAD_DIM] = o[g * B_BLOCK:(g + 1) * B_BLOCK].astype(BF16)
    o_ref[...] = _h_load(h_ref, EVEN_TB, tiled) + _dot(y_scr[...], wout_ref[...])


def _even_mix(z, h, ln_g, ln_b, w_s, b_s, bias, sink, w_out, *, tiled):
    nsub = EVEN_TB // B_BLOCK
    nblk = SEQ // B_BLOCK
    kv_cb = EVEN_K0 // (2 * B_KV_WIDTH)
    sink_col = jnp.broadcast_to(sink.reshape(B_HEADS, 1, 1), (B_HEADS, B_BLOCK, 1)).reshape(
        B_HEADS * B_BLOCK, 1)
    return pl.pallas_call(
        functools.partial(_even_mix_kernel, tiled=tiled),
        grid=(SEQ // EVEN_TB,),
        in_specs=[
            pl.BlockSpec((1, EVEN_TB, EVEN_IN), lambda i: (0, i, 0)),
            pl.BlockSpec((None, B_BLOCK, 2 * B_KV_WIDTH),
                         lambda i: (0, jnp.maximum(i * nsub - 1, 0), kv_cb)),
            pl.BlockSpec((None, B_BLOCK, 2 * B_KV_WIDTH),
                         lambda i: (0, jnp.minimum((i + 1) * nsub, nblk - 1), kv_cb)),
            _h_spec(EVEN_TB, tiled),
            pl.BlockSpec((1, A_WIDTH), lambda i: (0, 0)),
            pl.BlockSpec((1, A_WIDTH), lambda i: (0, 0)),
            pl.BlockSpec((A_GROUPS, A_CHUNK, A_CHUNK), lambda i: (0, 0, 0)),
            pl.BlockSpec((A_GROUPS, A_CHUNK, A_CH), lambda i: (0, 0, 0)),
            pl.BlockSpec((3, B_HEADS * B_BLOCK, 3 * B_BLOCK), lambda i: (0, 0, 0)),
            pl.BlockSpec((B_HEADS * B_BLOCK, 1), lambda i: (0, 0)),
            pl.BlockSpec((A_WIDTH + B_WIDTH, D_MODEL), lambda i: (0, 0)),
        ],
        out_specs=pl.BlockSpec((EVEN_TB, D_MODEL), lambda i: (i, 0)),
        out_shape=jax.ShapeDtypeStruct((SEQ, D_MODEL), F32),
        scratch_shapes=[
            pltpu.VMEM((EVEN_TB + 2 * B_BLOCK, 2 * B_KV_WIDTH), BF16),
            pltpu.VMEM((EVEN_TB, A_WIDTH + B_WIDTH), BF16),
        ],
        compiler_params=_cparams("parallel"),
        name="even_mixer",
    )(z, z, z, h, ln_g.reshape(1, A_WIDTH), ln_b.reshape(1, A_WIDTH), w_s.astype(BF16),
      jnp.broadcast_to(b_s[:, :, None], (A_GROUPS, A_CHUNK, A_CH)), bias, sink_col, w_out)


def _dil_attn_kernel(zc_ref, kp_ref, kn_ref, vp_ref, vn_ref, bias_ref, o_ref, lse_ref,
                     k_scr, v_scr, o_scr, lse_scr, *, dil):
    t = pl.program_id(0)
    seg = ODD_TILE // dil
    nsb = seg // C_BLOCK
    nblk = SEQ // dil // C_BLOCK
    k_scr[:, 0:C_BLOCK] = kp_ref[...]
    k_scr[:, C_BLOCK:C_BLOCK + seg] = zc_ref[:, :, C_WIDTH:2 * C_WIDTH]
    k_scr[:, C_BLOCK + seg:] = kn_ref[...]
    v_scr[:, 0:C_BLOCK] = vp_ref[...]
    v_scr[:, C_BLOCK:C_BLOCK + seg] = zc_ref[:, :, 2 * C_WIDTH:3 * C_WIDTH]
    v_scr[:, C_BLOCK + seg:] = vn_ref[...]
    lane = lax.broadcasted_iota(jnp.int32, (C_BLOCK, V7X_LANES), 1)

    def body(it, carry):
        blocks = []
        for u in range(ODD_INTERLEAVE):
            n = it * ODD_INTERLEAVE + u
            r = n // nsb
            s = n % nsb
            r0 = pl.multiple_of(s * C_BLOCK, C_BLOCK)
            gb = t * nsb + s
            sel = jnp.where(gb == 0, 0, jnp.where(gb == nblk - 1, 2, 1))
            rows = (pl.ds(s * (C_BLOCK * dil) + r, C_BLOCK, stride=dil) if dil > 1
                    else pl.ds(r0, C_BLOCK))
            blocks.append((r, r0, sel, rows))
        lgs = []
        for r, r0, _, _ in blocks:
            for hd in range(C_HEADS):
                c0 = hd * C_HEAD_DIM
                q = zc_ref[r, pl.ds(r0, C_BLOCK), c0:c0 + C_HEAD_DIM]
                kw = k_scr[r, pl.ds(r0, 3 * C_BLOCK), c0:c0 + C_HEAD_DIM]
                lgs.append(_dot_nt(q, kw))
        lg = [jnp.concatenate(lgs[u * C_HEADS:(u + 1) * C_HEADS], axis=0) * (C_HEAD_DIM ** -0.5)
              + bias_ref[blk[2]] for u, blk in enumerate(blocks)]
        m = [jnp.max(x, axis=-1, keepdims=True) for x in lg]
        p = [jnp.exp(x - mm) for x, mm in zip(lg, m)]
        den = [jnp.sum(x, axis=-1, keepdims=True) for x in p]
        inv = [1.0 / d for d in den]
        lse = [mm + jnp.log(d) for mm, d in zip(m, den)]
        pb = [x.astype(BF16) for x in p]
        outs = []
        for u, (r, r0, _, _) in enumerate(blocks):
            for hd in range(C_HEADS):
                c0 = hd * C_HEAD_DIM
                vw = v_scr[r, pl.ds(r0, 3 * C_BLOCK), c0:c0 + C_HEAD_DIM]
                outs.append(_dot(pb[u][hd * C_BLOCK:(hd + 1) * C_BLOCK], vw))
        for u, (_, _, _, rows) in enumerate(blocks):
            lse_tile = jnp.zeros((C_BLOCK, V7X_LANES), F32)
            for hd in range(C_HEADS):
                o_scr[hd, rows, :] = outs[u * C_HEADS + hd] * inv[u][hd * C_BLOCK:(hd + 1) * C_BLOCK]
                lse_tile = jnp.where(lane == hd, lse[u][hd * C_BLOCK:(hd + 1) * C_BLOCK], lse_tile)
            lse_scr[rows, :] = lse_tile
        return carry

    lax.fori_loop(0, ODD_BLOCKS // ODD_INTERLEAVE, body, 0)
    for hd in range(C_HEADS):
        o_ref[:, hd * C_HEAD_DIM:(hd + 1) * C_HEAD_DIM] = o_scr[hd].astype(o_ref.dtype)
    lse_ref[...] = lse_scr[...]


def _dil_attn(zg, bias, dil):
    seg = ODD_TILE // dil
    nsb = seg // C_BLOCK
    last = SEQ // dil // C_BLOCK - 1

    def halo(j, nxt):
        if nxt:
            return pl.BlockSpec((dil, C_BLOCK, C_WIDTH),
                                lambda t: (0, jnp.minimum((t + 1) * nsb, last), j))
        return pl.BlockSpec((dil, C_BLOCK, C_WIDTH), lambda t: (0, jnp.maximum(t * nsb - 1, 0), j))

    return pl.pallas_call(
        functools.partial(_dil_attn_kernel, dil=dil),
        grid=(SEQ // ODD_TILE,),
        in_specs=[pl.BlockSpec((dil, seg, 3 * C_WIDTH), lambda t: (0, t, 0)),
                  halo(1, False), halo(1, True), halo(2, False), halo(2, True),
                  pl.BlockSpec((3, C_HEADS * C_BLOCK, 3 * C_BLOCK), lambda t: (0, 0, 0))],
        out_specs=[pl.BlockSpec((ODD_TILE, C_WIDTH), lambda t: (t, 0)),
                   pl.BlockSpec((ODD_TILE, V7X_LANES), lambda t: (t, 0))],
        out_shape=[jax.ShapeDtypeStruct((SEQ, C_WIDTH), BF16),
                   jax.ShapeDtypeStruct((SEQ, V7X_LANES), F32)],
        scratch_shapes=[pltpu.VMEM((dil, seg + 2 * C_BLOCK, C_WIDTH), BF16),
                        pltpu.VMEM((dil, seg + 2 * C_BLOCK, C_WIDTH), BF16),
                        pltpu.VMEM((C_HEADS, ODD_TILE, C_HEAD_DIM), F32),
                        pltpu.VMEM((ODD_TILE, V7X_LANES), F32)],
        compiler_params=_cparams("parallel"),
        name=f"dilated_attn_d{dil}",
    )(zg, zg, zg, zg, zg, bias)


def _combine_kernel(o0_ref, o1_ref, o2_ref, l0_ref, l1_ref, l2_ref, h_ref, wout_ref, out_ref, y_scr,
                    *, tiled):
    l0 = l0_ref[...]
    l1 = l1_ref[...]
    l2 = l2_ref[...]
    m = jnp.maximum(jnp.maximum(l0, l1), l2)
    e0 = jnp.exp(l0 - m)
    e1 = jnp.exp(l1 - m)
    e2 = jnp.exp(l2 - m)
    tot = e0 + e1 + e2
    w0 = e0 / tot
    w1 = e1 / tot
    w2 = e2 / tot
    for hd in range(C_HEADS):
        c0 = hd * C_HEAD_DIM
        y = (w0[:, hd:hd + 1] * o0_ref[:, c0:c0 + C_HEAD_DIM].astype(F32)
             + w1[:, hd:hd + 1] * o1_ref[:, c0:c0 + C_HEAD_DIM].astype(F32)
             + w2[:, hd:hd + 1] * o2_ref[:, c0:c0 + C_HEAD_DIM].astype(F32))
        y_scr[:, c0:c0 + C_HEAD_DIM] = y.astype(BF16)
    out_ref[...] = _h_load(h_ref, COMB_TB, tiled) + _dot(y_scr[...], wout_ref[...])


def _combine(outs, lses, h, w_out, *, tiled):
    blk_o = pl.BlockSpec((COMB_TB, C_WIDTH), lambda i: (i, 0))
    blk_l = pl.BlockSpec((COMB_TB, V7X_LANES), lambda i: (i, 0))
    return pl.pallas_call(
        functools.partial(_combine_kernel, tiled=tiled),
        grid=(SEQ // COMB_TB,),
        in_specs=[blk_o, blk_o, blk_o, blk_l, blk_l, blk_l,
                  _h_spec(COMB_TB, tiled),
                  pl.BlockSpec((C_WIDTH, D_MODEL), lambda i: (0, 0))],
        out_specs=pl.BlockSpec((COMB_TB, D_MODEL), lambda i: (i, 0)),
        out_shape=jax.ShapeDtypeStruct((SEQ, D_MODEL), F32),
        scratch_shapes=[pltpu.VMEM((COMB_TB, C_WIDTH), BF16)],
        compiler_params=_cparams("parallel"),
        name="group_combine_proj",
    )(*outs, *lses, h, w_out)


def _cross_kernel(h_ref, gx_ref, wq_ref, kv_ref, wo_ref, gf_ref, wr_ref, br_ref,
                  hx_ref, meta_ref, o_scr):
    h = h_ref[...]
    q = _dot(_rms(h, gx_ref[...]).astype(BF16), wq_ref[...]).astype(BF16)
    for hd in range(X_HEADS):
        c0 = hd * X_HEAD_DIM
        lg = _dot_nt(q[:, c0:c0 + X_HEAD_DIM], kv_ref[0, :, c0:c0 + X_HEAD_DIM]) * (X_HEAD_DIM ** -0.5)
        m = jnp.max(lg, axis=-1, keepdims=True)
        p = jnp.exp(lg - m)
        den = jnp.sum(p, axis=-1, keepdims=True)
        o = _dot(p.astype(BF16), kv_ref[0, :, X_WIDTH + c0:X_WIDTH + c0 + X_HEAD_DIM]) / den
        o_scr[:, c0:c0 + X_HEAD_DIM] = o.astype(BF16)
    h2 = h + _dot(o_scr[...], wo_ref[...])
    _store_rows(hx_ref, h2)

    t = _rms(h2, gf_ref[...])
    t_hi = t.astype(BF16)
    t_lo = (t - t_hi.astype(F32)).astype(BF16)
    lt = (_dot_nt(wr_ref[0], t_hi) + _dot_nt(wr_ref[0], t_lo) + _dot_nt(wr_ref[1], t_hi)) + br_ref[...]
    g = [lt[k:k + 1, :] for k in range(MOE_GROUPS)]
    gmax = jnp.maximum(jnp.maximum(g[0], g[1]), jnp.maximum(g[2], g[3]))
    grp = jnp.where(g[0] == gmax, 0, jnp.where(g[1] == gmax, 1, jnp.where(g[2] == gmax, 2, 3)))
    g_gate = 1.0 / (jnp.exp(g[0] - gmax) + jnp.exp(g[1] - gmax) + jnp.exp(g[2] - gmax)
                    + jnp.exp(g[3] - gmax))
    e = []
    for k in range(MOE_EPG):
        rows = [lt[MOE_GROUPS + gi * MOE_EPG + k:MOE_GROUPS + gi * MOE_EPG + k + 1, :]
                for gi in range(MOE_GROUPS)]
        e.append(jnp.where(grp == 0, rows[0], jnp.where(grp == 1, rows[1],
                                                         jnp.where(grp == 2, rows[2], rows[3]))))
    v1 = jnp.maximum(jnp.maximum(e[0], e[1]), jnp.maximum(e[2], e[3]))
    i1 = jnp.where(e[0] == v1, 0, jnp.where(e[1] == v1, 1, jnp.where(e[2] == v1, 2, 3)))
    r = [jnp.where(i1 == k, -jnp.inf, e[k]) for k in range(MOE_EPG)]
    v2 = jnp.maximum(jnp.maximum(r[0], r[1]), jnp.maximum(r[2], r[3]))
    i2 = jnp.where(r[0] == v2, 0, jnp.where(r[1] == v2, 1, jnp.where(r[2] == v2, 2, 3)))
    d = jnp.exp(v2 - v1)
    w1 = g_gate / (1.0 + d)
    w2 = g_gate * d / (1.0 + d)
    first_lo = i1 < i2
    lo = jnp.where(first_lo, i1, i2)
    hi = jnp.where(first_lo, i2, i1)
    w_lo = jnp.where(first_lo, w1, w2)
    w_hi = jnp.where(first_lo, w2, w1)
    pair = jnp.where(lo == 0, hi - 1, jnp.where(lo == 1, jnp.where(hi == 3, 3, 4), 5))
    w_a = jnp.where(lo == 2, w_hi, w_lo)
    w_b = jnp.where(lo == 2, w_lo, w_hi)
    bucket = (grp * N_PAIRS + pair).astype(F32)
    row = lax.broadcasted_iota(jnp.int32, (8, CROSS_TB), 0)
    meta_ref[...] = jnp.where(row == 0, bucket, jnp.where(row == 1, w_a, jnp.where(row == 2, w_b, 0.0)))


def _cross_router(h, g_cross, wq, kv, wo, g_ffn, wr_t, br):
    full = lambda shape: pl.BlockSpec(shape, lambda i: tuple(0 for _ in shape))
    return pl.pallas_call(
        _cross_kernel,
        grid=(SEQ // CROSS_TB,),
        in_specs=[
            pl.BlockSpec((CROSS_TB, D_MODEL), lambda i: (i, 0)),
            full((1, D_MODEL)),
            full((D_MODEL, X_WIDTH)),
            full((1, MEM_LEN, 2 * X_WIDTH)),
            full((X_WIDTH, D_MODEL)),
            full((1, D_MODEL)),
            full((2, ROUTER_ROWS, D_MODEL)),
            full((ROUTER_ROWS, 1)),
        ],
        out_specs=[_h_spec(CROSS_TB, True),
                   pl.BlockSpec((8, CROSS_TB), lambda i: (0, i))],
        out_shape=[jax.ShapeDtypeStruct((SEQ * ROW_CHUNKS, V7X_LANES), F32),
                   jax.ShapeDtypeStruct((8, SEQ), F32)],
        scratch_shapes=[pltpu.VMEM((CROSS_TB, X_WIDTH), BF16)],
        compiler_params=_cparams("parallel"),
        name="cross_attn_router",
    )(h, g_cross.reshape(1, D_MODEL), wq, kv, wo, g_ffn.reshape(1, D_MODEL), wr_t, br)


def _moe_kernel(src_ref, dst_ref, ea_ref, eb_ref, nused_ref,
                hx_hbm, gates_ref, gf_ref, wga_ref, wua_ref, wda_ref, wgb_ref, wub_ref, wdb_ref,
                out_hbm, xbuf, obuf, wup_a, wdn_a, wup_b, wdn_b, gsem, ssem):
    k = pl.program_id(0)
    nused = nused_ref[0]
    slot = k % 2
    other = 1 - slot
    xslot = k % 3

    def row_tile(idx):
        return pl.ds(pl.multiple_of(idx * ROW_CHUNKS, ROW_CHUNKS), ROW_CHUNKS)

    def start_gather(tile, sl):
        for r in range(MOE_TM):
            pltpu.make_async_copy(hx_hbm.at[row_tile(src_ref[tile * MOE_TM + r])],
                                  xbuf.at[sl, pl.ds(r * ROW_CHUNKS, ROW_CHUNKS)], gsem.at[sl]).start()

    def start_scatter(entry, sl):
        for r in range(MOE_TM):
            pltpu.make_async_copy(obuf.at[sl, pl.ds(r * ROW_CHUNKS, ROW_CHUNKS)],
                                  out_hbm.at[row_tile(dst_ref[entry * MOE_TM + r])],
                                  ssem.at[sl]).start(priority=SCATTER_DMA_PRIORITY)

    def wait_gather(sl):
        pltpu.make_async_copy(hx_hbm.at[pl.ds(0, MOE_TM * ROW_CHUNKS)], xbuf.at[sl], gsem.at[sl]).wait()

    def wait_scatter(sl):
        pltpu.make_async_copy(obuf.at[sl], out_hbm.at[pl.ds(0, MOE_TM * ROW_CHUNKS)], ssem.at[sl]).wait()

    @pl.when(k == 0)
    def _():
        start_gather(0, 0)
        start_gather(1, 1)
        obuf[1] = jnp.zeros((MOE_TM * ROW_CHUNKS, V7X_LANES), F32)
        pltpu.make_async_copy(obuf.at[1], out_hbm.at[pl.ds(SEQ * ROW_CHUNKS, MOE_TM * ROW_CHUNKS)],
                              ssem.at[0]).start()

    prev = jnp.maximum(k - 1, 0)

    @pl.when((k < nused) & ((k == 0) | (ea_ref[k] != ea_ref[prev])))
    def _():
        wup_a[:, :D_EXPERT] = wga_ref[0].astype(BF16)
        wup_a[:, D_EXPERT:] = wua_ref[0].astype(BF16)
        wdn_a[...] = wda_ref[0].astype(BF16)

    @pl.when((k < nused) & ((k == 0) | (eb_ref[k] != eb_ref[prev])))
    def _():
        wup_b[:, :D_EXPERT] = wgb_ref[0].astype(BF16)
        wup_b[:, D_EXPERT:] = wub_ref[0].astype(BF16)
        wdn_b[...] = wdb_ref[0].astype(BF16)

    @pl.when(k < nused)
    def _():
        wait_gather(xslot)
        h2 = _load_rows(xbuf, MOE_TM, (xslot,))
        start_gather(k + 2, (k + 2) % 3)
        start_scatter(k, other)
        t = _rms(h2, gf_ref[...]).astype(BF16)
        y = jnp.zeros((MOE_TM, D_MODEL), F32)
        for col, wup, wdn in ((0, wup_a, wdn_a), (1, wup_b, wdn_b)):
            gate = gates_ref[:, col:col + 1]
            gu = _dot(t, wup[...])
            hid = jax.nn.silu(gu[:, :D_EXPERT]) * gu[:, D_EXPERT:] * gate
            y = y + _dot(hid.astype(BF16), wdn[...])
        wait_scatter(slot)
        _store_rows(obuf, h2 + y, (slot,))

    @pl.when(k == nused - 1)
    def _():
        start_scatter(k + 1, slot)
        wait_gather((k + 1) % 3)
        wait_gather((k + 2) % 3)
        wait_scatter(other)
        wait_scatter(slot)


def _moe(hx, gates, g_ffn, w_gate, w_up, w_down, src, dst, ea, eb, nused):
    def wspec(shape, which):
        if which == 0:
            return pl.BlockSpec((1,) + shape, lambda k, s, d, a, b, n: (a[k], 0, 0))
        return pl.BlockSpec((1,) + shape, lambda k, s, d, a, b, n: (b[k], 0, 0))

    up_shape = (D_MODEL, D_EXPERT)
    down_shape = (D_EXPERT, D_MODEL)
    grid_spec = pltpu.PrefetchScalarGridSpec(
        num_scalar_prefetch=5,
        grid=(MOE_TILES,),
        in_specs=[
            pl.BlockSpec(memory_space=pl.ANY),
            pl.BlockSpec((MOE_TM, V7X_LANES), lambda k, s, d, a, b, n: (k, 0)),
            pl.BlockSpec((1, D_MODEL), lambda k, s, d, a, b, n: (0, 0)),
            wspec(up_shape, 0), wspec(up_shape, 0), wspec(down_shape, 0),
            wspec(up_shape, 1), wspec(up_shape, 1), wspec(down_shape, 1),
        ],
        out_specs=pl.BlockSpec(memory_space=pl.ANY),
        scratch_shapes=[
            pltpu.VMEM((3, MOE_TM * ROW_CHUNKS, V7X_LANES), F32),
            pltpu.VMEM((2, MOE_TM * ROW_CHUNKS, V7X_LANES), F32),
            pltpu.VMEM((D_MODEL, 2 * D_EXPERT), BF16),
            pltpu.VMEM((D_EXPERT, D_MODEL), BF16),
            pltpu.VMEM((D_MODEL, 2 * D_EXPERT), BF16),
            pltpu.VMEM((D_EXPERT, D_MODEL), BF16),
            pltpu.SemaphoreType.DMA((3,)),
            pltpu.SemaphoreType.DMA((2,)),
        ],
    )
    return pl.pallas_call(
        _moe_kernel,
        grid_spec=grid_spec,
        out_shape=jax.ShapeDtypeStruct((H_ROWS * ROW_CHUNKS, V7X_LANES), F32),
        compiler_params=_cparams("arbitrary"),
        name="routed_moe",
    )(src, dst, ea, eb, nused, hx, gates, g_ffn.reshape(1, D_MODEL),
      w_gate, w_up, w_down, w_gate, w_up, w_down)


def _route_tables(meta):
    bucket = meta[0].astype(jnp.int32)
    ids = jnp.arange(N_BUCKETS, dtype=jnp.int32)
    counts = jnp.sum((bucket[:, None] == ids[None, :]).astype(jnp.int32), axis=0)
    ntile = (counts + MOE_TM - 1) // MOE_TM
    pad = ntile * MOE_TM - counts
    tile_end = jnp.cumsum(ntile)
    nused = tile_end[-1]
    dummy_key = jnp.where(jnp.arange(MOE_TM - 1, dtype=jnp.int32)[None, :] < pad[:, None],
                          ids[:, None], N_BUCKETS)
    keys = jnp.concatenate([bucket, dummy_key.reshape(-1)])
    vals = jnp.concatenate([jnp.arange(SEQ, dtype=jnp.int32),
                            jnp.full((N_BUCKETS * (MOE_TM - 1),), SEQ, jnp.int32)])
    nslot = MOE_TILES * MOE_TM
    zpad = jnp.zeros((N_BUCKETS * (MOE_TM - 1),), F32)
    _, tok, ga, gb = lax.sort((keys, vals, jnp.concatenate([meta[1], zpad]), jnp.concatenate([meta[2], zpad])),
                              num_keys=1, is_stable=True)
    tok = tok[:nslot]
    gates = jnp.pad(jnp.stack([ga[:nslot], gb[:nslot]], axis=1), ((0, 0), (0, V7X_LANES - 2)))
    valid = tok < SEQ
    slot = jnp.arange(MOE_TILES * MOE_TM, dtype=jnp.int32)
    dump = SEQ + ((slot // MOE_TM) % 2) * MOE_TM + slot % MOE_TM
    src = jnp.concatenate([jnp.where(valid, tok, 0), jnp.zeros((2 * MOE_TM,), jnp.int32)])
    pseudo = SEQ + MOE_TM + jnp.arange(MOE_TM, dtype=jnp.int32)
    dst = jnp.concatenate([pseudo, jnp.where(valid, tok, dump)])
    tiles = jnp.arange(MOE_TILES, dtype=jnp.int32)
    tile_bucket = jnp.minimum(jnp.sum((tiles[:, None] >= tile_end[None, :]).astype(jnp.int32), axis=1),
                              N_BUCKETS - 1)
    onehot = (tile_bucket[:, None] == ids[None, :]).astype(jnp.int32)
    base = (np.arange(N_BUCKETS) // N_PAIRS) * MOE_EPG
    ea = jnp.sum(onehot * jnp.asarray(base + np.asarray(SLOT_A)[np.arange(N_BUCKETS) % N_PAIRS],
                                      jnp.int32)[None, :], axis=1)
    eb = jnp.sum(onehot * jnp.asarray(base + np.asarray(SLOT_B)[np.arange(N_BUCKETS) % N_PAIRS],
                                      jnp.int32)[None, :], axis=1)
    return (src.astype(jnp.int32), dst.astype(jnp.int32), ea.astype(jnp.int32), eb.astype(jnp.int32),
            nused.reshape(1).astype(jnp.int32), gates)


def _final_norm_kernel(h_ref, g_ref, o_ref):
    o_ref[...] = _rms(_load_rows(h_ref, o_ref.shape[0]), g_ref[...])


def _final_norm(h, g):
    tb = PROJ_TM
    return pl.pallas_call(
        _final_norm_kernel,
        grid=(SEQ // tb,),
        in_specs=[_h_spec(tb, True),
                  pl.BlockSpec((1, D_MODEL), lambda i: (0, 0))],
        out_specs=pl.BlockSpec((tb, D_MODEL), lambda i: (i, 0)),
        out_shape=jax.ShapeDtypeStruct((SEQ, D_MODEL), F32),
        compiler_params=_cparams("parallel"),
        name="final_norm",
    )(h, g.reshape(1, D_MODEL))


def kernel(x, mem, ln_mix, ln_cross, ln_mem, ln_ffn, ln_final, rel_table, even_w_in, even_w_out,
           sgu_ln_g, sgu_ln_b, sgu_w, sgu_b, attn_sink, odd_w_in, odd_w_out, xq_w, xkv_w, xo_w,
           router_group_w, router_group_b, router_expert_w, router_expert_b,
           expert_w_gate, expert_w_up, expert_w_down):
    h = x.reshape(SEQ, D_MODEL)
    mem2 = mem.reshape(MEM_LEN, D_MODEL)
    bias_even = _band_bias(rel_table, B_BLOCK, B_HALF_WINDOW, 1)
    bias_odd = [_band_bias(rel_table, C_BLOCK, window // 2 // dil, dil) for window, dil in C_PAIRS]

    for layer in range(DEPTH):
        i = layer // 2
        tiled = layer > 0
        if layer % 2 == 0:
            z = _proj(h, ln_mix[layer], even_w_in, w_lead=i, rows=SEQ, tm=EVEN_TB, tn=EVEN_IN, n=EVEN_IN,
                      gelu_cols=2 * A_WIDTH, tiled=tiled)
            h = _even_mix(z, h, sgu_ln_g[i], sgu_ln_b[i], sgu_w[i], sgu_b[i], bias_even,
                          attn_sink[i], even_w_out[i].astype(BF16), tiled=tiled)
        else:
            outs, lses = [], []
            for gi, (_, dil) in enumerate(C_PAIRS):
                zg = _proj(h, ln_mix[layer], odd_w_in, w_lead=i, w_col0=gi * 3 * C_WIDTH // PROJ_TN,
                           rows=SEQ, tm=PROJ_TM, tn=PROJ_TN, n=3 * C_WIDTH, dil=dil, tiled=tiled)
                o, lse = _dil_attn(zg, bias_odd[gi], dil)
                outs.append(o)
                lses.append(lse)
            h = _combine(outs, lses, h, odd_w_out[i].astype(BF16), tiled=tiled)

        kv = _proj(mem2, ln_mem[layer], xkv_w, w_lead=layer, rows=MEM_LEN, tm=MEM_LEN,
                   tn=2 * X_WIDTH, n=2 * X_WIDTH)
        wr_t = jnp.zeros((ROUTER_ROWS, D_MODEL), F32)
        wr_t = wr_t.at[:MOE_GROUPS].set(router_group_w[layer].T)
        wr_t = wr_t.at[MOE_GROUPS:MOE_GROUPS + N_EXPERTS].set(
            router_expert_w[layer].reshape(D_MODEL, N_EXPERTS).T)
        br = jnp.zeros((ROUTER_ROWS, 1), F32)
        br = br.at[:MOE_GROUPS, 0].set(router_group_b[layer])
        br = br.at[MOE_GROUPS:MOE_GROUPS + N_EXPERTS, 0].set(router_expert_b[layer].reshape(N_EXPERTS))
        wr_hi = wr_t.astype(BF16)
        wr_split = jnp.stack([wr_hi, (wr_t - wr_hi.astype(F32)).astype(BF16)])
        hx, meta = _cross_router(h, ln_cross[layer], xq_w[layer].astype(BF16), kv,
                                 xo_w[layer].astype(BF16), ln_ffn[layer], wr_split, br)

        src, dst, ea, eb, nused, gates = _route_tables(meta)
        h = _moe(hx, gates, ln_ffn[layer],
                 expert_w_gate[layer].reshape(N_EXPERTS, D_MODEL, D_EXPERT),
                 expert_w_up[layer].reshape(N_EXPERTS, D_MODEL, D_EXPERT),
                 expert_w_down[layer].reshape(N_EXPERTS, D_EXPERT, D_MODEL),
                 src, dst, ea, eb, nused)

    return _final_norm(h, ln_final).reshape(1, SEQ, D_MODEL)
```

```python
import functools
import math

import numpy as np
import jax
import jax.numpy as jnp
from jax import lax
from jax.experimental import pallas as pl
from jax.experimental.pallas import tpu as pltpu

F32 = jnp.float32
BF16 = jnp.bfloat16

D_MODEL = 1024
SEQ = 16384
DEPTH = 4
MEM_LEN = 256
EPS = 1e-6
NEG_INF = -1e30

A_GROUPS = 4
A_CH = 128
A_WIDTH = A_GROUPS * A_CH
A_CHUNK = 128
B_HEADS = 8
B_KV_HEADS = 2
B_Q_PER_KV = B_HEADS // B_KV_HEADS
B_HEAD_DIM = 64
B_WIDTH = B_HEADS * B_HEAD_DIM
B_KV_WIDTH = B_KV_HEADS * B_HEAD_DIM
B_HALF_WINDOW = 128
B_BLOCK = 128
EVEN_IN = 2 * A_WIDTH + B_WIDTH + 2 * B_KV_WIDTH
EVEN_Q0 = 2 * A_WIDTH
EVEN_K0 = EVEN_Q0 + B_WIDTH
EVEN_V0 = EVEN_K0 + B_KV_WIDTH

C_PAIRS = ((128, 1), (512, 4), (2048, 16))
C_GROUPS = len(C_PAIRS)
C_HEADS = 8
C_HEAD_DIM = 128
C_WIDTH = C_HEADS * C_HEAD_DIM
C_BLOCK = 64
ODD_IN = C_GROUPS * 3 * C_WIDTH

REL_BUCKETS = 32
REL_MAX_DIST = 1024
REL_HEADS = 8

X_HEADS = 4
X_HEAD_DIM = 128
X_WIDTH = X_HEADS * X_HEAD_DIM

MOE_GROUPS = 4
MOE_EPG = 4
N_EXPERTS = MOE_GROUPS * MOE_EPG
D_EXPERT = 512
SLOT_A = (0, 0, 0, 1, 1, 3)
SLOT_B = (1, 2, 3, 3, 2, 2)
N_PAIRS = len(SLOT_A)
N_BUCKETS = MOE_GROUPS * N_PAIRS

V7X_LANES = 128
ROW_CHUNKS = D_MODEL // V7X_LANES
V7X_VMEM_BYTES = 64 * 1024 * 1024
VMEM_LIMIT = 56 * 1024 * 1024

PROJ_TM = 1024
PROJ_TN = 1024
EVEN_TB = 512
EVEN_STACK = 2
ODD_TILE = PROJ_TM
ODD_BLOCKS = ODD_TILE // C_BLOCK
ODD_INTERLEAVE = 4
COMB_TB = 512
CROSS_TB = 512
MOE_TM = 256
ROUTER_ROWS = 32
MOE_TILES = (SEQ + N_BUCKETS * (MOE_TM - 1)) // MOE_TM
GATHER_TN = 1536
FINAL_TB = 512


def _cparams(*sem):
    return pltpu.CompilerParams(dimension_semantics=sem, vmem_limit_bytes=VMEM_LIMIT)


def _rms(x, g):
    return x * lax.rsqrt(jnp.mean(x * x, axis=-1, keepdims=True) + EPS) * g


def _dot(a, b):
    return jnp.dot(a, b, preferred_element_type=F32)


def _dot_nt(a, b):
    return lax.dot_general(a, b, (((1,), (1,)), ((), ())), preferred_element_type=F32)


def _load_rows(ref, n, lead=()):
    return jnp.concatenate([ref[lead + (pl.ds(c, n, stride=ROW_CHUNKS), slice(None))]
                            for c in range(ROW_CHUNKS)], axis=1)


def _store_rows(ref, val, lead=()):
    n = val.shape[0]
    for c in range(ROW_CHUNKS):
        ref[lead + (pl.ds(c, n, stride=ROW_CHUNKS), slice(None))] = val[:, c * V7X_LANES:(c + 1) * V7X_LANES]


def _row_tile(idx):
    if isinstance(idx, int):
        return pl.ds(idx * ROW_CHUNKS, ROW_CHUNKS)
    return pl.ds(pl.multiple_of(idx * ROW_CHUNKS, ROW_CHUNKS), ROW_CHUNKS)


class _RowGather:
    def __init__(self, idx_ref, src_hbm, buf, sem, tm):
        self.idx_ref, self.src, self.buf, self.sem, self.tm = idx_ref, src_hbm, buf, sem, tm

    def start(self, tile, slot, rows):
        for r in rows:
            pltpu.make_async_copy(self.src.at[_row_tile(self.idx_ref[tile * self.tm + r])],
                                  self.buf.at[slot, _row_tile(r)], self.sem.at[slot]).start()

    def wait(self, slot):
        pltpu.make_async_copy(self.src.at[pl.ds(0, self.tm * ROW_CHUNKS)], self.buf.at[slot],
                              self.sem.at[slot]).wait()


def _fill_xn(xf, xn_ref, scratch, dil):
    tm = xn_ref.shape[0]
    seg = tm // dil
    if dil == 1:
        xn_ref[...] = xf.astype(BF16)
        return
    xs_ref, = scratch
    for c in range(ROW_CHUNKS):
        xs_ref[c] = xf[:, c * V7X_LANES:(c + 1) * V7X_LANES]
    for r in range(dil):
        for c in range(ROW_CHUNKS):
            xn_ref[r * seg:(r + 1) * seg, c * V7X_LANES:(c + 1) * V7X_LANES] = (
                xs_ref[c, pl.ds(r, seg, stride=dil), :].astype(BF16))


def _proj_out(acc, o_ref, gelu_cols, dil):
    seg = acc.shape[0] // dil
    if gelu_cols:
        o_ref[0, :, :gelu_cols] = jax.nn.gelu(acc[:, :gelu_cols]).astype(o_ref.dtype)
        o_ref[0, :, gelu_cols:] = acc[:, gelu_cols:].astype(o_ref.dtype)
    else:
        for r in range(dil):
            o_ref[r] = acc[r * seg:(r + 1) * seg].astype(o_ref.dtype)


def _proj_kernel(h_ref, g_ref, w_ref, o_ref, xn_ref, *scratch, gelu_cols, dil):
    @pl.when(pl.program_id(1) == 0)
    def _():
        _fill_xn(_rms(h_ref[...], g_ref[...]), xn_ref, scratch, dil)

    _proj_out(_dot(xn_ref[...], w_ref[...].astype(BF16)), o_ref, gelu_cols, dil)


def _gather_proj_kernel(pos_ref, hs_hbm, g_ref, w_ref, o_ref, hnat_ref, xn_ref, hbuf, sem,
                        *, gelu_cols, ni, nj):
    i = pl.program_id(0)
    j = pl.program_id(1)
    tm = xn_ref.shape[0]
    per = tm // nj
    slot = i % 2
    gather = _RowGather(pos_ref, hs_hbm, hbuf, sem, tm)

    @pl.when((i == 0) & (j == 0))
    def _():
        gather.start(0, 0, range(tm))

    @pl.when(j == 0)
    def _():
        gather.wait(slot)
        h = _load_rows(hbuf, tm, (slot,))
        hnat_ref[...] = h
        _fill_xn(_rms(h, g_ref[...]), xn_ref, (), 1)

    nxt = jnp.where(i + 1 < ni, i + 1, 0)
    gather.start(nxt, 1 - slot, [j * per + r for r in range(per)])
    _proj_out(_dot(xn_ref[...], w_ref[...].astype(BF16)), o_ref, gelu_cols, 1)

    @pl.when((i == ni - 1) & (j == nj - 1))
    def _():
        gather.wait(1 - slot)


def _proj(h, g, w, *, rows, tm, tn, n, w_lead=0, w_col0=0, gelu_cols=0, dil=1, pos=None):
    seg = tm // dil
    ni, nj = rows // tm, n // tn
    out_z = jax.ShapeDtypeStruct((dil, rows // dil, n), BF16)
    if pos is None:
        return pl.pallas_call(
            functools.partial(_proj_kernel, gelu_cols=gelu_cols, dil=dil),
            grid=(ni, nj),
            in_specs=[
                pl.BlockSpec((tm, D_MODEL), lambda i, j: (i, 0)),
                pl.BlockSpec((1, D_MODEL), lambda i, j: (0, 0)),
                pl.BlockSpec((None, D_MODEL, tn), lambda i, j: (w_lead, 0, w_col0 + j)),
            ],
            out_specs=pl.BlockSpec((dil, seg, tn), lambda i, j: (0, i, j)),
            out_shape=out_z,
            scratch_shapes=[pltpu.VMEM((tm, D_MODEL), BF16)] + (
                [pltpu.VMEM((ROW_CHUNKS, tm, V7X_LANES), F32)] if dil > 1 else []),
            compiler_params=_cparams("parallel", "arbitrary"),
            name=f"norm_proj_d{dil}",
        )(h, g.reshape(1, D_MODEL), w)
    assert dil == 1 and tm % nj == 0
    grid_spec = pltpu.PrefetchScalarGridSpec(
        num_scalar_prefetch=1,
        grid=(ni, nj),
        in_specs=[
            pl.BlockSpec(memory_space=pl.ANY),
            pl.BlockSpec((1, D_MODEL), lambda i, j, p: (0, 0)),
            pl.BlockSpec((None, D_MODEL, tn), lambda i, j, p: (w_lead, 0, w_col0 + j)),
        ],
        out_specs=[pl.BlockSpec((1, tm, tn), lambda i, j, p: (0, i, j)),
                   pl.BlockSpec((tm, D_MODEL), lambda i, j, p: (i, 0))],
        scratch_shapes=[pltpu.VMEM((tm, D_MODEL), BF16),
                        pltpu.VMEM((2, tm * ROW_CHUNKS, V7X_LANES), F32),
                        pltpu.SemaphoreType.DMA((2,))],
    )
    return pl.pallas_call(
        functools.partial(_gather_proj_kernel, gelu_cols=gelu_cols, ni=ni, nj=nj),
        grid_spec=grid_spec,
        out_shape=[out_z, jax.ShapeDtypeStruct((rows, D_MODEL), F32)],
        compiler_params=_cparams("arbitrary", "arbitrary"),
        name="gather_norm_proj",
    )(pos, h, g.reshape(1, D_MODEL), w)


def _t5_bucket_np(rel):
    nb = REL_BUCKETS // 2
    max_exact = nb // 2
    ret = np.where(rel > 0, nb, 0)
    n = np.abs(rel)
    nf = np.maximum(n, 1).astype(np.float32)
    large = max_exact + (np.log(nf / np.float32(max_exact)) / np.float32(math.log(REL_MAX_DIST / max_exact))
                         * np.float32(nb - max_exact)).astype(np.int32)
    large = np.minimum(large, nb - 1)
    return (ret + np.where(n < max_exact, n, large)).astype(np.int32)


def _bias_kernel(table_ref, idx_ref, mask_ref, o_ref, *, block):
    idx = idx_ref[...]
    for h in range(REL_HEADS):
        acc = jnp.zeros(idx.shape, F32)
        for b in range(REL_BUCKETS):
            acc = jnp.where(idx == b, table_ref[b, h], acc)
        for v in range(3):
            o_ref[v, h * block:(h + 1) * block, :] = acc + mask_ref[v]


def _band_bias(table, block, half, dil):
    rel = np.arange(3 * block)[None, :] - block - np.arange(block)[:, None]
    band = np.abs(rel) <= half
    col = np.arange(3 * block)[None, :]
    masks = np.stack([band & (col >= block), band, band & (col < 2 * block)])
    add = np.where(masks, 0.0, NEG_INF).astype(np.float32)
    return pl.pallas_call(
        functools.partial(_bias_kernel, block=block),
        in_specs=[pl.BlockSpec(memory_space=pltpu.SMEM),
                  pl.BlockSpec(memory_space=pltpu.VMEM),
                  pl.BlockSpec(memory_space=pltpu.VMEM)],
        out_specs=pl.BlockSpec(memory_space=pltpu.VMEM),
        out_shape=jax.ShapeDtypeStruct((3, REL_HEADS * block, 3 * block), F32),
        name=f"rel_bias_d{dil}",
    )(table, jnp.asarray(_t5_bucket_np(rel * dil)), jnp.asarray(add))


def _even_mix_kernel(z_ref, kvp_ref, kvn_ref, h_ref, lng_ref, lnb_ref, ws_ref, bs_ref, bias_ref,
                     sink_ref, wout_ref, o_ref, kv_scr, y_scr):
    i = pl.program_id(0)
    nsub = EVEN_TB // B_BLOCK
    nblk = SEQ // B_BLOCK
    kv_scr[0:B_BLOCK] = kvp_ref[...]
    kv_scr[B_BLOCK:B_BLOCK + EVEN_TB] = z_ref[0, :, EVEN_K0:EVEN_IN]
    kv_scr[B_BLOCK + EVEN_TB:] = kvn_ref[...]
    lng = lng_ref[...]
    lnb = lnb_ref[...]
    for s in range(nsub):
        r0 = s * B_BLOCK
        gb = i * nsub + s
        sel = jnp.where(gb == 0, 0, jnp.where(gb == nblk - 1, 2, 1))
        u = z_ref[0, r0:r0 + A_CHUNK, 0:A_WIDTH].astype(F32)
        va = z_ref[0, r0:r0 + A_CHUNK, A_WIDTH:2 * A_WIDTH].astype(F32)
        mu = jnp.mean(va, axis=-1, keepdims=True)
        vc = va - mu
        var = jnp.mean(vc * vc, axis=-1, keepdims=True)
        vn = (vc * lax.rsqrt(var + EPS) * lng + lnb).astype(BF16)
        for g in range(A_GROUPS):
            c0 = g * A_CH
            mixed = _dot(ws_ref[g], vn[:, c0:c0 + A_CH]) + bs_ref[g]
            y_scr[r0:r0 + A_CHUNK, c0:c0 + A_CH] = (u[:, c0:c0 + A_CH] * mixed).astype(BF16)
        units = [(kh, kh * B_Q_PER_KV + half * EVEN_STACK)
                 for kh in range(B_KV_HEADS) for half in range(B_Q_PER_KV // EVEN_STACK)]
        lgs = []
        for kh, hd0 in units:
            kw = kv_scr[r0:r0 + 3 * B_BLOCK, kh * B_HEAD_DIM:(kh + 1) * B_HEAD_DIM]
            q = jnp.concatenate(
                [z_ref[0, r0:r0 + B_BLOCK,
                       EVEN_Q0 + (hd0 + g) * B_HEAD_DIM:EVEN_Q0 + (hd0 + g + 1) * B_HEAD_DIM]
                 for g in range(EVEN_STACK)], axis=0)
            lgs.append(_dot_nt(q, kw))
        lgs = [lg * (B_HEAD_DIM ** -0.5) + bias_ref[sel, hd0 * B_BLOCK:(hd0 + EVEN_STACK) * B_BLOCK, :]
               for lg, (_, hd0) in zip(lgs, units)]
        sks = [sink_ref[hd0 * B_BLOCK:(hd0 + EVEN_STACK) * B_BLOCK, :] for _, hd0 in units]
        ms = [jnp.maximum(jnp.max(lg, axis=-1, keepdims=True), sk) for lg, sk in zip(lgs, sks)]
        ps = [jnp.exp(lg - m) for lg, m in zip(lgs, ms)]
        dens = [jnp.sum(p, axis=-1, keepdims=True) + jnp.exp(sk - m) for p, sk, m in zip(ps, sks, ms)]
        os_ = []
        for p, (kh, _) in zip(ps, units):
            vw = kv_scr[r0:r0 + 3 * B_BLOCK,
                        B_KV_WIDTH + kh * B_HEAD_DIM:B_KV_WIDTH + (kh + 1) * B_HEAD_DIM]
            os_.append(_dot(p.astype(BF16), vw))
        for o, den, (_, hd0) in zip(os_, dens, units):
            o = o * (1.0 / den)
            for g in range(EVEN_STACK):
                c0 = A_WIDTH + (hd0 + g) * B_HEAD_DIM
                y_scr[r0:r0 + B_BLOCK, c0:c0 + B_HEAD_DIM] = o[g * B_BLOCK:(g + 1) * B_BLOCK].astype(BF16)
    o_ref[...] = h_ref[...] + _dot(y_scr[...], wout_ref[...])


def _even_mix(z, h, ln_g, ln_b, w_s, b_s, bias, sink, w_out):
    nsub = EVEN_TB // B_BLOCK
    nblk = SEQ // B_BLOCK
    kv_cb = EVEN_K0 // (2 * B_KV_WIDTH)
    sink_col = jnp.broadcast_to(sink.reshape(B_HEADS, 1, 1), (B_HEADS, B_BLOCK, 1)).reshape(
        B_HEADS * B_BLOCK, 1)
    return pl.pallas_call(
        _even_mix_kernel,
        grid=(SEQ // EVEN_TB,),
        in_specs=[
            pl.BlockSpec((1, EVEN_TB, EVEN_IN), lambda i: (0, i, 0)),
            pl.BlockSpec((None, B_BLOCK, 2 * B_KV_WIDTH),
                         lambda i: (0, jnp.maximum(i * nsub - 1, 0), kv_cb)),
            pl.BlockSpec((None, B_BLOCK, 2 * B_KV_WIDTH),
                         lambda i: (0, jnp.minimum((i + 1) * nsub, nblk - 1), kv_cb)),
            pl.BlockSpec((EVEN_TB, D_MODEL), lambda i: (i, 0)),
            pl.BlockSpec((1, A_WIDTH), lambda i: (0, 0)),
            pl.BlockSpec((1, A_WIDTH), lambda i: (0, 0)),
            pl.BlockSpec((A_GROUPS, A_CHUNK, A_CHUNK), lambda i: (0, 0, 0)),
            pl.BlockSpec((A_GROUPS, A_CHUNK, A_CH), lambda i: (0, 0, 0)),
            pl.BlockSpec((3, B_HEADS * B_BLOCK, 3 * B_BLOCK), lambda i: (0, 0, 0)),
            pl.BlockSpec((B_HEADS * B_BLOCK, 1), lambda i: (0, 0)),
            pl.BlockSpec((A_WIDTH + B_WIDTH, D_MODEL), lambda i: (0, 0)),
        ],
        out_specs=pl.BlockSpec((EVEN_TB, D_MODEL), lambda i: (i, 0)),
        out_shape=jax.ShapeDtypeStruct((SEQ, D_MODEL), F32),
        scratch_shapes=[
            pltpu.VMEM((EVEN_TB + 2 * B_BLOCK, 2 * B_KV_WIDTH), BF16),
            pltpu.VMEM((EVEN_TB, A_WIDTH + B_WIDTH), BF16),
        ],
        compiler_params=_cparams("parallel"),
        name="even_mixer",
    )(z, z, z, h, ln_g.reshape(1, A_WIDTH), ln_b.reshape(1, A_WIDTH), w_s.astype(BF16),
      jnp.broadcast_to(b_s[:, :, None], (A_GROUPS, A_CHUNK, A_CH)), bias, sink_col, w_out)


def _dil_attn_kernel(zc_ref, kp_ref, kn_ref, vp_ref, vn_ref, bias_ref, o_ref, lse_ref,
                     k_scr, v_scr, o_scr, lse_scr, *, dil):
    t = pl.program_id(0)
    seg = ODD_TILE // dil
    nsb = seg // C_BLOCK
    nblk = SEQ // dil // C_BLOCK
    k_scr[:, 0:C_BLOCK] = kp_ref[...]
    k_scr[:, C_BLOCK:C_BLOCK + seg] = zc_ref[:, :, C_WIDTH:2 * C_WIDTH]
    k_scr[:, C_BLOCK + seg:] = kn_ref[...]
    v_scr[:, 0:C_BLOCK] = vp_ref[...]
    v_scr[:, C_BLOCK:C_BLOCK + seg] = zc_ref[:, :, 2 * C_WIDTH:3 * C_WIDTH]
    v_scr[:, C_BLOCK + seg:] = vn_ref[...]
    lane = lax.broadcasted_iota(jnp.int32, (C_BLOCK, V7X_LANES), 1)

    def body(it, carry):
        blocks = []
        for u in range(ODD_INTERLEAVE):
            n = it * ODD_INTERLEAVE + u
            r = n // nsb
            s = n % nsb
            r0 = pl.multiple_of(s * C_BLOCK, C_BLOCK)
            gb = t * nsb + s
            sel = jnp.where(gb == 0, 0, jnp.where(gb == nblk - 1, 2, 1))
            rows = (pl.ds(s * (C_BLOCK * dil) + r, C_BLOCK, stride=dil) if dil > 1
                    else pl.ds(r0, C_BLOCK))
            blocks.append((r, r0, sel, rows))
        lgs = []
        for r, r0, _, _ in blocks:
            for hd in range(C_HEADS):
                c0 = hd * C_HEAD_DIM
                q = zc_ref[r, pl.ds(r0, C_BLOCK), c0:c0 + C_HEAD_DIM]
                kw = k_scr[r, pl.ds(r0, 3 * C_BLOCK), c0:c0 + C_HEAD_DIM]
                lgs.append(_dot_nt(q, kw))
        lg = [jnp.concatenate(lgs[u * C_HEADS:(u + 1) * C_HEADS], axis=0) * (C_HEAD_DIM ** -0.5)
              + bias_ref[blk[2]] for u, blk in enumerate(blocks)]
        m = [jnp.max(x, axis=-1, keepdims=True) for x in lg]
        p = [jnp.exp(x - mm) for x, mm in zip(lg, m)]
        den = [jnp.sum(x, axis=-1, keepdims=True) for x in p]
        inv = [1.0 / d for d in den]
        lse = [mm + jnp.log(d) for mm, d in zip(m, den)]
        pb = [x.astype(BF16) for x in p]
        outs = []
        for u, (r, r0, _, _) in enumerate(blocks):
            for hd in range(C_HEADS):
                c0 = hd * C_HEAD_DIM
                vw = v_scr[r, pl.ds(r0, 3 * C_BLOCK), c0:c0 + C_HEAD_DIM]
                outs.append(_dot(pb[u][hd * C_BLOCK:(hd + 1) * C_BLOCK], vw))
        for u, (_, _, _, rows) in enumerate(blocks):
            lse_tile = jnp.zeros((C_BLOCK, V7X_LANES), F32)
            for hd in range(C_HEADS):
                o_scr[hd, rows, :] = outs[u * C_HEADS + hd] * inv[u][hd * C_BLOCK:(hd + 1) * C_BLOCK]
                lse_tile = jnp.where(lane == hd, lse[u][hd * C_BLOCK:(hd + 1) * C_BLOCK], lse_tile)
            lse_scr[rows, :] = lse_tile
        return carry

    lax.fori_loop(0, ODD_BLOCKS // ODD_INTERLEAVE, body, 0)
    for hd in range(C_HEADS):
        o_ref[:, hd * C_HEAD_DIM:(hd + 1) * C_HEAD_DIM] = o_scr[hd].astype(o_ref.dtype)
    lse_ref[...] = lse_scr[...]


def _dil_attn(zg, bias, dil):
    seg = ODD_TILE // dil
    nsb = seg // C_BLOCK
    last = SEQ // dil // C_BLOCK - 1

    def halo(j, nxt):
        if nxt:
            return pl.BlockSpec((dil, C_BLOCK, C_WIDTH),
                                lambda t: (0, jnp.minimum((t + 1) * nsb, last), j))
        return pl.BlockSpec((dil, C_BLOCK, C_WIDTH), lambda t: (0, jnp.maximum(t * nsb - 1, 0), j))

    return pl.pallas_call(
        functools.partial(_dil_attn_kernel, dil=dil),
        grid=(SEQ // ODD_TILE,),
        in_specs=[pl.BlockSpec((dil, seg, 3 * C_WIDTH), lambda t: (0, t, 0)),
                  halo(1, False), halo(1, True), halo(2, False), halo(2, True),
                  pl.BlockSpec((3, C_HEADS * C_BLOCK, 3 * C_BLOCK), lambda t: (0, 0, 0))],
        out_specs=[pl.BlockSpec((ODD_TILE, C_WIDTH), lambda t: (t, 0)),
                   pl.BlockSpec((ODD_TILE, V7X_LANES), lambda t: (t, 0))],
        out_shape=[jax.ShapeDtypeStruct((SEQ, C_WIDTH), BF16),
                   jax.ShapeDtypeStruct((SEQ, V7X_LANES), F32)],
        scratch_shapes=[pltpu.VMEM((dil, seg + 2 * C_BLOCK, C_WIDTH), BF16),
                        pltpu.VMEM((dil, seg + 2 * C_BLOCK, C_WIDTH), BF16),
                        pltpu.VMEM((C_HEADS, ODD_TILE, C_HEAD_DIM), F32),
                        pltpu.VMEM((ODD_TILE, V7X_LANES), F32)],
        compiler_params=_cparams("parallel"),
        name=f"dilated_attn_d{dil}",
    )(zg, zg, zg, zg, zg, bias)


def _combine_kernel(o0_ref, o1_ref, o2_ref, l0_ref, l1_ref, l2_ref, h_ref, wout_ref, out_ref, y_scr):
    l0 = l0_ref[...]
    l1 = l1_ref[...]
    l2 = l2_ref[...]
    m = jnp.maximum(jnp.maximum(l0, l1), l2)
    e0 = jnp.exp(l0 - m)
    e1 = jnp.exp(l1 - m)
    e2 = jnp.exp(l2 - m)
    tot = e0 + e1 + e2
    w0 = e0 / tot
    w1 = e1 / tot
    w2 = e2 / tot
    for hd in range(C_HEADS):
        c0 = hd * C_HEAD_DIM
        y = (w0[:, hd:hd + 1] * o0_ref[:, c0:c0 + C_HEAD_DIM].astype(F32)
             + w1[:, hd:hd + 1] * o1_ref[:, c0:c0 + C_HEAD_DIM].astype(F32)
             + w2[:, hd:hd + 1] * o2_ref[:, c0:c0 + C_HEAD_DIM].astype(F32))
        y_scr[:, c0:c0 + C_HEAD_DIM] = y.astype(BF16)
    out_ref[...] = h_ref[...] + _dot(y_scr[...], wout_ref[...])


def _combine(outs, lses, h, w_out):
    blk_o = pl.BlockSpec((COMB_TB, C_WIDTH), lambda i: (i, 0))
    blk_l = pl.BlockSpec((COMB_TB, V7X_LANES), lambda i: (i, 0))
    return pl.pallas_call(
        _combine_kernel,
        grid=(SEQ // COMB_TB,),
        in_specs=[blk_o, blk_o, blk_o, blk_l, blk_l, blk_l,
                  pl.BlockSpec((COMB_TB, D_MODEL), lambda i: (i, 0)),
                  pl.BlockSpec((C_WIDTH, D_MODEL), lambda i: (0, 0))],
        out_specs=pl.BlockSpec((COMB_TB, D_MODEL), lambda i: (i, 0)),
        out_shape=jax.ShapeDtypeStruct((SEQ, D_MODEL), F32),
        scratch_shapes=[pltpu.VMEM((COMB_TB, C_WIDTH), BF16)],
        compiler_params=_cparams("parallel"),
        name="group_combine_proj",
    )(*outs, *lses, h, w_out)


def _cross_kernel(h_ref, gx_ref, wq_ref, kv_ref, wo_ref, gf_ref, wr_ref, br_ref,
                  hx_ref, meta_ref, o_scr):
    h = h_ref[...]
    q = _dot(_rms(h, gx_ref[...]).astype(BF16), wq_ref[...]).astype(BF16)
    for hd in range(X_HEADS):
        c0 = hd * X_HEAD_DIM
        lg = _dot_nt(q[:, c0:c0 + X_HEAD_DIM], kv_ref[0, :, c0:c0 + X_HEAD_DIM]) * (X_HEAD_DIM ** -0.5)
        m = jnp.max(lg, axis=-1, keepdims=True)
        p = jnp.exp(lg - m)
        den = jnp.sum(p, axis=-1, keepdims=True)
        o = _dot(p.astype(BF16), kv_ref[0, :, X_WIDTH + c0:X_WIDTH + c0 + X_HEAD_DIM]) / den
        o_scr[:, c0:c0 + X_HEAD_DIM] = o.astype(BF16)
    h2 = h + _dot(o_scr[...], wo_ref[...])
    _store_rows(hx_ref, h2)

    t = _rms(h2, gf_ref[...])
    t_hi = t.astype(BF16)
    t_lo = (t - t_hi.astype(F32)).astype(BF16)
    lt = (_dot_nt(wr_ref[0], t_hi) + _dot_nt(wr_ref[0], t_lo) + _dot_nt(wr_ref[1], t_hi)) + br_ref[...]
    g = [lt[k:k + 1, :] for k in range(MOE_GROUPS)]
    gmax = jnp.maximum(jnp.maximum(g[0], g[1]), jnp.maximum(g[2], g[3]))
    grp = jnp.where(g[0] == gmax, 0, jnp.where(g[1] == gmax, 1, jnp.where(g[2] == gmax, 2, 3)))
    g_gate = 1.0 / (jnp.exp(g[0] - gmax) + jnp.exp(g[1] - gmax) + jnp.exp(g[2] - gmax)
                    + jnp.exp(g[3] - gmax))
    e = []
    for k in range(MOE_EPG):
        rows = [lt[MOE_GROUPS + gi * MOE_EPG + k:MOE_GROUPS + gi * MOE_EPG + k + 1, :]
                for gi in range(MOE_GROUPS)]
        e.append(jnp.where(grp == 0, rows[0], jnp.where(grp == 1, rows[1],
                                                         jnp.where(grp == 2, rows[2], rows[3]))))
    v1 = jnp.maximum(jnp.maximum(e[0], e[1]), jnp.maximum(e[2], e[3]))
    i1 = jnp.where(e[0] == v1, 0, jnp.where(e[1] == v1, 1, jnp.where(e[2] == v1, 2, 3)))
    r = [jnp.where(i1 == k, -jnp.inf, e[k]) for k in range(MOE_EPG)]
    v2 = jnp.maximum(jnp.maximum(r[0], r[1]), jnp.maximum(r[2], r[3]))
    i2 = jnp.where(r[0] == v2, 0, jnp.where(r[1] == v2, 1, jnp.where(r[2] == v2, 2, 3)))
    d = jnp.exp(v2 - v1)
    w1 = g_gate / (1.0 + d)
    w2 = g_gate * d / (1.0 + d)
    first_lo = i1 < i2
    lo = jnp.where(first_lo, i1, i2)
    hi = jnp.where(first_lo, i2, i1)
    w_lo = jnp.where(first_lo, w1, w2)
    w_hi = jnp.where(first_lo, w2, w1)
    pair = jnp.where(lo == 0, hi - 1, jnp.where(lo == 1, jnp.where(hi == 3, 3, 4), 5))
    w_a = jnp.where(lo == 2, w_hi, w_lo)
    w_b = jnp.where(lo == 2, w_lo, w_hi)
    bucket = (grp * N_PAIRS + pair).astype(F32)
    row = lax.broadcasted_iota(jnp.int32, (8, CROSS_TB), 0)
    meta_ref[...] = jnp.where(row == 0, bucket, jnp.where(row == 1, w_a, jnp.where(row == 2, w_b, 0.0)))


def _cross_router(h, g_cross, wq, kv, wo, g_ffn, wr_t, br):
    full = lambda shape: pl.BlockSpec(shape, lambda i: tuple(0 for _ in shape))
    return pl.pallas_call(
        _cross_kernel,
        grid=(SEQ // CROSS_TB,),
        in_specs=[
            pl.BlockSpec((CROSS_TB, D_MODEL), lambda i: (i, 0)),
            full((1, D_MODEL)),
            full((D_MODEL, X_WIDTH)),
            full((1, MEM_LEN, 2 * X_WIDTH)),
            full((X_WIDTH, D_MODEL)),
            full((1, D_MODEL)),
            full((2, ROUTER_ROWS, D_MODEL)),
            full((ROUTER_ROWS, 1)),
        ],
        out_specs=[pl.BlockSpec((CROSS_TB * ROW_CHUNKS, V7X_LANES), lambda i: (i, 0)),
                   pl.BlockSpec((8, CROSS_TB), lambda i: (0, i))],
        out_shape=[jax.ShapeDtypeStruct((SEQ * ROW_CHUNKS, V7X_LANES), F32),
                   jax.ShapeDtypeStruct((8, SEQ), F32)],
        scratch_shapes=[pltpu.VMEM((CROSS_TB, X_WIDTH), BF16)],
        compiler_params=_cparams("parallel"),
        name="cross_attn_router",
    )(h, g_cross.reshape(1, D_MODEL), wq, kv, wo, g_ffn.reshape(1, D_MODEL), wr_t, br)


def _moe_kernel(src_ref, ea_ref, eb_ref, nused_ref,
                hx_hbm, gates_ref, gf_ref, wga_ref, wua_ref, wda_ref, wgb_ref, wub_ref, wdb_ref,
                out_ref, xbuf, wup_a, wdn_a, wup_b, wdn_b, gsem):
    k = pl.program_id(0)
    nused = nused_ref[0]
    xslot = k % 3
    gather = _RowGather(src_ref, hx_hbm, xbuf, gsem, MOE_TM)

    @pl.when(k == 0)
    def _():
        gather.start(0, 0, range(MOE_TM))
        gather.start(1, 1, range(MOE_TM))

    prev = jnp.maximum(k - 1, 0)

    @pl.when((k < nused) & ((k == 0) | (ea_ref[k] != ea_ref[prev])))
    def _():
        wup_a[:, :D_EXPERT] = wga_ref[0].astype(BF16)
        wup_a[:, D_EXPERT:] = wua_ref[0].astype(BF16)
        wdn_a[...] = wda_ref[0].astype(BF16)

    @pl.when((k < nused) & ((k == 0) | (eb_ref[k] != eb_ref[prev])))
    def _():
        wup_b[:, :D_EXPERT] = wgb_ref[0].astype(BF16)
        wup_b[:, D_EXPERT:] = wub_ref[0].astype(BF16)
        wdn_b[...] = wdb_ref[0].astype(BF16)

    @pl.when(k < nused)
    def _():
        gather.wait(xslot)
        h2 = _load_rows(xbuf, MOE_TM, (xslot,))
        gather.start(k + 2, (k + 2) % 3, range(MOE_TM))
        t = _rms(h2, gf_ref[...]).astype(BF16)
        y = jnp.zeros((MOE_TM, D_MODEL), F32)
        for col, wup, wdn in ((0, wup_a, wdn_a), (1, wup_b, wdn_b)):
            gate = gates_ref[:, col:col + 1]
            gu = _dot(t, wup[...])
            hid = jax.nn.silu(gu[:, :D_EXPERT]) * gu[:, D_EXPERT:] * gate
            y = y + _dot(hid.astype(BF16), wdn[...])
        _store_rows(out_ref, h2 + y)

    @pl.when(k >= nused)
    def _():
        out_ref[...] = jnp.zeros(out_ref.shape, F32)

    @pl.when(k == nused - 1)
    def _():
        gather.wait((k + 1) % 3)
        gather.wait((k + 2) % 3)


def _moe(hx, gates, g_ffn, w_gate, w_up, w_down, src, ea, eb, nused):
    def wspec(shape, which):
        if which == 0:
            return pl.BlockSpec((1,) + shape, lambda k, s, a, b, n: (a[k], 0, 0))
        return pl.BlockSpec((1,) + shape, lambda k, s, a, b, n: (b[k], 0, 0))

    up_shape = (D_MODEL, D_EXPERT)
    down_shape = (D_EXPERT, D_MODEL)
    grid_spec = pltpu.PrefetchScalarGridSpec(
        num_scalar_prefetch=4,
        grid=(MOE_TILES,),
        in_specs=[
            pl.BlockSpec(memory_space=pl.ANY),
            pl.BlockSpec((MOE_TM, V7X_LANES), lambda k, s, a, b, n: (k, 0)),
            pl.BlockSpec((1, D_MODEL), lambda k, s, a, b, n: (0, 0)),
            wspec(up_shape, 0), wspec(up_shape, 0), wspec(down_shape, 0),
            wspec(up_shape, 1), wspec(up_shape, 1), wspec(down_shape, 1),
        ],
        out_specs=pl.BlockSpec((MOE_TM * ROW_CHUNKS, V7X_LANES), lambda k, s, a, b, n: (k, 0)),
        scratch_shapes=[
            pltpu.VMEM((3, MOE_TM * ROW_CHUNKS, V7X_LANES), F32),
            pltpu.VMEM((D_MODEL, 2 * D_EXPERT), BF16),
            pltpu.VMEM((D_EXPERT, D_MODEL), BF16),
            pltpu.VMEM((D_MODEL, 2 * D_EXPERT), BF16),
            pltpu.VMEM((D_EXPERT, D_MODEL), BF16),
            pltpu.SemaphoreType.DMA((3,)),
        ],
    )
    return pl.pallas_call(
        _moe_kernel,
        grid_spec=grid_spec,
        out_shape=jax.ShapeDtypeStruct((MOE_TILES * MOE_TM * ROW_CHUNKS, V7X_LANES), F32),
        compiler_params=_cparams("arbitrary"),
        name="routed_moe",
    )(src, ea, eb, nused, hx, gates, g_ffn.reshape(1, D_MODEL),
      w_gate, w_up, w_down, w_gate, w_up, w_down)


def _route_tables(meta):
    bucket = meta[0].astype(jnp.int32)
    ids = jnp.arange(N_BUCKETS, dtype=jnp.int32)
    counts = jnp.sum((bucket[:, None] == ids[None, :]).astype(jnp.int32), axis=0)
    ntile = (counts + MOE_TM - 1) // MOE_TM
    pad = ntile * MOE_TM - counts
    tile_end = jnp.cumsum(ntile)
    nused = tile_end[-1]
    dummy_key = jnp.where(jnp.arange(MOE_TM - 1, dtype=jnp.int32)[None, :] < pad[:, None],
                          ids[:, None], N_BUCKETS)
    keys = jnp.concatenate([bucket, dummy_key.reshape(-1)])
    vals = jnp.concatenate([jnp.arange(SEQ, dtype=jnp.int32),
                            jnp.full((N_BUCKETS * (MOE_TM - 1),), SEQ, jnp.int32)])
    nslot = MOE_TILES * MOE_TM
    zpad = jnp.zeros((N_BUCKETS * (MOE_TM - 1),), F32)
    _, tok, ga, gb = lax.sort((keys, vals, jnp.concatenate([meta[1], zpad]), jnp.concatenate([meta[2], zpad])),
                              num_keys=1, is_stable=True)
    tok = tok[:nslot]
    gates = jnp.pad(jnp.stack([ga[:nslot], gb[:nslot]], axis=1), ((0, 0), (0, V7X_LANES - 2)))
    valid = tok < SEQ
    src = jnp.concatenate([jnp.where(valid, tok, 0), jnp.zeros((2 * MOE_TM,), jnp.int32)])
    _, pos = lax.sort((tok, jnp.arange(nslot, dtype=jnp.int32)), num_keys=1, is_stable=True)
    pos = pos[:SEQ]
    tiles = jnp.arange(MOE_TILES, dtype=jnp.int32)
    tile_bucket = jnp.minimum(jnp.sum((tiles[:, None] >= tile_end[None, :]).astype(jnp.int32), axis=1),
                              N_BUCKETS - 1)
    onehot = (tile_bucket[:, None] == ids[None, :]).astype(jnp.int32)
    base = (np.arange(N_BUCKETS) // N_PAIRS) * MOE_EPG
    ea = jnp.sum(onehot * jnp.asarray(base + np.asarray(SLOT_A)[np.arange(N_BUCKETS) % N_PAIRS],
                                      jnp.int32)[None, :], axis=1)
    eb = jnp.sum(onehot * jnp.asarray(base + np.asarray(SLOT_B)[np.arange(N_BUCKETS) % N_PAIRS],
                                      jnp.int32)[None, :], axis=1)
    return (src.astype(jnp.int32), pos.astype(jnp.int32), ea.astype(jnp.int32), eb.astype(jnp.int32),
            nused.reshape(1).astype(jnp.int32), gates)


def _final_norm_kernel(pos_ref, hs_hbm, g_ref, o_ref, hbuf, sem, *, ni):
    i = pl.program_id(0)
    tm = o_ref.shape[0]
    slot = i % 2
    gather = _RowGather(pos_ref, hs_hbm, hbuf, sem, tm)

    @pl.when(i == 0)
    def _():
        gather.start(0, 0, range(tm))

    @pl.when(i + 1 < ni)
    def _():
        gather.start(i + 1, 1 - slot, range(tm))

    gather.wait(slot)
    o_ref[...] = _rms(_load_rows(hbuf, tm, (slot,)), g_ref[...])


def _final_norm(hs, pos, g):
    tb = FINAL_TB
    ni = SEQ // tb
    grid_spec = pltpu.PrefetchScalarGridSpec(
        num_scalar_prefetch=1,
        grid=(ni,),
        in_specs=[pl.BlockSpec(memory_space=pl.ANY),
                  pl.BlockSpec((1, D_MODEL), lambda i, p: (0, 0))],
        out_specs=pl.BlockSpec((tb, D_MODEL), lambda i, p: (i, 0)),
        scratch_shapes=[pltpu.VMEM((2, tb * ROW_CHUNKS, V7X_LANES), F32),
                        pltpu.SemaphoreType.DMA((2,))],
    )
    return pl.pallas_call(
        functools.partial(_final_norm_kernel, ni=ni),
        grid_spec=grid_spec,
        out_shape=jax.ShapeDtypeStruct((SEQ, D_MODEL), F32),
        compiler_params=_cparams("arbitrary"),
        name="final_norm",
    )(pos, hs, g.reshape(1, D_MODEL))


def kernel(x, mem, ln_mix, ln_cross, ln_mem, ln_ffn, ln_final, rel_table, even_w_in, even_w_out,
           sgu_ln_g, sgu_ln_b, sgu_w, sgu_b, attn_sink, odd_w_in, odd_w_out, xq_w, xkv_w, xo_w,
           router_group_w, router_group_b, router_expert_w, router_expert_b,
           expert_w_gate, expert_w_up, expert_w_down):
    h = x.reshape(SEQ, D_MODEL)
    mem2 = mem.reshape(MEM_LEN, D_MODEL)
    bias_even = _band_bias(rel_table, B_BLOCK, B_HALF_WINDOW, 1)
    bias_odd = [_band_bias(rel_table, C_BLOCK, window // 2 // dil, dil) for window, dil in C_PAIRS]

    pos = None
    for layer in range(DEPTH):
        i = layer // 2
        if layer % 2 == 0:
            z = _proj(h, ln_mix[layer], even_w_in, w_lead=i, rows=SEQ, tm=EVEN_TB, tn=EVEN_IN, n=EVEN_IN,
                      gelu_cols=2 * A_WIDTH, pos=pos)
            if pos is not None:
                z, h = z
            h = _even_mix(z, h, sgu_ln_g[i], sgu_ln_b[i], sgu_w[i], sgu_b[i], bias_even,
                          attn_sink[i], even_w_out[i].astype(BF16))
        else:
            zg, h = _proj(h, ln_mix[layer], odd_w_in, w_lead=i, rows=SEQ, tm=PROJ_TM, tn=GATHER_TN,
                          n=3 * C_WIDTH, pos=pos)
            outs, lses = [], []
            for gi, (_, dil) in enumerate(C_PAIRS):
                if gi > 0:
                    zg = _proj(h, ln_mix[layer], odd_w_in, w_lead=i, w_col0=gi * 3 * C_WIDTH // PROJ_TN,
                               rows=SEQ, tm=PROJ_TM, tn=PROJ_TN, n=3 * C_WIDTH, dil=dil)
                o, lse = _dil_attn(zg, bias_odd[gi], dil)
                outs.append(o)
                lses.append(lse)
            h = _combine(outs, lses, h, odd_w_out[i].astype(BF16))

        kv = _proj(mem2, ln_mem[layer], xkv_w, w_lead=layer, rows=MEM_LEN, tm=MEM_LEN,
                   tn=2 * X_WIDTH, n=2 * X_WIDTH)
        wr_t = jnp.zeros((ROUTER_ROWS, D_MODEL), F32)
        wr_t = wr_t.at[:MOE_GROUPS].set(router_group_w[layer].T)
        wr_t = wr_t.at[MOE_GROUPS:MOE_GROUPS + N_EXPERTS].set(
            router_expert_w[layer].reshape(D_MODEL, N_EXPERTS).T)
        br = jnp.zeros((ROUTER_ROWS, 1), F32)
        br = br.at[:MOE_GROUPS, 0].set(router_group_b[layer])
        br = br.at[MOE_GROUPS:MOE_GROUPS + N_EXPERTS, 0].set(router_expert_b[layer].reshape(N_EXPERTS))
        wr_hi = wr_t.astype(BF16)
        wr_split = jnp.stack([wr_hi, (wr_t - wr_hi.astype(F32)).astype(BF16)])
        hx, meta = _cross_router(h, ln_cross[layer], xq_w[layer].astype(BF16), kv,
                                 xo_w[layer].astype(BF16), ln_ffn[layer], wr_split, br)

        src, pos, ea, eb, nused, gates = _route_tables(meta)
        h = _moe(hx, gates, ln_ffn[layer],
                 expert_w_gate.reshape(DEPTH * N_EXPERTS, D_MODEL, D_EXPERT),
                 expert_w_up.reshape(DEPTH * N_EXPERTS, D_MODEL, D_EXPERT),
                 expert_w_down.reshape(DEPTH * N_EXPERTS, D_EXPERT, D_MODEL),
                 src, ea + layer * N_EXPERTS, eb + layer * N_EXPERTS, nused)

    return _final_norm(h, pos, ln_final).reshape(1, SEQ, D_MODEL)
```

```python
import functools
import math

import numpy as np
import jax
import jax.numpy as jnp
from jax import lax
from jax.experimental import pallas as pl
from jax.experimental.pallas import tpu as pltpu

F32 = jnp.float32
BF16 = jnp.bfloat16

D_MODEL = 1024
SEQ = 16384
DEPTH = 4
MEM_LEN = 256
EPS = 1e-6
NEG_INF = -1e30

A_GROUPS = 4
A_CH = 128
A_WIDTH = A_GROUPS * A_CH
A_CHUNK = 128
B_HEADS = 8
B_KV_HEADS = 2
B_Q_PER_KV = B_HEADS // B_KV_HEADS
B_HEAD_DIM = 64
B_WIDTH = B_HEADS * B_HEAD_DIM
B_KV_WIDTH = B_KV_HEADS * B_HEAD_DIM
B_HALF_WINDOW = 128
B_BLOCK = 128
EVEN_IN = 2 * A_WIDTH + B_WIDTH + 2 * B_KV_WIDTH
EVEN_Q0 = 2 * A_WIDTH
EVEN_K0 = EVEN_Q0 + B_WIDTH
EVEN_V0 = EVEN_K0 + B_KV_WIDTH

C_PAIRS = ((128, 1), (512, 4), (2048, 16))
C_GROUPS = len(C_PAIRS)
C_HEADS = 8
C_HEAD_DIM = 128
C_WIDTH = C_HEADS * C_HEAD_DIM
C_BLOCK = 64
ODD_IN = C_GROUPS * 3 * C_WIDTH

REL_BUCKETS = 32
REL_MAX_DIST = 1024
REL_HEADS = 8

X_HEADS = 4
X_HEAD_DIM = 128
X_WIDTH = X_HEADS * X_HEAD_DIM

MOE_GROUPS = 4
MOE_EPG = 4
N_EXPERTS = MOE_GROUPS * MOE_EPG
D_EXPERT = 512
SLOT_A = (0, 0, 0, 1, 1, 3)
SLOT_B = (1, 2, 3, 3, 2, 2)
N_PAIRS = len(SLOT_A)
N_BUCKETS = MOE_GROUPS * N_PAIRS

V7X_LANES = 128
ROW_CHUNKS = D_MODEL // V7X_LANES
V7X_VMEM_BYTES = 64 * 1024 * 1024
VMEM_LIMIT = 56 * 1024 * 1024

PROJ_TM = 1024
PROJ_TN = 1024
EVEN_TB = 512
EVEN_STACK = 2
ODD_TILE = PROJ_TM
ODD_BLOCKS = ODD_TILE // C_BLOCK
ODD_INTERLEAVE = 4
COMB_TB = 512
CROSS_TB = 512
MOE_TM = 512
ROUTER_ROWS = 32
MOE_TILES = (SEQ + N_BUCKETS * (MOE_TM - 1)) // MOE_TM
GATHER_TN = 1536
FINAL_TB = 512


def _cparams(*sem):
    return pltpu.CompilerParams(dimension_semantics=sem, vmem_limit_bytes=VMEM_LIMIT)


def _rms(x, g):
    return x * lax.rsqrt(jnp.mean(x * x, axis=-1, keepdims=True) + EPS) * g


def _dot(a, b):
    return jnp.dot(a, b, preferred_element_type=F32)


def _dot_nt(a, b):
    return lax.dot_general(a, b, (((1,), (1,)), ((), ())), preferred_element_type=F32)


def _load_rows(ref, n, lead=()):
    return jnp.concatenate([ref[lead + (pl.ds(c, n, stride=ROW_CHUNKS), slice(None))]
                            for c in range(ROW_CHUNKS)], axis=1)


def _store_rows(ref, val, lead=()):
    n = val.shape[0]
    for c in range(ROW_CHUNKS):
        ref[lead + (pl.ds(c, n, stride=ROW_CHUNKS), slice(None))] = val[:, c * V7X_LANES:(c + 1) * V7X_LANES]


def _row_tile(idx):
    if isinstance(idx, int):
        return pl.ds(idx * ROW_CHUNKS, ROW_CHUNKS)
    return pl.ds(pl.multiple_of(idx * ROW_CHUNKS, ROW_CHUNKS), ROW_CHUNKS)


class _RowGather:
    def __init__(self, idx_ref, src_hbm, buf, sem, tm):
        self.idx_ref, self.src, self.buf, self.sem, self.tm = idx_ref, src_hbm, buf, sem, tm

    def start(self, tile, slot, rows):
        for r in rows:
            pltpu.make_async_copy(self.src.at[_row_tile(self.idx_ref[tile * self.tm + r])],
                                  self.buf.at[slot, _row_tile(r)], self.sem.at[slot]).start()

    def wait(self, slot):
        pltpu.make_async_copy(self.src.at[pl.ds(0, self.tm * ROW_CHUNKS)], self.buf.at[slot],
                              self.sem.at[slot]).wait()


def _fill_xn(xf, xn_ref, scratch, dil):
    tm = xn_ref.shape[0]
    seg = tm // dil
    if dil == 1:
        xn_ref[...] = xf.astype(BF16)
        return
    xs_ref, = scratch
    for c in range(ROW_CHUNKS):
        xs_ref[c] = xf[:, c * V7X_LANES:(c + 1) * V7X_LANES]
    for r in range(dil):
        for c in range(ROW_CHUNKS):
            xn_ref[r * seg:(r + 1) * seg, c * V7X_LANES:(c + 1) * V7X_LANES] = (
                xs_ref[c, pl.ds(r, seg, stride=dil), :].astype(BF16))


def _proj_out(acc, o_ref, gelu_cols, dil):
    seg = acc.shape[0] // dil
    if gelu_cols:
        o_ref[0, :, :gelu_cols] = jax.nn.gelu(acc[:, :gelu_cols]).astype(o_ref.dtype)
        o_ref[0, :, gelu_cols:] = acc[:, gelu_cols:].astype(o_ref.dtype)
    else:
        for r in range(dil):
            o_ref[r] = acc[r * seg:(r + 1) * seg].astype(o_ref.dtype)


def _proj_kernel(h_ref, g_ref, w_ref, o_ref, xn_ref, *scratch, gelu_cols, dil):
    @pl.when(pl.program_id(1) == 0)
    def _():
        _fill_xn(_rms(h_ref[...], g_ref[...]), xn_ref, scratch, dil)

    _proj_out(_dot(xn_ref[...], w_ref[...].astype(BF16)), o_ref, gelu_cols, dil)


def _gather_proj_kernel(pos_ref, hs_hbm, g_ref, w_ref, o_ref, hnat_ref, xn_ref, hbuf, sem,
                        *, gelu_cols, ni, nj):
    i = pl.program_id(0)
    j = pl.program_id(1)
    tm = xn_ref.shape[0]
    per = tm // nj
    slot = i % 2
    gather = _RowGather(pos_ref, hs_hbm, hbuf, sem, tm)

    @pl.when((i == 0) & (j == 0))
    def _():
        gather.start(0, 0, range(tm))

    @pl.when(j == 0)
    def _():
        gather.wait(slot)
        h = _load_rows(hbuf, tm, (slot,))
        hnat_ref[...] = h
        _fill_xn(_rms(h, g_ref[...]), xn_ref, (), 1)

    nxt = jnp.where(i + 1 < ni, i + 1, 0)
    gather.start(nxt, 1 - slot, [j * per + r for r in range(per)])
    _proj_out(_dot(xn_ref[...], w_ref[...].astype(BF16)), o_ref, gelu_cols, 1)

    @pl.when((i == ni - 1) & (j == nj - 1))
    def _():
        gather.wait(1 - slot)


def _proj(h, g, w, *, rows, tm, tn, n, w_lead=0, w_col0=0, gelu_cols=0, dil=1, pos=None):
    seg = tm // dil
    ni, nj = rows // tm, n // tn
    out_z = jax.ShapeDtypeStruct((dil, rows // dil, n), BF16)
    if pos is None:
        return pl.pallas_call(
            functools.partial(_proj_kernel, gelu_cols=gelu_cols, dil=dil),
            grid=(ni, nj),
            in_specs=[
                pl.BlockSpec((tm, D_MODEL), lambda i, j: (i, 0)),
                pl.BlockSpec((1, D_MODEL), lambda i, j: (0, 0)),
                pl.BlockSpec((None, D_MODEL, tn), lambda i, j: (w_lead, 0, w_col0 + j)),
            ],
            out_specs=pl.BlockSpec((dil, seg, tn), lambda i, j: (0, i, j)),
            out_shape=out_z,
            scratch_shapes=[pltpu.VMEM((tm, D_MODEL), BF16)] + (
                [pltpu.VMEM((ROW_CHUNKS, tm, V7X_LANES), F32)] if dil > 1 else []),
            compiler_params=_cparams("parallel", "arbitrary"),
            name=f"norm_proj_d{dil}",
        )(h, g.reshape(1, D_MODEL), w)
    assert dil == 1 and tm % nj == 0
    grid_spec = pltpu.PrefetchScalarGridSpec(
        num_scalar_prefetch=1,
        grid=(ni, nj),
        in_specs=[
            pl.BlockSpec(memory_space=pl.ANY),
            pl.BlockSpec((1, D_MODEL), lambda i, j, p: (0, 0)),
            pl.BlockSpec((None, D_MODEL, tn), lambda i, j, p: (w_lead, 0, w_col0 + j)),
        ],
        out_specs=[pl.BlockSpec((1, tm, tn), lambda i, j, p: (0, i, j)),
                   pl.BlockSpec((tm, D_MODEL), lambda i, j, p: (i, 0))],
        scratch_shapes=[pltpu.VMEM((tm, D_MODEL), BF16),
                        pltpu.VMEM((2, tm * ROW_CHUNKS, V7X_LANES), F32),
                        pltpu.SemaphoreType.DMA((2,))],
    )
    return pl.pallas_call(
        functools.partial(_gather_proj_kernel, gelu_cols=gelu_cols, ni=ni, nj=nj),
        grid_spec=grid_spec,
        out_shape=[out_z, jax.ShapeDtypeStruct((rows, D_MODEL), F32)],
        compiler_params=_cparams("arbitrary", "arbitrary"),
        name="gather_norm_proj",
    )(pos, h, g.reshape(1, D_MODEL), w)


def _t5_bucket_np(rel):
    nb = REL_BUCKETS // 2
    max_exact = nb // 2
    ret = np.where(rel > 0, nb, 0)
    n = np.abs(rel)
    nf = np.maximum(n, 1).astype(np.float32)
    large = max_exact + (np.log(nf / np.float32(max_exact)) / np.float32(math.log(REL_MAX_DIST / max_exact))
                         * np.float32(nb - max_exact)).astype(np.int32)
    large = np.minimum(large, nb - 1)
    return (ret + np.where(n < max_exact, n, large)).astype(np.int32)


def _bias_kernel(table_ref, idx_ref, mask_ref, o_ref, *, block):
    idx = idx_ref[...]
    for h in range(REL_HEADS):
        acc = jnp.zeros(idx.shape, F32)
        for b in range(REL_BUCKETS):
            acc = jnp.where(idx == b, table_ref[b, h], acc)
        for v in range(3):
            o_ref[v, h * block:(h + 1) * block, :] = acc + mask_ref[v]


def _band_bias(table, block, half, dil):
    rel = np.arange(3 * block)[None, :] - block - np.arange(block)[:, None]
    band = np.abs(rel) <= half
    col = np.arange(3 * block)[None, :]
    masks = np.stack([band & (col >= block), band, band & (col < 2 * block)])
    add = np.where(masks, 0.0, NEG_INF).astype(np.float32)
    return pl.pallas_call(
        functools.partial(_bias_kernel, block=block),
        in_specs=[pl.BlockSpec(memory_space=pltpu.SMEM),
                  pl.BlockSpec(memory_space=pltpu.VMEM),
                  pl.BlockSpec(memory_space=pltpu.VMEM)],
        out_specs=pl.BlockSpec(memory_space=pltpu.VMEM),
        out_shape=jax.ShapeDtypeStruct((3, REL_HEADS * block, 3 * block), F32),
        name=f"rel_bias_d{dil}",
    )(table, jnp.asarray(_t5_bucket_np(rel * dil)), jnp.asarray(add))


def _even_mix_kernel(z_ref, kvp_ref, kvn_ref, h_ref, lng_ref, lnb_ref, ws_ref, bs_ref, bias_ref,
                     sink_ref, wout_ref, o_ref, kv_scr, y_scr):
    i = pl.program_id(0)
    nsub = EVEN_TB // B_BLOCK
    nblk = SEQ // B_BLOCK
    kv_scr[0:B_BLOCK] = kvp_ref[...]
    kv_scr[B_BLOCK:B_BLOCK + EVEN_TB] = z_ref[0, :, EVEN_K0:EVEN_IN]
    kv_scr[B_BLOCK + EVEN_TB:] = kvn_ref[...]
    lng = lng_ref[...]
    lnb = lnb_ref[...]
    for s in range(nsub):
        r0 = s * B_BLOCK
        gb = i * nsub + s
        sel = jnp.where(gb == 0, 0, jnp.where(gb == nblk - 1, 2, 1))
        u = z_ref[0, r0:r0 + A_CHUNK, 0:A_WIDTH].astype(F32)
        va = z_ref[0, r0:r0 + A_CHUNK, A_WIDTH:2 * A_WIDTH].astype(F32)
        mu = jnp.mean(va, axis=-1, keepdims=True)
        vc = va - mu
        var = jnp.mean(vc * vc, axis=-1, keepdims=True)
        vn = (vc * lax.rsqrt(var + EPS) * lng + lnb).astype(BF16)
        for g in range(A_GROUPS):
            c0 = g * A_CH
            mixed = _dot(ws_ref[g], vn[:, c0:c0 + A_CH]) + bs_ref[g]
            y_scr[r0:r0 + A_CHUNK, c0:c0 + A_CH] = (u[:, c0:c0 + A_CH] * mixed).astype(BF16)
        units = [(kh, kh * B_Q_PER_KV + half * EVEN_STACK)
                 for kh in range(B_KV_HEADS) for half in range(B_Q_PER_KV // EVEN_STACK)]
        lgs = []
        for kh, hd0 in units:
            kw = kv_scr[r0:r0 + 3 * B_BLOCK, kh * B_HEAD_DIM:(kh + 1) * B_HEAD_DIM]
            q = jnp.concatenate(
                [z_ref[0, r0:r0 + B_BLOCK,
                       EVEN_Q0 + (hd0 + g) * B_HEAD_DIM:EVEN_Q0 + (hd0 + g + 1) * B_HEAD_DIM]
                 for g in range(EVEN_STACK)], axis=0)
            lgs.append(_dot_nt(q, kw))
        lgs = [lg * (B_HEAD_DIM ** -0.5) + bias_ref[sel, hd0 * B_BLOCK:(hd0 + EVEN_STACK) * B_BLOCK, :]
               for lg, (_, hd0) in zip(lgs, units)]
        sks = [sink_ref[hd0 * B_BLOCK:(hd0 + EVEN_STACK) * B_BLOCK, :] for _, hd0 in units]
        ms = [jnp.maximum(jnp.max(lg, axis=-1, keepdims=True), sk) for lg, sk in zip(lgs, sks)]
        ps = [jnp.exp(lg - m) for lg, m in zip(lgs, ms)]
        dens = [jnp.sum(p, axis=-1, keepdims=True) + jnp.exp(sk - m) for p, sk, m in zip(ps, sks, ms)]
        os_ = []
        for p, (kh, _) in zip(ps, units):
            vw = kv_scr[r0:r0 + 3 * B_BLOCK,
                        B_KV_WIDTH + kh * B_HEAD_DIM:B_KV_WIDTH + (kh + 1) * B_HEAD_DIM]
            os_.append(_dot(p.astype(BF16), vw))
        for o, den, (_, hd0) in zip(os_, dens, units):
            o = o * (1.0 / den)
            for g in range(EVEN_STACK):
                c0 = A_WIDTH + (hd0 + g) * B_HEAD_DIM
                y_scr[r0:r0 + B_BLOCK, c0:c0 + B_HEAD_DIM] = o[g * B_BLOCK:(g + 1) * B_BLOCK].astype(BF16)
    o_ref[...] = h_ref[...] + _dot(y_scr[...], wout_ref[...])


def _even_mix(z, h, ln_g, ln_b, w_s, b_s, bias, sink, w_out):
    nsub = EVEN_TB // B_BLOCK
    nblk = SEQ // B_BLOCK
    kv_cb = EVEN_K0 // (2 * B_KV_WIDTH)
    sink_col = jnp.broadcast_to(sink.reshape(B_HEADS, 1, 1), (B_HEADS, B_BLOCK, 1)).reshape(
        B_HEADS * B_BLOCK, 1)
    return pl.pallas_call(
        _even_mix_kernel,
        grid=(SEQ // EVEN_TB,),
        in_specs=[
            pl.BlockSpec((1, EVEN_TB, EVEN_IN), lambda i: (0, i, 0)),
            pl.BlockSpec((None, B_BLOCK, 2 * B_KV_WIDTH),
                         lambda i: (0, jnp.maximum(i * nsub - 1, 0), kv_cb)),
            pl.BlockSpec((None, B_BLOCK, 2 * B_KV_WIDTH),
                         lambda i: (0, jnp.minimum((i + 1) * nsub, nblk - 1), kv_cb)),
            pl.BlockSpec((EVEN_TB, D_MODEL), lambda i: (i, 0)),
            pl.BlockSpec((1, A_WIDTH), lambda i: (0, 0)),
            pl.BlockSpec((1, A_WIDTH), lambda i: (0, 0)),
            pl.BlockSpec((A_GROUPS, A_CHUNK, A_CHUNK), lambda i: (0, 0, 0)),
            pl.BlockSpec((A_GROUPS, A_CHUNK, A_CH), lambda i: (0, 0, 0)),
            pl.BlockSpec((3, B_HEADS * B_BLOCK, 3 * B_BLOCK), lambda i: (0, 0, 0)),
            pl.BlockSpec((B_HEADS * B_BLOCK, 1), lambda i: (0, 0)),
            pl.BlockSpec((A_WIDTH + B_WIDTH, D_MODEL), lambda i: (0, 0)),
        ],
        out_specs=pl.BlockSpec((EVEN_TB, D_MODEL), lambda i: (i, 0)),
        out_shape=jax.ShapeDtypeStruct((SEQ, D_MODEL), F32),
        scratch_shapes=[
            pltpu.VMEM((EVEN_TB + 2 * B_BLOCK, 2 * B_KV_WIDTH), BF16),
            pltpu.VMEM((EVEN_TB, A_WIDTH + B_WIDTH), BF16),
        ],
        compiler_params=_cparams("parallel"),
        name="even_mixer",
    )(z, z, z, h, ln_g.reshape(1, A_WIDTH), ln_b.reshape(1, A_WIDTH), w_s.astype(BF16),
      jnp.broadcast_to(b_s[:, :, None], (A_GROUPS, A_CHUNK, A_CH)), bias, sink_col, w_out)


def _dil_attn_kernel(zc_ref, kp_ref, kn_ref, vp_ref, vn_ref, bias_ref, o_ref, lse_ref,
                     k_scr, v_scr, o_scr, lse_scr, *, dil):
    t = pl.program_id(0)
    seg = ODD_TILE // dil
    nsb = seg // C_BLOCK
    nblk = SEQ // dil // C_BLOCK
    k_scr[:, 0:C_BLOCK] = kp_ref[...]
    k_scr[:, C_BLOCK:C_BLOCK + seg] = zc_ref[:, :, C_WIDTH:2 * C_WIDTH]
    k_scr[:, C_BLOCK + seg:] = kn_ref[...]
    v_scr[:, 0:C_BLOCK] = vp_ref[...]
    v_scr[:, C_BLOCK:C_BLOCK + seg] = zc_ref[:, :, 2 * C_WIDTH:3 * C_WIDTH]
    v_scr[:, C_BLOCK + seg:] = vn_ref[...]
    lane = lax.broadcasted_iota(jnp.int32, (C_BLOCK, V7X_LANES), 1)

    def body(it, carry):
        blocks = []
        for u in range(ODD_INTERLEAVE):
            n = it * ODD_INTERLEAVE + u
            r = n // nsb
            s = n % nsb
            r0 = pl.multiple_of(s * C_BLOCK, C_BLOCK)
            gb = t * nsb + s
            sel = jnp.where(gb == 0, 0, jnp.where(gb == nblk - 1, 2, 1))
            rows = (pl.ds(s * (C_BLOCK * dil) + r, C_BLOCK, stride=dil) if dil > 1
                    else pl.ds(r0, C_BLOCK))
            blocks.append((r, r0, sel, rows))
        lgs = []
        for r, r0, _, _ in blocks:
            for hd in range(C_HEADS):
                c0 = hd * C_HEAD_DIM
                q = zc_ref[r, pl.ds(r0, C_BLOCK), c0:c0 + C_HEAD_DIM]
                kw = k_scr[r, pl.ds(r0, 3 * C_BLOCK), c0:c0 + C_HEAD_DIM]
                lgs.append(_dot_nt(q, kw))
        lg = [jnp.concatenate(lgs[u * C_HEADS:(u + 1) * C_HEADS], axis=0) * (C_HEAD_DIM ** -0.5)
              + bias_ref[blk[2]] for u, blk in enumerate(blocks)]
        m = [jnp.max(x, axis=-1, keepdims=True) for x in lg]
        p = [jnp.exp(x - mm) for x, mm in zip(lg, m)]
        den = [jnp.sum(x, axis=-1, keepdims=True) for x in p]
        inv = [1.0 / d for d in den]
        lse = [mm + jnp.log(d) for mm, d in zip(m, den)]
        pb = [x.astype(BF16) for x in p]
        outs = []
        for u, (r, r0, _, _) in enumerate(blocks):
            for hd in range(C_HEADS):
                c0 = hd * C_HEAD_DIM
                vw = v_scr[r, pl.ds(r0, 3 * C_BLOCK), c0:c0 + C_HEAD_DIM]
                outs.append(_dot(pb[u][hd * C_BLOCK:(hd + 1) * C_BLOCK], vw))
        for u, (_, _, _, rows) in enumerate(blocks):
            lse_tile = jnp.zeros((C_BLOCK, V7X_LANES), F32)
            for hd in range(C_HEADS):
                o_scr[hd, rows, :] = outs[u * C_HEADS + hd] * inv[u][hd * C_BLOCK:(hd + 1) * C_BLOCK]
                lse_tile = jnp.where(lane == hd, lse[u][hd * C_BLOCK:(hd + 1) * C_BLOCK], lse_tile)
            lse_scr[rows, :] = lse_tile
        return carry

    lax.fori_loop(0, ODD_BLOCKS // ODD_INTERLEAVE, body, 0)
    for hd in range(C_HEADS):
        o_ref[:, hd * C_HEAD_DIM:(hd + 1) * C_HEAD_DIM] = o_scr[hd].astype(o_ref.dtype)
    lse_ref[...] = lse_scr[...]


def _dil_attn(zg, bias, dil):
    seg = ODD_TILE // dil
    nsb = seg // C_BLOCK
    last = SEQ // dil // C_BLOCK - 1

    def halo(j, nxt):
        if nxt:
            return pl.BlockSpec((dil, C_BLOCK, C_WIDTH),
                                lambda t: (0, jnp.minimum((t + 1) * nsb, last), j))
        return pl.BlockSpec((dil, C_BLOCK, C_WIDTH), lambda t: (0, jnp.maximum(t * nsb - 1, 0), j))

    return pl.pallas_call(
        functools.partial(_dil_attn_kernel, dil=dil),
        grid=(SEQ // ODD_TILE,),
        in_specs=[pl.BlockSpec((dil, seg, 3 * C_WIDTH), lambda t: (0, t, 0)),
                  halo(1, False), halo(1, True), halo(2, False), halo(2, True),
                  pl.BlockSpec((3, C_HEADS * C_BLOCK, 3 * C_BLOCK), lambda t: (0, 0, 0))],
        out_specs=[pl.BlockSpec((ODD_TILE, C_WIDTH), lambda t: (t, 0)),
                   pl.BlockSpec((ODD_TILE, V7X_LANES), lambda t: (t, 0))],
        out_shape=[jax.ShapeDtypeStruct((SEQ, C_WIDTH), BF16),
                   jax.ShapeDtypeStruct((SEQ, V7X_LANES), F32)],
        scratch_shapes=[pltpu.VMEM((dil, seg + 2 * C_BLOCK, C_WIDTH), BF16),
                        pltpu.VMEM((dil, seg + 2 * C_BLOCK, C_WIDTH), BF16),
                        pltpu.VMEM((C_HEADS, ODD_TILE, C_HEAD_DIM), F32),
                        pltpu.VMEM((ODD_TILE, V7X_LANES), F32)],
        compiler_params=_cparams("parallel"),
        name=f"dilated_attn_d{dil}",
    )(zg, zg, zg, zg, zg, bias)


def _combine_kernel(o0_ref, o1_ref, o2_ref, l0_ref, l1_ref, l2_ref, h_ref, wout_ref, out_ref, y_scr):
    l0 = l0_ref[...]
    l1 = l1_ref[...]
    l2 = l2_ref[...]
    m = jnp.maximum(jnp.maximum(l0, l1), l2)
    e0 = jnp.exp(l0 - m)
    e1 = jnp.exp(l1 - m)
    e2 = jnp.exp(l2 - m)
    tot = e0 + e1 + e2
    w0 = e0 / tot
    w1 = e1 / tot
    w2 = e2 / tot
    for hd in range(C_HEADS):
        c0 = hd * C_HEAD_DIM
        y = (w0[:, hd:hd + 1] * o0_ref[:, c0:c0 + C_HEAD_DIM].astype(F32)
             + w1[:, hd:hd + 1] * o1_ref[:, c0:c0 + C_HEAD_DIM].astype(F32)
             + w2[:, hd:hd + 1] * o2_ref[:, c0:c0 + C_HEAD_DIM].astype(F32))
        y_scr[:, c0:c0 + C_HEAD_DIM] = y.astype(BF16)
    out_ref[...] = h_ref[...] + _dot(y_scr[...], wout_ref[...])


def _combine(outs, lses, h, w_out):
    blk_o = pl.BlockSpec((COMB_TB, C_WIDTH), lambda i: (i, 0))
    blk_l = pl.BlockSpec((COMB_TB, V7X_LANES), lambda i: (i, 0))
    return pl.pallas_call(
        _combine_kernel,
        grid=(SEQ // COMB_TB,),
        in_specs=[blk_o, blk_o, blk_o, blk_l, blk_l, blk_l,
                  pl.BlockSpec((COMB_TB, D_MODEL), lambda i: (i, 0)),
                  pl.BlockSpec((C_WIDTH, D_MODEL), lambda i: (0, 0))],
        out_specs=pl.BlockSpec((COMB_TB, D_MODEL), lambda i: (i, 0)),
        out_shape=jax.ShapeDtypeStruct((SEQ, D_MODEL), F32),
        scratch_shapes=[pltpu.VMEM((COMB_TB, C_WIDTH), BF16)],
        compiler_params=_cparams("parallel"),
        name="group_combine_proj",
    )(*outs, *lses, h, w_out)


def _cross_kernel(h_ref, gx_ref, wq_ref, kv_ref, wo_ref, gf_ref, wr_ref, br_ref,
                  hx_ref, meta_ref, o_scr):
    h = h_ref[...]
    q = _dot(_rms(h, gx_ref[...]).astype(BF16), wq_ref[...]).astype(BF16)
    for hd in range(X_HEADS):
        c0 = hd * X_HEAD_DIM
        lg = _dot_nt(q[:, c0:c0 + X_HEAD_DIM], kv_ref[0, :, c0:c0 + X_HEAD_DIM]) * (X_HEAD_DIM ** -0.5)
        m = jnp.max(lg, axis=-1, keepdims=True)
        p = jnp.exp(lg - m)
        den = jnp.sum(p, axis=-1, keepdims=True)
        o = _dot(p.astype(BF16), kv_ref[0, :, X_WIDTH + c0:X_WIDTH + c0 + X_HEAD_DIM]) / den
        o_scr[:, c0:c0 + X_HEAD_DIM] = o.astype(BF16)
    h2 = h + _dot(o_scr[...], wo_ref[...])
    _store_rows(hx_ref, h2)

    t = _rms(h2, gf_ref[...])
    t_hi = t.astype(BF16)
    t_lo = (t - t_hi.astype(F32)).astype(BF16)
    lt = (_dot_nt(wr_ref[0], t_hi) + _dot_nt(wr_ref[0], t_lo) + _dot_nt(wr_ref[1], t_hi)) + br_ref[...]
    g = [lt[k:k + 1, :] for k in range(MOE_GROUPS)]
    gmax = jnp.maximum(jnp.maximum(g[0], g[1]), jnp.maximum(g[2], g[3]))
    grp = jnp.where(g[0] == gmax, 0, jnp.where(g[1] == gmax, 1, jnp.where(g[2] == gmax, 2, 3)))
    g_gate = 1.0 / (jnp.exp(g[0] - gmax) + jnp.exp(g[1] - gmax) + jnp.exp(g[2] - gmax)
                    + jnp.exp(g[3] - gmax))
    e = []
    for k in range(MOE_EPG):
        rows = [lt[MOE_GROUPS + gi * MOE_EPG + k:MOE_GROUPS + gi * MOE_EPG + k + 1, :]
                for gi in range(MOE_GROUPS)]
        e.append(jnp.where(grp == 0, rows[0], jnp.where(grp == 1, rows[1],
                                                         jnp.where(grp == 2, rows[2], rows[3]))))
    v1 = jnp.maximum(jnp.maximum(e[0], e[1]), jnp.maximum(e[2], e[3]))
    i1 = jnp.where(e[0] == v1, 0, jnp.where(e[1] == v1, 1, jnp.where(e[2] == v1, 2, 3)))
    r = [jnp.where(i1 == k, -jnp.inf, e[k]) for k in range(MOE_EPG)]
    v2 = jnp.maximum(jnp.maximum(r[0], r[1]), jnp.maximum(r[2], r[3]))
    i2 = jnp.where(r[0] == v2, 0, jnp.where(r[1] == v2, 1, jnp.where(r[2] == v2, 2, 3)))
    d = jnp.exp(v2 - v1)
    w1 = g_gate / (1.0 + d)
    w2 = g_gate * d / (1.0 + d)
    first_lo = i1 < i2
    lo = jnp.where(first_lo, i1, i2)
    hi = jnp.where(first_lo, i2, i1)
    w_lo = jnp.where(first_lo, w1, w2)
    w_hi = jnp.where(first_lo, w2, w1)
    pair = jnp.where(lo == 0, hi - 1, jnp.where(lo == 1, jnp.where(hi == 3, 3, 4), 5))
    w_a = jnp.where(lo == 2, w_hi, w_lo)
    w_b = jnp.where(lo == 2, w_lo, w_hi)
    bucket = (grp * N_PAIRS + pair).astype(F32)
    row = lax.broadcasted_iota(jnp.int32, (8, CROSS_TB), 0)
    meta_ref[...] = jnp.where(row == 0, bucket, jnp.where(row == 1, w_a, jnp.where(row == 2, w_b, 0.0)))


def _cross_router(h, g_cross, wq, kv, wo, g_ffn, wr_t, br):
    full = lambda shape: pl.BlockSpec(shape, lambda i: tuple(0 for _ in shape))
    return pl.pallas_call(
        _cross_kernel,
        grid=(SEQ // CROSS_TB,),
        in_specs=[
            pl.BlockSpec((CROSS_TB, D_MODEL), lambda i: (i, 0)),
            full((1, D_MODEL)),
            full((D_MODEL, X_WIDTH)),
            full((1, MEM_LEN, 2 * X_WIDTH)),
            full((X_WIDTH, D_MODEL)),
            full((1, D_MODEL)),
            full((2, ROUTER_ROWS, D_MODEL)),
            full((ROUTER_ROWS, 1)),
        ],
        out_specs=[pl.BlockSpec((CROSS_TB * ROW_CHUNKS, V7X_LANES), lambda i: (i, 0)),
                   pl.BlockSpec((8, CROSS_TB), lambda i: (0, i))],
        out_shape=[jax.ShapeDtypeStruct((SEQ * ROW_CHUNKS, V7X_LANES), F32),
                   jax.ShapeDtypeStruct((8, SEQ), F32)],
        scratch_shapes=[pltpu.VMEM((CROSS_TB, X_WIDTH), BF16)],
        compiler_params=_cparams("parallel"),
        name="cross_attn_router",
    )(h, g_cross.reshape(1, D_MODEL), wq, kv, wo, g_ffn.reshape(1, D_MODEL), wr_t, br)


def _moe_kernel(src_ref, ea_ref, eb_ref, nused_ref,
                hx_hbm, gates_ref, gf_ref, wga_ref, wua_ref, wda_ref, wgb_ref, wub_ref, wdb_ref,
                out_ref, xbuf, wup_a, wdn_a, wup_b, wdn_b, gsem):
    k = pl.program_id(0)
    nused = nused_ref[0]
    xslot = k % 3
    gather = _RowGather(src_ref, hx_hbm, xbuf, gsem, MOE_TM)

    @pl.when(k == 0)
    def _():
        gather.start(0, 0, range(MOE_TM))
        gather.start(1, 1, range(MOE_TM))

    prev = jnp.maximum(k - 1, 0)

    @pl.when((k < nused) & ((k == 0) | (ea_ref[k] != ea_ref[prev])))
    def _():
        wup_a[:, :D_EXPERT] = wga_ref[0].astype(BF16)
        wup_a[:, D_EXPERT:] = wua_ref[0].astype(BF16)
        wdn_a[...] = wda_ref[0].astype(BF16)

    @pl.when((k < nused) & ((k == 0) | (eb_ref[k] != eb_ref[prev])))
    def _():
        wup_b[:, :D_EXPERT] = wgb_ref[0].astype(BF16)
        wup_b[:, D_EXPERT:] = wub_ref[0].astype(BF16)
        wdn_b[...] = wdb_ref[0].astype(BF16)

    @pl.when(k < nused)
    def _():
        gather.wait(xslot)
        h2 = _load_rows(xbuf, MOE_TM, (xslot,))
        gather.start(k + 2, (k + 2) % 3, range(MOE_TM))
        t = _rms(h2, gf_ref[...]).astype(BF16)
        y = jnp.zeros((MOE_TM, D_MODEL), F32)
        for col, wup, wdn in ((0, wup_a, wdn_a), (1, wup_b, wdn_b)):
            gate = gates_ref[:, col:col + 1]
            gu = _dot(t, wup[...])
            hid = jax.nn.silu(gu[:, :D_EXPERT]) * gu[:, D_EXPERT:] * gate
            y = y + _dot(hid.astype(BF16), wdn[...])
        _store_rows(out_ref, h2 + y)

    @pl.when(k >= nused)
    def _():
        out_ref[...] = jnp.zeros(out_ref.shape, F32)

    @pl.when(k == nused - 1)
    def _():
        gather.wait((k + 1) % 3)
        gather.wait((k + 2) % 3)


def _moe(hx, gates, g_ffn, w_gate, w_up, w_down, src, ea, eb, nused):
    def wspec(shape, which):
        if which == 0:
            return pl.BlockSpec((1,) + shape, lambda k, s, a, b, n: (a[k], 0, 0))
        return pl.BlockSpec((1,) + shape, lambda k, s, a, b, n: (b[k], 0, 0))

    up_shape = (D_MODEL, D_EXPERT)
    down_shape = (D_EXPERT, D_MODEL)
    grid_spec = pltpu.PrefetchScalarGridSpec(
        num_scalar_prefetch=4,
        grid=(MOE_TILES,),
        in_specs=[
            pl.BlockSpec(memory_space=pl.ANY),
            pl.BlockSpec((MOE_TM, V7X_LANES), lambda k, s, a, b, n: (k, 0)),
            pl.BlockSpec((1, D_MODEL), lambda k, s, a, b, n: (0, 0)),
            wspec(up_shape, 0), wspec(up_shape, 0), wspec(down_shape, 0),
            wspec(up_shape, 1), wspec(up_shape, 1), wspec(down_shape, 1),
        ],
        out_specs=pl.BlockSpec((MOE_TM * ROW_CHUNKS, V7X_LANES), lambda k, s, a, b, n: (k, 0)),
        scratch_shapes=[
            pltpu.VMEM((3, MOE_TM * ROW_CHUNKS, V7X_LANES), F32),
            pltpu.VMEM((D_MODEL, 2 * D_EXPERT), BF16),
            pltpu.VMEM((D_EXPERT, D_MODEL), BF16),
            pltpu.VMEM((D_MODEL, 2 * D_EXPERT), BF16),
            pltpu.VMEM((D_EXPERT, D_MODEL), BF16),
            pltpu.SemaphoreType.DMA((3,)),
        ],
    )
    return pl.pallas_call(
        _moe_kernel,
        grid_spec=grid_spec,
        out_shape=jax.ShapeDtypeStruct((MOE_TILES * MOE_TM * ROW_CHUNKS, V7X_LANES), F32),
        compiler_params=_cparams("arbitrary"),
        name="routed_moe",
    )(src, ea, eb, nused, hx, gates, g_ffn.reshape(1, D_MODEL),
      w_gate, w_up, w_down, w_gate, w_up, w_down)


def _route_tables(meta):
    bucket = meta[0].astype(jnp.int32)
    ids = jnp.arange(N_BUCKETS, dtype=jnp.int32)
    counts = jnp.sum((bucket[:, None] == ids[None, :]).astype(jnp.int32), axis=0)
    ntile = (counts + MOE_TM - 1) // MOE_TM
    pad = ntile * MOE_TM - counts
    tile_end = jnp.cumsum(ntile)
    nused = tile_end[-1]
    dummy_key = jnp.where(jnp.arange(MOE_TM - 1, dtype=jnp.int32)[None, :] < pad[:, None],
                          ids[:, None], N_BUCKETS)
    keys = jnp.concatenate([bucket, dummy_key.reshape(-1)])
    vals = jnp.concatenate([jnp.arange(SEQ, dtype=jnp.int32),
                            jnp.full((N_BUCKETS * (MOE_TM - 1),), SEQ, jnp.int32)])
    nslot = MOE_TILES * MOE_TM
    zpad = jnp.zeros((N_BUCKETS * (MOE_TM - 1),), F32)
    _, tok, ga, gb = lax.sort((keys, vals, jnp.concatenate([meta[1], zpad]), jnp.concatenate([meta[2], zpad])),
                              num_keys=1, is_stable=True)
    tok = tok[:nslot]
    gates = jnp.pad(jnp.stack([ga[:nslot], gb[:nslot]], axis=1), ((0, 0), (0, V7X_LANES - 2)))
    valid = tok < SEQ
    src = jnp.concatenate([jnp.where(valid, tok, 0), jnp.zeros((2 * MOE_TM,), jnp.int32)])
    _, pos = lax.sort((tok, jnp.arange(nslot, dtype=jnp.int32)), num_keys=1, is_stable=True)
    pos = pos[:SEQ]
    tiles = jnp.arange(MOE_TILES, dtype=jnp.int32)
    tile_bucket = jnp.minimum(jnp.sum((tiles[:, None] >= tile_end[None, :]).astype(jnp.int32), axis=1),
                              N_BUCKETS - 1)
    onehot = (tile_bucket[:, None] == ids[None, :]).astype(jnp.int32)
    base = (np.arange(N_BUCKETS) // N_PAIRS) * MOE_EPG
    ea = jnp.sum(onehot * jnp.asarray(base + np.asarray(SLOT_A)[np.arange(N_BUCKETS) % N_PAIRS],
                                      jnp.int32)[None, :], axis=1)
    eb = jnp.sum(onehot * jnp.asarray(base + np.asarray(SLOT_B)[np.arange(N_BUCKETS) % N_PAIRS],
                                      jnp.int32)[None, :], axis=1)
    return (src.astype(jnp.int32), pos.astype(jnp.int32), ea.astype(jnp.int32), eb.astype(jnp.int32),
            nused.reshape(1).astype(jnp.int32), gates)


def _final_norm_kernel(pos_ref, hs_hbm, g_ref, o_ref, hbuf, sem, *, ni):
    i = pl.program_id(0)
    tm = o_ref.shape[0]
    slot = i % 2
    gather = _RowGather(pos_ref, hs_hbm, hbuf, sem, tm)

    @pl.when(i == 0)
    def _():
        gather.start(0, 0, range(tm))

    @pl.when(i + 1 < ni)
    def _():
        gather.start(i + 1, 1 - slot, range(tm))

    gather.wait(slot)
    o_ref[...] = _rms(_load_rows(hbuf, tm, (slot,)), g_ref[...])


def _final_norm(hs, pos, g):
    tb = FINAL_TB
    ni = SEQ // tb
    grid_spec = pltpu.PrefetchScalarGridSpec(
        num_scalar_prefetch=1,
        grid=(ni,),
        in_specs=[pl.BlockSpec(memory_space=pl.ANY),
                  pl.BlockSpec((1, D_MODEL), lambda i, p: (0, 0))],
        out_specs=pl.BlockSpec((tb, D_MODEL), lambda i, p: (i, 0)),
        scratch_shapes=[pltpu.VMEM((2, tb * ROW_CHUNKS, V7X_LANES), F32),
                        pltpu.SemaphoreType.DMA((2,))],
    )
    return pl.pallas_call(
        functools.partial(_final_norm_kernel, ni=ni),
        grid_spec=grid_spec,
        out_shape=jax.ShapeDtypeStruct((SEQ, D_MODEL), F32),
        compiler_params=_cparams("arbitrary"),
        name="final_norm",
    )(pos, hs, g.reshape(1, D_MODEL))


def kernel(x, mem, ln_mix, ln_cross, ln_mem, ln_ffn, ln_final, rel_table, even_w_in, even_w_out,
           sgu_ln_g, sgu_ln_b, sgu_w, sgu_b, attn_sink, odd_w_in, odd_w_out, xq_w, xkv_w, xo_w,
           router_group_w, router_group_b, router_expert_w, router_expert_b,
           expert_w_gate, expert_w_up, expert_w_down):
    h = x.reshape(SEQ, D_MODEL)
    mem2 = mem.reshape(MEM_LEN, D_MODEL)
    bias_even = _band_bias(rel_table, B_BLOCK, B_HALF_WINDOW, 1)
    bias_odd = [_band_bias(rel_table, C_BLOCK, window // 2 // dil, dil) for window, dil in C_PAIRS]

    pos = None
    for layer in range(DEPTH):
        i = layer // 2
        if layer % 2 == 0:
            z = _proj(h, ln_mix[layer], even_w_in, w_lead=i, rows=SEQ, tm=EVEN_TB, tn=EVEN_IN, n=EVEN_IN,
                      gelu_cols=2 * A_WIDTH, pos=pos)
            if pos is not None:
                z, h = z
            h = _even_mix(z, h, sgu_ln_g[i], sgu_ln_b[i], sgu_w[i], sgu_b[i], bias_even,
                          attn_sink[i], even_w_out[i].astype(BF16))
        else:
            zg, h = _proj(h, ln_mix[layer], odd_w_in, w_lead=i, rows=SEQ, tm=PROJ_TM, tn=GATHER_TN,
                          n=3 * C_WIDTH, pos=pos)
            outs, lses = [], []
            for gi, (_, dil) in enumerate(C_PAIRS):
                if gi > 0:
                    zg = _proj(h, ln_mix[layer], odd_w_in, w_lead=i, w_col0=gi * 3 * C_WIDTH // PROJ_TN,
                               rows=SEQ, tm=PROJ_TM, tn=PROJ_TN, n=3 * C_WIDTH, dil=dil)
                o, lse = _dil_attn(zg, bias_odd[gi], dil)
                outs.append(o)
                lses.append(lse)
            h = _combine(outs, lses, h, odd_w_out[i].astype(BF16))

        kv = _proj(mem2, ln_mem[layer], xkv_w, w_lead=layer, rows=MEM_LEN, tm=MEM_LEN,
                   tn=2 * X_WIDTH, n=2 * X_WIDTH)
        wr_t = jnp.zeros((ROUTER_ROWS, D_MODEL), F32)
        wr_t = wr_t.at[:MOE_GROUPS].set(router_group_w[layer].T)
        wr_t = wr_t.at[MOE_GROUPS:MOE_GROUPS + N_EXPERTS].set(
            router_expert_w[layer].reshape(D_MODEL, N_EXPERTS).T)
        br = jnp.zeros((ROUTER_ROWS, 1), F32)
        br = br.at[:MOE_GROUPS, 0].set(router_group_b[layer])
        br = br.at[MOE_GROUPS:MOE_GROUPS + N_EXPERTS, 0].set(router_expert_b[layer].reshape(N_EXPERTS))
        wr_hi = wr_t.astype(BF16)
        wr_split = jnp.stack([wr_hi, (wr_t - wr_hi.astype(F32)).astype(BF16)])
        hx, meta = _cross_router(h, ln_cross[layer], xq_w[layer].astype(BF16), kv,
                                 xo_w[layer].astype(BF16), ln_ffn[layer], wr_split, br)

        src, pos, ea, eb, nused, gates = _route_tables(meta)
        h = _moe(hx, gates, ln_ffn[layer],
                 expert_w_gate.reshape(DEPTH * N_EXPERTS, D_MODEL, D_EXPERT),
                 expert_w_up.reshape(DEPTH * N_EXPERTS, D_MODEL, D_EXPERT),
                 expert_w_down.reshape(DEPTH * N_EXPERTS, D_EXPERT, D_MODEL),
                 src, ea + layer * N_EXPERTS, eb + layer * N_EXPERTS, nused)

    return _final_norm(h, pos, ln_final).reshape(1, SEQ, D_MODEL)
```

```python
import functools
import math

import numpy as np
import jax
import jax.numpy as jnp
from jax import lax
from jax.experimental import pallas as pl
from jax.experimental.pallas import tpu as pltpu

F32 = jnp.float32
BF16 = jnp.bfloat16

D_MODEL = 1024
SEQ = 16384
DEPTH = 4
MEM_LEN = 256
EPS = 1e-6
NEG_INF = -1e30

A_GROUPS = 4
A_CH = 128
A_WIDTH = A_GROUPS * A_CH
A_CHUNK = 128
B_HEADS = 8
B_KV_HEADS = 2
B_Q_PER_KV = B_HEADS // B_KV_HEADS
B_HEAD_DIM = 64
B_WIDTH = B_HEADS * B_HEAD_DIM
B_KV_WIDTH = B_KV_HEADS * B_HEAD_DIM
B_HALF_WINDOW = 128
B_BLOCK = 128
EVEN_IN = 2 * A_WIDTH + B_WIDTH + 2 * B_KV_WIDTH
EVEN_Q0 = 2 * A_WIDTH
EVEN_K0 = EVEN_Q0 + B_WIDTH
EVEN_V0 = EVEN_K0 + B_KV_WIDTH

C_PAIRS = ((128, 1), (512, 4), (2048, 16))
C_GROUPS = len(C_PAIRS)
C_HEADS = 8
C_HEAD_DIM = 128
C_WIDTH = C_HEADS * C_HEAD_DIM
C_BLOCK = 64
ODD_IN = C_GROUPS * 3 * C_WIDTH

REL_BUCKETS = 32
REL_MAX_DIST = 1024
REL_HEADS = 8

X_HEADS = 4
X_HEAD_DIM = 128
X_WIDTH = X_HEADS * X_HEAD_DIM

MOE_GROUPS = 4
MOE_EPG = 4
N_EXPERTS = MOE_GROUPS * MOE_EPG
D_EXPERT = 512
SLOT_A = (0, 0, 0, 1, 1, 3)
SLOT_B = (1, 2, 3, 3, 2, 2)
N_PAIRS = len(SLOT_A)
N_BUCKETS = MOE_GROUPS * N_PAIRS

V7X_LANES = 128
ROW_CHUNKS = D_MODEL // V7X_LANES
V7X_VMEM_BYTES = 64 * 1024 * 1024
VMEM_LIMIT = 56 * 1024 * 1024

PROJ_TM = 1024
PROJ_TN = 1024
EVEN_TB = 512
EVEN_STACK = 2
ODD_TILE = PROJ_TM
ODD_BLOCKS = ODD_TILE // C_BLOCK
ODD_INTERLEAVE = 4
COMB_TB = 512
CROSS_TB = 512
MOE_TM = 256
ROUTER_ROWS = 32
MOE_TILES = (SEQ + N_BUCKETS * (MOE_TM - 1)) // MOE_TM
GATHER_TN = 1536
FINAL_TB = 512


def _cparams(*sem):
    return pltpu.CompilerParams(dimension_semantics=sem, vmem_limit_bytes=VMEM_LIMIT)


def _rms(x, g):
    return x * lax.rsqrt(jnp.mean(x * x, axis=-1, keepdims=True) + EPS) * g


def _dot(a, b):
    return jnp.dot(a, b, preferred_element_type=F32)


def _dot_nt(a, b):
    return lax.dot_general(a, b, (((1,), (1,)), ((), ())), preferred_element_type=F32)


def _load_rows(ref, n, lead=()):
    return jnp.concatenate([ref[lead + (pl.ds(c, n, stride=ROW_CHUNKS), slice(None))]
                            for c in range(ROW_CHUNKS)], axis=1)


def _store_rows(ref, val, lead=()):
    n = val.shape[0]
    for c in range(ROW_CHUNKS):
        ref[lead + (pl.ds(c, n, stride=ROW_CHUNKS), slice(None))] = val[:, c * V7X_LANES:(c + 1) * V7X_LANES]


def _row_tile(idx):
    if isinstance(idx, int):
        return pl.ds(idx * ROW_CHUNKS, ROW_CHUNKS)
    return pl.ds(pl.multiple_of(idx * ROW_CHUNKS, ROW_CHUNKS), ROW_CHUNKS)


class _RowGather:
    def __init__(self, idx_ref, src_hbm, buf, sem, tm):
        self.idx_ref, self.src, self.buf, self.sem, self.tm = idx_ref, src_hbm, buf, sem, tm

    def start(self, tile, slot, rows):
        for r in rows:
            pltpu.make_async_copy(self.src.at[_row_tile(self.idx_ref[tile * self.tm + r])],
                                  self.buf.at[slot, _row_tile(r)], self.sem.at[slot]).start()

    def wait(self, slot):
        pltpu.make_async_copy(self.src.at[pl.ds(0, self.tm * ROW_CHUNKS)], self.buf.at[slot],
                              self.sem.at[slot]).wait()


def _fill_xn(xf, xn_ref, scratch, dil):
    tm = xn_ref.shape[0]
    seg = tm // dil
    if dil == 1:
        xn_ref[...] = xf.astype(BF16)
        return
    xs_ref, = scratch
    for c in range(ROW_CHUNKS):
        xs_ref[c] = xf[:, c * V7X_LANES:(c + 1) * V7X_LANES]
    for r in range(dil):
        for c in range(ROW_CHUNKS):
            xn_ref[r * seg:(r + 1) * seg, c * V7X_LANES:(c + 1) * V7X_LANES] = (
                xs_ref[c, pl.ds(r, seg, stride=dil), :].astype(BF16))


def _proj_out(acc, o_ref, gelu_cols, dil):
    seg = acc.shape[0] // dil
    if gelu_cols:
        o_ref[0, :, :gelu_cols] = jax.nn.gelu(acc[:, :gelu_cols]).astype(o_ref.dtype)
        o_ref[0, :, gelu_cols:] = acc[:, gelu_cols:].astype(o_ref.dtype)
    else:
        for r in range(dil):
            o_ref[r] = acc[r * seg:(r + 1) * seg].astype(o_ref.dtype)


def _proj_kernel(h_ref, g_ref, w_ref, o_ref, xn_ref, *scratch, gelu_cols, dil):
    @pl.when(pl.program_id(1) == 0)
    def _():
        _fill_xn(_rms(h_ref[...], g_ref[...]), xn_ref, scratch, dil)

    _proj_out(_dot(xn_ref[...], w_ref[...].astype(BF16)), o_ref, gelu_cols, dil)


def _gather_proj_kernel(pos_ref, hs_hbm, g_ref, w_ref, o_ref, hnat_ref, xn_ref, hbuf, sem,
                        *, gelu_cols, ni, nj):
    i = pl.program_id(0)
    j = pl.program_id(1)
    tm = xn_ref.shape[0]
    per = tm // nj
    slot = i % 2
    gather = _RowGather(pos_ref, hs_hbm, hbuf, sem, tm)

    @pl.when((i == 0) & (j == 0))
    def _():
        gather.start(0, 0, range(tm))

    @pl.when(j == 0)
    def _():
        gather.wait(slot)
        h = _load_rows(hbuf, tm, (slot,))
        hnat_ref[...] = h
        _fill_xn(_rms(h, g_ref[...]), xn_ref, (), 1)

    nxt = jnp.where(i + 1 < ni, i + 1, 0)
    gather.start(nxt, 1 - slot, [j * per + r for r in range(per)])
    _proj_out(_dot(xn_ref[...], w_ref[...].astype(BF16)), o_ref, gelu_cols, 1)

    @pl.when((i == ni - 1) & (j == nj - 1))
    def _():
        gather.wait(1 - slot)


def _proj(h, g, w, *, rows, tm, tn, n, w_lead=0, w_col0=0, gelu_cols=0, dil=1, pos=None):
    seg = tm // dil
    ni, nj = rows // tm, n // tn
    out_z = jax.ShapeDtypeStruct((dil, rows // dil, n), BF16)
    if pos is None:
        return pl.pallas_call(
            functools.partial(_proj_kernel, gelu_cols=gelu_cols, dil=dil),
            grid=(ni, nj),
            in_specs=[
                pl.BlockSpec((tm, D_MODEL), lambda i, j: (i, 0)),
                pl.BlockSpec((1, D_MODEL), lambda i, j: (0, 0)),
                pl.BlockSpec((None, D_MODEL, tn), lambda i, j: (w_lead, 0, w_col0 + j)),
            ],
            out_specs=pl.BlockSpec((dil, seg, tn), lambda i, j: (0, i, j)),
            out_shape=out_z,
            scratch_shapes=[pltpu.VMEM((tm, D_MODEL), BF16)] + (
                [pltpu.VMEM((ROW_CHUNKS, tm, V7X_LANES), F32)] if dil > 1 else []),
            compiler_params=_cparams("parallel", "arbitrary"),
            name=f"norm_proj_d{dil}",
        )(h, g.reshape(1, D_MODEL), w)
    assert dil == 1 and tm % nj == 0
    grid_spec = pltpu.PrefetchScalarGridSpec(
        num_scalar_prefetch=1,
        grid=(ni, nj),
        in_specs=[
            pl.BlockSpec(memory_space=pl.ANY),
            pl.BlockSpec((1, D_MODEL), lambda i, j, p: (0, 0)),
            pl.BlockSpec((None, D_MODEL, tn), lambda i, j, p: (w_lead, 0, w_col0 + j)),
        ],
        out_specs=[pl.BlockSpec((1, tm, tn), lambda i, j, p: (0, i, j)),
                   pl.BlockSpec((tm, D_MODEL), lambda i, j, p: (i, 0))],
        scratch_shapes=[pltpu.VMEM((tm, D_MODEL), BF16),
                        pltpu.VMEM((2, tm * ROW_CHUNKS, V7X_LANES), F32),
                        pltpu.SemaphoreType.DMA((2,))],
    )
    return pl.pallas_call(
        functools.partial(_gather_proj_kernel, gelu_cols=gelu_cols, ni=ni, nj=nj),
        grid_spec=grid_spec,
        out_shape=[out_z, jax.ShapeDtypeStruct((rows, D_MODEL), F32)],
        compiler_params=_cparams("arbitrary", "arbitrary"),
        name="gather_norm_proj",
    )(pos, h, g.reshape(1, D_MODEL), w)


def _t5_bucket_np(rel):
    nb = REL_BUCKETS // 2
    max_exact = nb // 2
    ret = np.where(rel > 0, nb, 0)
    n = np.abs(rel)
    nf = np.maximum(n, 1).astype(np.float32)
    large = max_exact + (np.log(nf / np.float32(max_exact)) / np.float32(math.log(REL_MAX_DIST / max_exact))
                         * np.float32(nb - max_exact)).astype(np.int32)
    large = np.minimum(large, nb - 1)
    return (ret + np.where(n < max_exact, n, large)).astype(np.int32)


def _bias_kernel(table_ref, idx_ref, mask_ref, o_ref, *, block):
    idx = idx_ref[...]
    for h in range(REL_HEADS):
        acc = jnp.zeros(idx.shape, F32)
        for b in range(REL_BUCKETS):
            acc = jnp.where(idx == b, table_ref[b, h], acc)
        for v in range(3):
            o_ref[v, h * block:(h + 1) * block, :] = acc + mask_ref[v]


def _band_bias(table, block, half, dil):
    rel = np.arange(3 * block)[None, :] - block - np.arange(block)[:, None]
    band = np.abs(rel) <= half
    col = np.arange(3 * block)[None, :]
    masks = np.stack([band & (col >= block), band, band & (col < 2 * block)])
    add = np.where(masks, 0.0, NEG_INF).astype(np.float32)
    return pl.pallas_call(
        functools.partial(_bias_kernel, block=block),
        in_specs=[pl.BlockSpec(memory_space=pltpu.SMEM),
                  pl.BlockSpec(memory_space=pltpu.VMEM),
                  pl.BlockSpec(memory_space=pltpu.VMEM)],
        out_specs=pl.BlockSpec(memory_space=pltpu.VMEM),
        out_shape=jax.ShapeDtypeStruct((3, REL_HEADS * block, 3 * block), F32),
        name=f"rel_bias_d{dil}",
    )(table, jnp.asarray(_t5_bucket_np(rel * dil)), jnp.asarray(add))


def _even_mix_kernel(z_ref, kvp_ref, kvn_ref, h_ref, lng_ref, lnb_ref, ws_ref, bs_ref, bias_ref,
                     sink_ref, wout_ref, o_ref, kv_scr, y_scr):
    i = pl.program_id(0)
    nsub = EVEN_TB // B_BLOCK
    nblk = SEQ // B_BLOCK
    kv_scr[0:B_BLOCK] = kvp_ref[...]
    kv_scr[B_BLOCK:B_BLOCK + EVEN_TB] = z_ref[0, :, EVEN_K0:EVEN_IN]
    kv_scr[B_BLOCK + EVEN_TB:] = kvn_ref[...]
    lng = lng_ref[...]
    lnb = lnb_ref[...]
    for s in range(nsub):
        r0 = s * B_BLOCK
        gb = i * nsub + s
        sel = jnp.where(gb == 0, 0, jnp.where(gb == nblk - 1, 2, 1))
        u = z_ref[0, r0:r0 + A_CHUNK, 0:A_WIDTH].astype(F32)
        va = z_ref[0, r0:r0 + A_CHUNK, A_WIDTH:2 * A_WIDTH].astype(F32)
        mu = jnp.mean(va, axis=-1, keepdims=True)
        vc = va - mu
        var = jnp.mean(vc * vc, axis=-1, keepdims=True)
        vn = (vc * lax.rsqrt(var + EPS) * lng + lnb).astype(BF16)
        for g in range(A_GROUPS):
            c0 = g * A_CH
            mixed = _dot(ws_ref[g], vn[:, c0:c0 + A_CH]) + bs_ref[g]
            y_scr[r0:r0 + A_CHUNK, c0:c0 + A_CH] = (u[:, c0:c0 + A_CH] * mixed).astype(BF16)
        units = [(kh, kh * B_Q_PER_KV + half * EVEN_STACK)
                 for kh in range(B_KV_HEADS) for half in range(B_Q_PER_KV // EVEN_STACK)]
        lgs = []
        for kh, hd0 in units:
            kw = kv_scr[r0:r0 + 3 * B_BLOCK, kh * B_HEAD_DIM:(kh + 1) * B_HEAD_DIM]
            q = jnp.concatenate(
                [z_ref[0, r0:r0 + B_BLOCK,
                       EVEN_Q0 + (hd0 + g) * B_HEAD_DIM:EVEN_Q0 + (hd0 + g + 1) * B_HEAD_DIM]
                 for g in range(EVEN_STACK)], axis=0)
            lgs.append(_dot_nt(q, kw))
        lgs = [lg * (B_HEAD_DIM ** -0.5) + bias_ref[sel, hd0 * B_BLOCK:(hd0 + EVEN_STACK) * B_BLOCK, :]
               for lg, (_, hd0) in zip(lgs, units)]
        sks = [sink_ref[hd0 * B_BLOCK:(hd0 + EVEN_STACK) * B_BLOCK, :] for _, hd0 in units]
        ms = [jnp.maximum(jnp.max(lg, axis=-1, keepdims=True), sk) for lg, sk in zip(lgs, sks)]
        ps = [jnp.exp(lg - m) for lg, m in zip(lgs, ms)]
        dens = [jnp.sum(p, axis=-1, keepdims=True) + jnp.exp(sk - m) for p, sk, m in zip(ps, sks, ms)]
        os_ = []
        for p, (kh, _) in zip(ps, units):
            vw = kv_scr[r0:r0 + 3 * B_BLOCK,
                        B_KV_WIDTH + kh * B_HEAD_DIM:B_KV_WIDTH + (kh + 1) * B_HEAD_DIM]
            os_.append(_dot(p.astype(BF16), vw))
        for o, den, (_, hd0) in zip(os_, dens, units):
            o = o * (1.0 / den)
            for g in range(EVEN_STACK):
                c0 = A_WIDTH + (hd0 + g) * B_HEAD_DIM
                y_scr[r0:r0 + B_BLOCK, c0:c0 + B_HEAD_DIM] = o[g * B_BLOCK:(g + 1) * B_BLOCK].astype(BF16)
    o_ref[...] = h_ref[...] + _dot(y_scr[...], wout_ref[...])


def _even_mix(z, h, ln_g, ln_b, w_s, b_s, bias, sink, w_out):
    nsub = EVEN_TB // B_BLOCK
    nblk = SEQ // B_BLOCK
    kv_cb = EVEN_K0 // (2 * B_KV_WIDTH)
    sink_col = jnp.broadcast_to(sink.reshape(B_HEADS, 1, 1), (B_HEADS, B_BLOCK, 1)).reshape(
        B_HEADS * B_BLOCK, 1)
    return pl.pallas_call(
        _even_mix_kernel,
        grid=(SEQ // EVEN_TB,),
        in_specs=[
            pl.BlockSpec((1, EVEN_TB, EVEN_IN), lambda i: (0, i, 0)),
            pl.BlockSpec((None, B_BLOCK, 2 * B_KV_WIDTH),
                         lambda i: (0, jnp.maximum(i * nsub - 1, 0), kv_cb)),
            pl.BlockSpec((None, B_BLOCK, 2 * B_KV_WIDTH),
                         lambda i: (0, jnp.minimum((i + 1) * nsub, nblk - 1), kv_cb)),
            pl.BlockSpec((EVEN_TB, D_MODEL), lambda i: (i, 0)),
            pl.BlockSpec((1, A_WIDTH), lambda i: (0, 0)),
            pl.BlockSpec((1, A_WIDTH), lambda i: (0, 0)),
            pl.BlockSpec((A_GROUPS, A_CHUNK, A_CHUNK), lambda i: (0, 0, 0)),
            pl.BlockSpec((A_GROUPS, A_CHUNK, A_CH), lambda i: (0, 0, 0)),
            pl.BlockSpec((3, B_HEADS * B_BLOCK, 3 * B_BLOCK), lambda i: (0, 0, 0)),
            pl.BlockSpec((B_HEADS * B_BLOCK, 1), lambda i: (0, 0)),
            pl.BlockSpec((A_WIDTH + B_WIDTH, D_MODEL), lambda i: (0, 0)),
        ],
        out_specs=pl.BlockSpec((EVEN_TB, D_MODEL), lambda i: (i, 0)),
        out_shape=jax.ShapeDtypeStruct((SEQ, D_MODEL), F32),
        scratch_shapes=[
            pltpu.VMEM((EVEN_TB + 2 * B_BLOCK, 2 * B_KV_WIDTH), BF16),
            pltpu.VMEM((EVEN_TB, A_WIDTH + B_WIDTH), BF16),
        ],
        compiler_params=_cparams("parallel"),
        name="even_mixer",
    )(z, z, z, h, ln_g.reshape(1, A_WIDTH), ln_b.reshape(1, A_WIDTH), w_s.astype(BF16),
      jnp.broadcast_to(b_s[:, :, None], (A_GROUPS, A_CHUNK, A_CH)), bias, sink_col, w_out)


def _dil_attn_kernel(zc_ref, kp_ref, kn_ref, vp_ref, vn_ref, bias_ref, o_ref, lse_ref,
                     k_scr, v_scr, o_scr, lse_scr, *, dil):
    t = pl.program_id(0)
    seg = ODD_TILE // dil
    nsb = seg // C_BLOCK
    nblk = SEQ // dil // C_BLOCK
    k_scr[:, 0:C_BLOCK] = kp_ref[...]
    k_scr[:, C_BLOCK:C_BLOCK + seg] = zc_ref[:, :, C_WIDTH:2 * C_WIDTH]
    k_scr[:, C_BLOCK + seg:] = kn_ref[...]
    v_scr[:, 0:C_BLOCK] = vp_ref[...]
    v_scr[:, C_BLOCK:C_BLOCK + seg] = zc_ref[:, :, 2 * C_WIDTH:3 * C_WIDTH]
    v_scr[:, C_BLOCK + seg:] = vn_ref[...]
    lane = lax.broadcasted_iota(jnp.int32, (C_BLOCK, V7X_LANES), 1)

    def body(it, carry):
        blocks = []
        for u in range(ODD_INTERLEAVE):
            n = it * ODD_INTERLEAVE + u
            r = n // nsb
            s = n % nsb
            r0 = pl.multiple_of(s * C_BLOCK, C_BLOCK)
            gb = t * nsb + s
            sel = jnp.where(gb == 0, 0, jnp.where(gb == nblk - 1, 2, 1))
            rows = (pl.ds(s * (C_BLOCK * dil) + r, C_BLOCK, stride=dil) if dil > 1
                    else pl.ds(r0, C_BLOCK))
            blocks.append((r, r0, sel, rows))
        lgs = []
        for r, r0, _, _ in blocks:
            for hd in range(C_HEADS):
                c0 = hd * C_HEAD_DIM
                q = zc_ref[r, pl.ds(r0, C_BLOCK), c0:c0 + C_HEAD_DIM]
                kw = k_scr[r, pl.ds(r0, 3 * C_BLOCK), c0:c0 + C_HEAD_DIM]
                lgs.append(_dot_nt(q, kw))
        lg = [jnp.concatenate(lgs[u * C_HEADS:(u + 1) * C_HEADS], axis=0) * (C_HEAD_DIM ** -0.5)
              + bias_ref[blk[2]] for u, blk in enumerate(blocks)]
        m = [jnp.max(x, axis=-1, keepdims=True) for x in lg]
        p = [jnp.exp(x - mm) for x, mm in zip(lg, m)]
        den = [jnp.sum(x, axis=-1, keepdims=True) for x in p]
        inv = [1.0 / d for d in den]
        lse = [mm + jnp.log(d) for mm, d in zip(m, den)]
        pb = [x.astype(BF16) for x in p]
        outs = []
        for u, (r, r0, _, _) in enumerate(blocks):
            for hd in range(C_HEADS):
                c0 = hd * C_HEAD_DIM
                vw = v_scr[r, pl.ds(r0, 3 * C_BLOCK), c0:c0 + C_HEAD_DIM]
                outs.append(_dot(pb[u][hd * C_BLOCK:(hd + 1) * C_BLOCK], vw))
        for u, (_, _, _, rows) in enumerate(blocks):
            lse_tile = jnp.zeros((C_BLOCK, V7X_LANES), F32)
            for hd in range(C_HEADS):
                o_scr[hd, rows, :] = outs[u * C_HEADS + hd] * inv[u][hd * C_BLOCK:(hd + 1) * C_BLOCK]
                lse_tile = jnp.where(lane == hd, lse[u][hd * C_BLOCK:(hd + 1) * C_BLOCK], lse_tile)
            lse_scr[rows, :] = lse_tile
        return carry

    lax.fori_loop(0, ODD_BLOCKS // ODD_INTERLEAVE, body, 0)
    for hd in range(C_HEADS):
        o_ref[:, hd * C_HEAD_DIM:(hd + 1) * C_HEAD_DIM] = o_scr[hd].astype(o_ref.dtype)
    lse_ref[...] = lse_scr[...]


def _dil_attn(zg, bias, dil):
    seg = ODD_TILE // dil
    nsb = seg // C_BLOCK
    last = SEQ // dil // C_BLOCK - 1

    def halo(j, nxt):
        if nxt:
            return pl.BlockSpec((dil, C_BLOCK, C_WIDTH),
                                lambda t: (0, jnp.minimum((t + 1) * nsb, last), j))
        return pl.BlockSpec((dil, C_BLOCK, C_WIDTH), lambda t: (0, jnp.maximum(t * nsb - 1, 0), j))

    return pl.pallas_call(
        functools.partial(_dil_attn_kernel, dil=dil),
        grid=(SEQ // ODD_TILE,),
        in_specs=[pl.BlockSpec((dil, seg, 3 * C_WIDTH), lambda t: (0, t, 0)),
                  halo(1, False), halo(1, True), halo(2, False), halo(2, True),
                  pl.BlockSpec((3, C_HEADS * C_BLOCK, 3 * C_BLOCK), lambda t: (0, 0, 0))],
        out_specs=[pl.BlockSpec((ODD_TILE, C_WIDTH), lambda t: (t, 0)),
                   pl.BlockSpec((ODD_TILE, V7X_LANES), lambda t: (t, 0))],
        out_shape=[jax.ShapeDtypeStruct((SEQ, C_WIDTH), BF16),
                   jax.ShapeDtypeStruct((SEQ, V7X_LANES), F32)],
        scratch_shapes=[pltpu.VMEM((dil, seg + 2 * C_BLOCK, C_WIDTH), BF16),
                        pltpu.VMEM((dil, seg + 2 * C_BLOCK, C_WIDTH), BF16),
                        pltpu.VMEM((C_HEADS, ODD_TILE, C_HEAD_DIM), F32),
                        pltpu.VMEM((ODD_TILE, V7X_LANES), F32)],
        compiler_params=_cparams("parallel"),
        name=f"dilated_attn_d{dil}",
    )(zg, zg, zg, zg, zg, bias)


def _combine_kernel(o0_ref, o1_ref, o2_ref, l0_ref, l1_ref, l2_ref, h_ref, wout_ref, out_ref, y_scr):
    l0 = l0_ref[...]
    l1 = l1_ref[...]
    l2 = l2_ref[...]
    m = jnp.maximum(jnp.maximum(l0, l1), l2)
    e0 = jnp.exp(l0 - m)
    e1 = jnp.exp(l1 - m)
    e2 = jnp.exp(l2 - m)
    tot = e0 + e1 + e2
    w0 = e0 / tot
    w1 = e1 / tot
    w2 = e2 / tot
    for hd in range(C_HEADS):
        c0 = hd * C_HEAD_DIM
        y = (w0[:, hd:hd + 1] * o0_ref[:, c0:c0 + C_HEAD_DIM].astype(F32)
             + w1[:, hd:hd + 1] * o1_ref[:, c0:c0 + C_HEAD_DIM].astype(F32)
             + w2[:, hd:hd + 1] * o2_ref[:, c0:c0 + C_HEAD_DIM].astype(F32))
        y_scr[:, c0:c0 + C_HEAD_DIM] = y.astype(BF16)
    out_ref[...] = h_ref[...] + _dot(y_scr[...], wout_ref[...])


def _combine(outs, lses, h, w_out):
    blk_o = pl.BlockSpec((COMB_TB, C_WIDTH), lambda i: (i, 0))
    blk_l = pl.BlockSpec((COMB_TB, V7X_LANES), lambda i: (i, 0))
    return pl.pallas_call(
        _combine_kernel,
        grid=(SEQ // COMB_TB,),
        in_specs=[blk_o, blk_o, blk_o, blk_l, blk_l, blk_l,
                  pl.BlockSpec((COMB_TB, D_MODEL), lambda i: (i, 0)),
                  pl.BlockSpec((C_WIDTH, D_MODEL), lambda i: (0, 0))],
        out_specs=pl.BlockSpec((COMB_TB, D_MODEL), lambda i: (i, 0)),
        out_shape=jax.ShapeDtypeStruct((SEQ, D_MODEL), F32),
        scratch_shapes=[pltpu.VMEM((COMB_TB, C_WIDTH), BF16)],
        compiler_params=_cparams("parallel"),
        name="group_combine_proj",
    )(*outs, *lses, h, w_out)


def _cross_kernel(h_ref, gx_ref, wq_ref, kv_ref, wo_ref, gf_ref, wr_ref, br_ref,
                  hx_ref, meta_ref, o_scr):
    h = h_ref[...]
    q = _dot(_rms(h, gx_ref[...]).astype(BF16), wq_ref[...]).astype(BF16)
    for hd in range(X_HEADS):
        c0 = hd * X_HEAD_DIM
        lg = _dot_nt(q[:, c0:c0 + X_HEAD_DIM], kv_ref[0, :, c0:c0 + X_HEAD_DIM]) * (X_HEAD_DIM ** -0.5)
        m = jnp.max(lg, axis=-1, keepdims=True)
        p = jnp.exp(lg - m)
        den = jnp.sum(p, axis=-1, keepdims=True)
        o = _dot(p.astype(BF16), kv_ref[0, :, X_WIDTH + c0:X_WIDTH + c0 + X_HEAD_DIM]) / den
        o_scr[:, c0:c0 + X_HEAD_DIM] = o.astype(BF16)
    h2 = h + _dot(o_scr[...], wo_ref[...])
    _store_rows(hx_ref, h2)

    t = _rms(h2, gf_ref[...])
    t_hi = t.astype(BF16)
    t_lo = (t - t_hi.astype(F32)).astype(BF16)
    lt = (_dot_nt(wr_ref[0], t_hi) + _dot_nt(wr_ref[0], t_lo) + _dot_nt(wr_ref[1], t_hi)) + br_ref[...]
    g = [lt[k:k + 1, :] for k in range(MOE_GROUPS)]
    gmax = jnp.maximum(jnp.maximum(g[0], g[1]), jnp.maximum(g[2], g[3]))
    grp = jnp.where(g[0] == gmax, 0, jnp.where(g[1] == gmax, 1, jnp.where(g[2] == gmax, 2, 3)))
    g_gate = 1.0 / (jnp.exp(g[0] - gmax) + jnp.exp(g[1] - gmax) + jnp.exp(g[2] - gmax)
                    + jnp.exp(g[3] - gmax))
    e = []
    for k in range(MOE_EPG):
        rows = [lt[MOE_GROUPS + gi * MOE_EPG + k:MOE_GROUPS + gi * MOE_EPG + k + 1, :]
                for gi in range(MOE_GROUPS)]
        e.append(jnp.where(grp == 0, rows[0], jnp.where(grp == 1, rows[1],
                                                         jnp.where(grp == 2, rows[2], rows[3]))))
    v1 = jnp.maximum(jnp.maximum(e[0], e[1]), jnp.maximum(e[2], e[3]))
    i1 = jnp.where(e[0] == v1, 0, jnp.where(e[1] == v1, 1, jnp.where(e[2] == v1, 2, 3)))
    r = [jnp.where(i1 == k, -jnp.inf, e[k]) for k in range(MOE_EPG)]
    v2 = jnp.maximum(jnp.maximum(r[0], r[1]), jnp.maximum(r[2], r[3]))
    i2 = jnp.where(r[0] == v2, 0, jnp.where(r[1] == v2, 1, jnp.where(r[2] == v2, 2, 3)))
    d = jnp.exp(v2 - v1)
    w1 = g_gate / (1.0 + d)
    w2 = g_gate * d / (1.0 + d)
    first_lo = i1 < i2
    lo = jnp.where(first_lo, i1, i2)
    hi = jnp.where(first_lo, i2, i1)
    w_lo = jnp.where(first_lo, w1, w2)
    w_hi = jnp.where(first_lo, w2, w1)
    pair = jnp.where(lo == 0, hi - 1, jnp.where(lo == 1, jnp.where(hi == 3, 3, 4), 5))
    w_a = jnp.where(lo == 2, w_hi, w_lo)
    w_b = jnp.where(lo == 2, w_lo, w_hi)
    bucket = (grp * N_PAIRS + pair).astype(F32)
    row = lax.broadcasted_iota(jnp.int32, (8, CROSS_TB), 0)
    meta_ref[...] = jnp.where(row == 0, bucket, jnp.where(row == 1, w_a, jnp.where(row == 2, w_b, 0.0)))


def _cross_router(h, g_cross, wq, kv, wo, g_ffn, wr_t, br):
    full = lambda shape: pl.BlockSpec(shape, lambda i: tuple(0 for _ in shape))
    return pl.pallas_call(
        _cross_kernel,
        grid=(SEQ // CROSS_TB,),
        in_specs=[
            pl.BlockSpec((CROSS_TB, D_MODEL), lambda i: (i, 0)),
            full((1, D_MODEL)),
            full((D_MODEL, X_WIDTH)),
            full((1, MEM_LEN, 2 * X_WIDTH)),
            full((X_WIDTH, D_MODEL)),
            full((1, D_MODEL)),
            full((2, ROUTER_ROWS, D_MODEL)),
            full((ROUTER_ROWS, 1)),
        ],
        out_specs=[pl.BlockSpec((CROSS_TB * ROW_CHUNKS, V7X_LANES), lambda i: (i, 0)),
                   pl.BlockSpec((8, CROSS_TB), lambda i: (0, i))],
        out_shape=[jax.ShapeDtypeStruct((SEQ * ROW_CHUNKS, V7X_LANES), F32),
                   jax.ShapeDtypeStruct((8, SEQ), F32)],
        scratch_shapes=[pltpu.VMEM((CROSS_TB, X_WIDTH), BF16)],
        compiler_params=_cparams("parallel"),
        name="cross_attn_router",
    )(h, g_cross.reshape(1, D_MODEL), wq, kv, wo, g_ffn.reshape(1, D_MODEL), wr_t, br)


def _moe_kernel(src_ref, ea_ref, eb_ref, nused_ref,
                hx_hbm, gates_ref, gf_ref, wga_ref, wua_ref, wda_ref, wgb_ref, wub_ref, wdb_ref,
                out_ref, xbuf, wup_a, wdn_a, wup_b, wdn_b, gsem):
    k = pl.program_id(0)
    nused = nused_ref[0]
    xslot = k % 3
    gather = _RowGather(src_ref, hx_hbm, xbuf, gsem, MOE_TM)

    @pl.when(k == 0)
    def _():
        gather.start(0, 0, range(MOE_TM))
        gather.start(1, 1, range(MOE_TM))

    prev = jnp.maximum(k - 1, 0)

    @pl.when((k < nused) & ((k == 0) | (ea_ref[k] != ea_ref[prev])))
    def _():
        wup_a[:D_EXPERT, :] = wga_ref[0].T.astype(BF16)
        wup_a[D_EXPERT:, :] = wua_ref[0].T.astype(BF16)
        wdn_a[...] = wda_ref[0].T.astype(BF16)

    @pl.when((k < nused) & ((k == 0) | (eb_ref[k] != eb_ref[prev])))
    def _():
        wup_b[:D_EXPERT, :] = wgb_ref[0].T.astype(BF16)
        wup_b[D_EXPERT:, :] = wub_ref[0].T.astype(BF16)
        wdn_b[...] = wdb_ref[0].T.astype(BF16)

    @pl.when(k < nused)
    def _():
        gather.wait(xslot)
        h2 = _load_rows(xbuf, MOE_TM, (xslot,))
        gather.start(k + 2, (k + 2) % 3, range(MOE_TM))
        t = _rms(h2, gf_ref[...]).astype(BF16)
        y_t = jnp.zeros((D_MODEL, MOE_TM), F32)
        for row, wup, wdn in ((0, wup_a, wdn_a), (1, wup_b, wdn_b)):
            gate = gates_ref[row:row + 1, :]
            gu_t = _dot_nt(wup[...], t)
            hid_t = jax.nn.silu(gu_t[:D_EXPERT]) * gu_t[D_EXPERT:] * gate
            y_t = y_t + _dot(wdn[...], hid_t.astype(BF16))
        _store_rows(out_ref, h2 + y_t.T)

    @pl.when(k >= nused)
    def _():
        out_ref[...] = jnp.zeros(out_ref.shape, F32)

    @pl.when(k == nused - 1)
    def _():
        gather.wait((k + 1) % 3)
        gather.wait((k + 2) % 3)


def _moe(hx, gates, g_ffn, w_gate, w_up, w_down, src, ea, eb, nused):
    def wspec(shape, which):
        if which == 0:
            return pl.BlockSpec((1,) + shape, lambda k, s, a, b, n: (a[k], 0, 0))
        return pl.BlockSpec((1,) + shape, lambda k, s, a, b, n: (b[k], 0, 0))

    up_shape = (D_MODEL, D_EXPERT)
    down_shape = (D_EXPERT, D_MODEL)
    grid_spec = pltpu.PrefetchScalarGridSpec(
        num_scalar_prefetch=4,
        grid=(MOE_TILES,),
        in_specs=[
            pl.BlockSpec(memory_space=pl.ANY),
            pl.BlockSpec((None, 8, MOE_TM), lambda k, s, a, b, n: (k, 0, 0)),
            pl.BlockSpec((1, D_MODEL), lambda k, s, a, b, n: (0, 0)),
            wspec(up_shape, 0), wspec(up_shape, 0), wspec(down_shape, 0),
            wspec(up_shape, 1), wspec(up_shape, 1), wspec(down_shape, 1),
        ],
        out_specs=pl.BlockSpec((MOE_TM * ROW_CHUNKS, V7X_LANES), lambda k, s, a, b, n: (k, 0)),
        scratch_shapes=[
            pltpu.VMEM((3, MOE_TM * ROW_CHUNKS, V7X_LANES), F32),
            pltpu.VMEM((2 * D_EXPERT, D_MODEL), BF16),
            pltpu.VMEM((D_MODEL, D_EXPERT), BF16),
            pltpu.VMEM((2 * D_EXPERT, D_MODEL), BF16),
            pltpu.VMEM((D_MODEL, D_EXPERT), BF16),
            pltpu.SemaphoreType.DMA((3,)),
        ],
    )
    return pl.pallas_call(
        _moe_kernel,
        grid_spec=grid_spec,
        out_shape=jax.ShapeDtypeStruct((MOE_TILES * MOE_TM * ROW_CHUNKS, V7X_LANES), F32),
        compiler_params=_cparams("arbitrary"),
        name="routed_moe",
    )(src, ea, eb, nused, hx, gates, g_ffn.reshape(1, D_MODEL),
      w_gate, w_up, w_down, w_gate, w_up, w_down)


def _route_tables(meta):
    bucket = meta[0].astype(jnp.int32)
    ids = jnp.arange(N_BUCKETS, dtype=jnp.int32)
    counts = jnp.sum((bucket[:, None] == ids[None, :]).astype(jnp.int32), axis=0)
    ntile = (counts + MOE_TM - 1) // MOE_TM
    pad = ntile * MOE_TM - counts
    tile_end = jnp.cumsum(ntile)
    nused = tile_end[-1]
    dummy_key = jnp.where(jnp.arange(MOE_TM - 1, dtype=jnp.int32)[None, :] < pad[:, None],
                          ids[:, None], N_BUCKETS)
    keys = jnp.concatenate([bucket, dummy_key.reshape(-1)])
    vals = jnp.concatenate([jnp.arange(SEQ, dtype=jnp.int32),
                            jnp.full((N_BUCKETS * (MOE_TM - 1),), SEQ, jnp.int32)])
    nslot = MOE_TILES * MOE_TM
    zpad = jnp.zeros((N_BUCKETS * (MOE_TM - 1),), F32)
    _, tok, ga, gb = lax.sort((keys, vals, jnp.concatenate([meta[1], zpad]), jnp.concatenate([meta[2], zpad])),
                              num_keys=1, is_stable=True)
    tok = tok[:nslot]
    gates = jnp.pad(jnp.stack([ga[:nslot].reshape(MOE_TILES, MOE_TM), gb[:nslot].reshape(MOE_TILES, MOE_TM)],
                              axis=1), ((0, 0), (0, 6), (0, 0)))
    valid = tok < SEQ
    src = jnp.concatenate([jnp.where(valid, tok, 0), jnp.zeros((2 * MOE_TM,), jnp.int32)])
    _, pos = lax.sort((tok, jnp.arange(nslot, dtype=jnp.int32)), num_keys=1, is_stable=True)
    pos = pos[:SEQ]
    tiles = jnp.arange(MOE_TILES, dtype=jnp.int32)
    tile_bucket = jnp.minimum(jnp.sum((tiles[:, None] >= tile_end[None, :]).astype(jnp.int32), axis=1),
                              N_BUCKETS - 1)
    onehot = (tile_bucket[:, None] == ids[None, :]).astype(jnp.int32)
    base = (np.arange(N_BUCKETS) // N_PAIRS) * MOE_EPG
    ea = jnp.sum(onehot * jnp.asarray(base + np.asarray(SLOT_A)[np.arange(N_BUCKETS) % N_PAIRS],
                                      jnp.int32)[None, :], axis=1)
    eb = jnp.sum(onehot * jnp.asarray(base + np.asarray(SLOT_B)[np.arange(N_BUCKETS) % N_PAIRS],
                                      jnp.int32)[None, :], axis=1)
    return (src.astype(jnp.int32), pos.astype(jnp.int32), ea.astype(jnp.int32), eb.astype(jnp.int32),
            nused.reshape(1).astype(jnp.int32), gates)


def _final_norm_kernel(pos_ref, hs_hbm, g_ref, o_ref, hbuf, sem, *, ni):
    i = pl.program_id(0)
    tm = o_ref.shape[0]
    slot = i % 2
    gather = _RowGather(pos_ref, hs_hbm, hbuf, sem, tm)

    @pl.when(i == 0)
    def _():
        gather.start(0, 0, range(tm))

    @pl.when(i + 1 < ni)
    def _():
        gather.start(i + 1, 1 - slot, range(tm))

    gather.wait(slot)
    o_ref[...] = _rms(_load_rows(hbuf, tm, (slot,)), g_ref[...])


def _final_norm(hs, pos, g):
    tb = FINAL_TB
    ni = SEQ // tb
    grid_spec = pltpu.PrefetchScalarGridSpec(
        num_scalar_prefetch=1,
        grid=(ni,),
        in_specs=[pl.BlockSpec(memory_space=pl.ANY),
                  pl.BlockSpec((1, D_MODEL), lambda i, p: (0, 0))],
        out_specs=pl.BlockSpec((tb, D_MODEL), lambda i, p: (i, 0)),
        scratch_shapes=[pltpu.VMEM((2, tb * ROW_CHUNKS, V7X_LANES), F32),
                        pltpu.SemaphoreType.DMA((2,))],
    )
    return pl.pallas_call(
        functools.partial(_final_norm_kernel, ni=ni),
        grid_spec=grid_spec,
        out_shape=jax.ShapeDtypeStruct((SEQ, D_MODEL), F32),
        compiler_params=_cparams("arbitrary"),
        name="final_norm",
    )(pos, hs, g.reshape(1, D_MODEL))


def kernel(x, mem, ln_mix, ln_cross, ln_mem, ln_ffn, ln_final, rel_table, even_w_in, even_w_out,
           sgu_ln_g, sgu_ln_b, sgu_w, sgu_b, attn_sink, odd_w_in, odd_w_out, xq_w, xkv_w, xo_w,
           router_group_w, router_group_b, router_expert_w, router_expert_b,
           expert_w_gate, expert_w_up, expert_w_down):
    h = x.reshape(SEQ, D_MODEL)
    mem2 = mem.reshape(MEM_LEN, D_MODEL)
    bias_even = _band_bias(rel_table, B_BLOCK, B_HALF_WINDOW, 1)
    bias_odd = [_band_bias(rel_table, C_BLOCK, window // 2 // dil, dil) for window, dil in C_PAIRS]

    pos = None
    for layer in range(DEPTH):
        i = layer // 2
        if layer % 2 == 0:
            z = _proj(h, ln_mix[layer], even_w_in, w_lead=i, rows=SEQ, tm=EVEN_TB, tn=EVEN_IN, n=EVEN_IN,
                      gelu_cols=2 * A_WIDTH, pos=pos)
            if pos is not None:
                z, h = z
            h = _even_mix(z, h, sgu_ln_g[i], sgu_ln_b[i], sgu_w[i], sgu_b[i], bias_even,
                          attn_sink[i], even_w_out[i].astype(BF16))
        else:
            zg, h = _proj(h, ln_mix[layer], odd_w_in, w_lead=i, rows=SEQ, tm=PROJ_TM, tn=GATHER_TN,
                          n=3 * C_WIDTH, pos=pos)
            outs, lses = [], []
            for gi, (_, dil) in enumerate(C_PAIRS):
                if gi > 0:
                    zg = _proj(h, ln_mix[layer], odd_w_in, w_lead=i, w_col0=gi * 3 * C_WIDTH // PROJ_TN,
                               rows=SEQ, tm=PROJ_TM, tn=PROJ_TN, n=3 * C_WIDTH, dil=dil)
                o, lse = _dil_attn(zg, bias_odd[gi], dil)
                outs.append(o)
                lses.append(lse)
            h = _combine(outs, lses, h, odd_w_out[i].astype(BF16))

        kv = _proj(mem2, ln_mem[layer], xkv_w, w_lead=layer, rows=MEM_LEN, tm=MEM_LEN,
                   tn=2 * X_WIDTH, n=2 * X_WIDTH)
        wr_t = jnp.zeros((ROUTER_ROWS, D_MODEL), F32)
        wr_t = wr_t.at[:MOE_GROUPS].set(router_group_w[layer].T)
        wr_t = wr_t.at[MOE_GROUPS:MOE_GROUPS + N_EXPERTS].set(
            router_expert_w[layer].reshape(D_MODEL, N_EXPERTS).T)
        br = jnp.zeros((ROUTER_ROWS, 1), F32)
        br = br.at[:MOE_GROUPS, 0].set(router_group_b[layer])
        br = br.at[MOE_GROUPS:MOE_GROUPS + N_EXPERTS, 0].set(router_expert_b[layer].reshape(N_EXPERTS))
        wr_hi = wr_t.astype(BF16)
        wr_split = jnp.stack([wr_hi, (wr_t - wr_hi.astype(F32)).astype(BF16)])
        hx, meta = _cross_router(h, ln_cross[layer], xq_w[layer].astype(BF16), kv,
                                 xo_w[layer].astype(BF16), ln_ffn[layer], wr_split, br)

        src, pos, ea, eb, nused, gates = _route_tables(meta)
        h = _moe(hx, gates, ln_ffn[layer],
                 expert_w_gate.reshape(DEPTH * N_EXPERTS, D_MODEL, D_EXPERT),
                 expert_w_up.reshape(DEPTH * N_EXPERTS, D_MODEL, D_EXPERT),
                 expert_w_down.reshape(DEPTH * N_EXPERTS, D_EXPERT, D_MODEL),
                 src, ea + layer * N_EXPERTS, eb + layer * N_EXPERTS, nused)

    return _final_norm(h, pos, ln_final).reshape(1, SEQ, D_MODEL)
```

```python
import functools
import math

import numpy as np
import jax
import jax.numpy as jnp
from jax import lax
from jax.experimental import pallas as pl
from jax.experimental.pallas import tpu as pltpu

F32 = jnp.float32
BF16 = jnp.bfloat16

D_MODEL = 1024
SEQ = 16384
DEPTH = 4
MEM_LEN = 256
EPS = 1e-6
NEG_INF = -1e30

A_GROUPS = 4
A_CH = 128
A_WIDTH = A_GROUPS * A_CH
A_CHUNK = 128
B_HEADS = 8
B_KV_HEADS = 2
B_Q_PER_KV = B_HEADS // B_KV_HEADS
B_HEAD_DIM = 64
B_WIDTH = B_HEADS * B_HEAD_DIM
B_KV_WIDTH = B_KV_HEADS * B_HEAD_DIM
B_HALF_WINDOW = 128
B_BLOCK = 128
EVEN_IN = 2 * A_WIDTH + B_WIDTH + 2 * B_KV_WIDTH
EVEN_Q0 = 2 * A_WIDTH
EVEN_K0 = EVEN_Q0 + B_WIDTH
EVEN_V0 = EVEN_K0 + B_KV_WIDTH

C_PAIRS = ((128, 1), (512, 4), (2048, 16))
C_GROUPS = len(C_PAIRS)
C_HEADS = 8
C_HEAD_DIM = 128
C_WIDTH = C_HEADS * C_HEAD_DIM
C_BLOCK = 64
ODD_IN = C_GROUPS * 3 * C_WIDTH

REL_BUCKETS = 32
REL_MAX_DIST = 1024
REL_HEADS = 8

X_HEADS = 4
X_HEAD_DIM = 128
X_WIDTH = X_HEADS * X_HEAD_DIM

MOE_GROUPS = 4
MOE_EPG = 4
N_EXPERTS = MOE_GROUPS * MOE_EPG
D_EXPERT = 512
SLOT_A = (0, 0, 0, 1, 1, 3)
SLOT_B = (1, 2, 3, 3, 2, 2)
N_PAIRS = len(SLOT_A)
N_BUCKETS = MOE_GROUPS * N_PAIRS

V7X_LANES = 128
ROW_CHUNKS = D_MODEL // V7X_LANES
V7X_VMEM_BYTES = 64 * 1024 * 1024
VMEM_LIMIT = 56 * 1024 * 1024

PROJ_TM = 1024
PROJ_TN = 1024
EVEN_TB = 512
EVEN_STACK = 2
EVEN_SUBS = 4
ODD_TILE = PROJ_TM
ODD_BLOCKS = ODD_TILE // C_BLOCK
ODD_INTERLEAVE = 4
COMB_TB = 512
CROSS_TB = 512
MOE_TM = 256
ROUTER_ROWS = 32
MOE_TILES = (SEQ + N_BUCKETS * (MOE_TM - 1)) // MOE_TM
GATHER_TN = 1536
FINAL_TB = 512


def _cparams(*sem):
    return pltpu.CompilerParams(dimension_semantics=sem, vmem_limit_bytes=VMEM_LIMIT)


def _rms(x, g):
    return x * lax.rsqrt(jnp.mean(x * x, axis=-1, keepdims=True) + EPS) * g


def _dot(a, b):
    return jnp.dot(a, b, preferred_element_type=F32)


def _dot_nt(a, b):
    return lax.dot_general(a, b, (((1,), (1,)), ((), ())), preferred_element_type=F32)


def _load_rows(ref, n, lead=()):
    return jnp.concatenate([ref[lead + (pl.ds(c, n, stride=ROW_CHUNKS), slice(None))]
                            for c in range(ROW_CHUNKS)], axis=1)


def _store_rows(ref, val, lead=()):
    n = val.shape[0]
    for c in range(ROW_CHUNKS):
        ref[lead + (pl.ds(c, n, stride=ROW_CHUNKS), slice(None))] = val[:, c * V7X_LANES:(c + 1) * V7X_LANES]


def _row_tile(idx):
    if isinstance(idx, int):
        return pl.ds(idx * ROW_CHUNKS, ROW_CHUNKS)
    return pl.ds(pl.multiple_of(idx * ROW_CHUNKS, ROW_CHUNKS), ROW_CHUNKS)


class _RowGather:
    def __init__(self, idx_ref, src_hbm, buf, sem, tm):
        self.idx_ref, self.src, self.buf, self.sem, self.tm = idx_ref, src_hbm, buf, sem, tm

    def start(self, tile, slot, rows):
        for r in rows:
            pltpu.make_async_copy(self.src.at[_row_tile(self.idx_ref[tile * self.tm + r])],
                                  self.buf.at[slot, _row_tile(r)], self.sem.at[slot]).start()

    def wait(self, slot):
        pltpu.make_async_copy(self.src.at[pl.ds(0, self.tm * ROW_CHUNKS)], self.buf.at[slot],
                              self.sem.at[slot]).wait()


def _fill_xn(xf, xn_ref, scratch, dil):
    tm = xn_ref.shape[0]
    seg = tm // dil
    if dil == 1:
        xn_ref[...] = xf.astype(BF16)
        return
    xs_ref, = scratch
    for c in range(ROW_CHUNKS):
        xs_ref[c] = xf[:, c * V7X_LANES:(c + 1) * V7X_LANES]
    for r in range(dil):
        for c in range(ROW_CHUNKS):
            xn_ref[r * seg:(r + 1) * seg, c * V7X_LANES:(c + 1) * V7X_LANES] = (
                xs_ref[c, pl.ds(r, seg, stride=dil), :].astype(BF16))


def _proj_out(acc, o_ref, gelu_cols, dil):
    seg = acc.shape[0] // dil
    if gelu_cols:
        o_ref[0, :, :gelu_cols] = jax.nn.gelu(acc[:, :gelu_cols]).astype(o_ref.dtype)
        o_ref[0, :, gelu_cols:] = acc[:, gelu_cols:].astype(o_ref.dtype)
    else:
        for r in range(dil):
            o_ref[r] = acc[r * seg:(r + 1) * seg].astype(o_ref.dtype)


def _proj_kernel(h_ref, g_ref, w_ref, o_ref, xn_ref, *scratch, gelu_cols, dil):
    @pl.when(pl.program_id(1) == 0)
    def _():
        _fill_xn(_rms(h_ref[...], g_ref[...]), xn_ref, scratch, dil)

    _proj_out(_dot(xn_ref[...], w_ref[...].astype(BF16)), o_ref, gelu_cols, dil)


def _gather_proj_kernel(pos_ref, hs_hbm, g_ref, w_ref, o_ref, hnat_ref, xn_ref, hbuf, sem,
                        *, gelu_cols, ni, nj):
    i = pl.program_id(0)
    j = pl.program_id(1)
    tm = xn_ref.shape[0]
    per = tm // nj
    slot = i % 2
    gather = _RowGather(pos_ref, hs_hbm, hbuf, sem, tm)

    @pl.when((i == 0) & (j == 0))
    def _():
        gather.start(0, 0, range(tm))

    @pl.when(j == 0)
    def _():
        gather.wait(slot)
        h = _load_rows(hbuf, tm, (slot,))
        hnat_ref[...] = h
        _fill_xn(_rms(h, g_ref[...]), xn_ref, (), 1)

    nxt = jnp.where(i + 1 < ni, i + 1, 0)
    gather.start(nxt, 1 - slot, [j * per + r for r in range(per)])
    _proj_out(_dot(xn_ref[...], w_ref[...].astype(BF16)), o_ref, gelu_cols, 1)

    @pl.when((i == ni - 1) & (j == nj - 1))
    def _():
        gather.wait(1 - slot)


def _proj(h, g, w, *, rows, tm, tn, n, w_lead=0, w_col0=0, gelu_cols=0, dil=1, pos=None):
    seg = tm // dil
    ni, nj = rows // tm, n // tn
    out_z = jax.ShapeDtypeStruct((dil, rows // dil, n), BF16)
    if pos is None:
        return pl.pallas_call(
            functools.partial(_proj_kernel, gelu_cols=gelu_cols, dil=dil),
            grid=(ni, nj),
            in_specs=[
                pl.BlockSpec((tm, D_MODEL), lambda i, j: (i, 0)),
                pl.BlockSpec((1, D_MODEL), lambda i, j: (0, 0)),
                pl.BlockSpec((None, D_MODEL, tn), lambda i, j: (w_lead, 0, w_col0 + j)),
            ],
            out_specs=pl.BlockSpec((dil, seg, tn), lambda i, j: (0, i, j)),
            out_shape=out_z,
            scratch_shapes=[pltpu.VMEM((tm, D_MODEL), BF16)] + (
                [pltpu.VMEM((ROW_CHUNKS, tm, V7X_LANES), F32)] if dil > 1 else []),
            compiler_params=_cparams("parallel", "arbitrary"),
            name=f"norm_proj_d{dil}",
        )(h, g.reshape(1, D_MODEL), w)
    assert dil == 1 and tm % nj == 0
    grid_spec = pltpu.PrefetchScalarGridSpec(
        num_scalar_prefetch=1,
        grid=(ni, nj),
        in_specs=[
            pl.BlockSpec(memory_space=pl.ANY),
            pl.BlockSpec((1, D_MODEL), lambda i, j, p: (0, 0)),
            pl.BlockSpec((None, D_MODEL, tn), lambda i, j, p: (w_lead, 0, w_col0 + j)),
        ],
        out_specs=[pl.BlockSpec((1, tm, tn), lambda i, j, p: (0, i, j)),
                   pl.BlockSpec((tm, D_MODEL), lambda i, j, p: (i, 0))],
        scratch_shapes=[pltpu.VMEM((tm, D_MODEL), BF16),
                        pltpu.VMEM((2, tm * ROW_CHUNKS, V7X_LANES), F32),
                        pltpu.SemaphoreType.DMA((2,))],
    )
    return pl.pallas_call(
        functools.partial(_gather_proj_kernel, gelu_cols=gelu_cols, ni=ni, nj=nj),
        grid_spec=grid_spec,
        out_shape=[out_z, jax.ShapeDtypeStruct((rows, D_MODEL), F32)],
        compiler_params=_cparams("arbitrary", "arbitrary"),
        name="gather_norm_proj",
    )(pos, h, g.reshape(1, D_MODEL), w)


def _t5_bucket_np(rel):
    nb = REL_BUCKETS // 2
    max_exact = nb // 2
    ret = np.where(rel > 0, nb, 0)
    n = np.abs(rel)
    nf = np.maximum(n, 1).astype(np.float32)
    large = max_exact + (np.log(nf / np.float32(max_exact)) / np.float32(math.log(REL_MAX_DIST / max_exact))
                         * np.float32(nb - max_exact)).astype(np.int32)
    large = np.minimum(large, nb - 1)
    return (ret + np.where(n < max_exact, n, large)).astype(np.int32)


def _bias_kernel(table_ref, idx_ref, mask_ref, o_ref, *, block):
    idx = idx_ref[...]
    for h in range(REL_HEADS):
        acc = jnp.zeros(idx.shape, F32)
        for b in range(REL_BUCKETS):
            acc = jnp.where(idx == b, table_ref[b, h], acc)
        for v in range(3):
            o_ref[v, h * block:(h + 1) * block, :] = acc + mask_ref[v]


def _band_bias(table, block, half, dil):
    rel = np.arange(3 * block)[None, :] - block - np.arange(block)[:, None]
    band = np.abs(rel) <= half
    col = np.arange(3 * block)[None, :]
    masks = np.stack([band & (col >= block), band, band & (col < 2 * block)])
    add = np.where(masks, 0.0, NEG_INF).astype(np.float32)
    return pl.pallas_call(
        functools.partial(_bias_kernel, block=block),
        in_specs=[pl.BlockSpec(memory_space=pltpu.SMEM),
                  pl.BlockSpec(memory_space=pltpu.VMEM),
                  pl.BlockSpec(memory_space=pltpu.VMEM)],
        out_specs=pl.BlockSpec(memory_space=pltpu.VMEM),
        out_shape=jax.ShapeDtypeStruct((3, REL_HEADS * block, 3 * block), F32),
        name=f"rel_bias_d{dil}",
    )(table, jnp.asarray(_t5_bucket_np(rel * dil)), jnp.asarray(add))


def _even_mix_kernel(z_ref, kvp_ref, kvn_ref, h_ref, lng_ref, lnb_ref, ws_ref, bs_ref, bias_ref,
                     sink_ref, wout_ref, o_ref, kv_scr, y_scr):
    i = pl.program_id(0)
    nsub = EVEN_TB // B_BLOCK
    nblk = SEQ // B_BLOCK
    kv_scr[0:B_BLOCK] = kvp_ref[...]
    kv_scr[B_BLOCK:B_BLOCK + EVEN_TB] = z_ref[0, :, EVEN_K0:EVEN_IN]
    kv_scr[B_BLOCK + EVEN_TB:] = kvn_ref[...]
    lng = lng_ref[...]
    lnb = lnb_ref[...]
    for s0 in range(0, nsub, EVEN_SUBS):
        subs = []
        for s in range(s0, s0 + EVEN_SUBS):
            r0 = s * B_BLOCK
            gb = i * nsub + s
            subs.append((r0, jnp.where(gb == 0, 0, jnp.where(gb == nblk - 1, 2, 1))))
        for r0, _ in subs:
            u = z_ref[0, r0:r0 + A_CHUNK, 0:A_WIDTH].astype(F32)
            va = z_ref[0, r0:r0 + A_CHUNK, A_WIDTH:2 * A_WIDTH].astype(F32)
            mu = jnp.mean(va, axis=-1, keepdims=True)
            vc = va - mu
            var = jnp.mean(vc * vc, axis=-1, keepdims=True)
            vn = (vc * lax.rsqrt(var + EPS) * lng + lnb).astype(BF16)
            for g in range(A_GROUPS):
                c0 = g * A_CH
                mixed = _dot(ws_ref[g], vn[:, c0:c0 + A_CH]) + bs_ref[g]
                y_scr[r0:r0 + A_CHUNK, c0:c0 + A_CH] = (u[:, c0:c0 + A_CH] * mixed).astype(BF16)
        units = [(r0, sel, kh, kh * B_Q_PER_KV + half * EVEN_STACK)
                 for r0, sel in subs
                 for kh in range(B_KV_HEADS) for half in range(B_Q_PER_KV // EVEN_STACK)]
        lgs = []
        for r0, _, kh, hd0 in units:
            kw = kv_scr[r0:r0 + 3 * B_BLOCK, kh * B_HEAD_DIM:(kh + 1) * B_HEAD_DIM]
            q = jnp.concatenate(
                [z_ref[0, r0:r0 + B_BLOCK,
                       EVEN_Q0 + (hd0 + g) * B_HEAD_DIM:EVEN_Q0 + (hd0 + g + 1) * B_HEAD_DIM]
                 for g in range(EVEN_STACK)], axis=0)
            lgs.append(_dot_nt(q, kw))
        lgs = [lg * (B_HEAD_DIM ** -0.5) + bias_ref[sel, hd0 * B_BLOCK:(hd0 + EVEN_STACK) * B_BLOCK, :]
               for lg, (_, sel, _, hd0) in zip(lgs, units)]
        sks = [sink_ref[hd0 * B_BLOCK:(hd0 + EVEN_STACK) * B_BLOCK, :] for _, _, _, hd0 in units]
        ms = [jnp.maximum(jnp.max(lg, axis=-1, keepdims=True), sk) for lg, sk in zip(lgs, sks)]
        ps = [jnp.exp(lg - m) for lg, m in zip(lgs, ms)]
        dens = [jnp.sum(p, axis=-1, keepdims=True) + jnp.exp(sk - m) for p, sk, m in zip(ps, sks, ms)]
        os_ = []
        for p, (r0, _, kh, _) in zip(ps, units):
            vw = kv_scr[r0:r0 + 3 * B_BLOCK,
                        B_KV_WIDTH + kh * B_HEAD_DIM:B_KV_WIDTH + (kh + 1) * B_HEAD_DIM]
            os_.append(_dot(p.astype(BF16), vw))
        for o, den, (r0, _, _, hd0) in zip(os_, dens, units):
            o = o * (1.0 / den)
            for g in range(EVEN_STACK):
                c0 = A_WIDTH + (hd0 + g) * B_HEAD_DIM
                y_scr[r0:r0 + B_BLOCK, c0:c0 + B_HEAD_DIM] = o[g * B_BLOCK:(g + 1) * B_BLOCK].astype(BF16)
    o_ref[...] = h_ref[...] + _dot(y_scr[...], wout_ref[...])


def _even_mix(z, h, ln_g, ln_b, w_s, b_s, bias, sink, w_out):
    nsub = EVEN_TB // B_BLOCK
    nblk = SEQ // B_BLOCK
    kv_cb = EVEN_K0 // (2 * B_KV_WIDTH)
    sink_col = jnp.broadcast_to(sink.reshape(B_HEADS, 1, 1), (B_HEADS, B_BLOCK, 1)).reshape(
        B_HEADS * B_BLOCK, 1)
    return pl.pallas_call(
        _even_mix_kernel,
        grid=(SEQ // EVEN_TB,),
        in_specs=[
            pl.BlockSpec((1, EVEN_TB, EVEN_IN), lambda i: (0, i, 0)),
            pl.BlockSpec((None, B_BLOCK, 2 * B_KV_WIDTH),
                         lambda i: (0, jnp.maximum(i * nsub - 1, 0), kv_cb)),
            pl.BlockSpec((None, B_BLOCK, 2 * B_KV_WIDTH),
                         lambda i: (0, jnp.minimum((i + 1) * nsub, nblk - 1), kv_cb)),
            pl.BlockSpec((EVEN_TB, D_MODEL), lambda i: (i, 0)),
            pl.BlockSpec((1, A_WIDTH), lambda i: (0, 0)),
            pl.BlockSpec((1, A_WIDTH), lambda i: (0, 0)),
            pl.BlockSpec((A_GROUPS, A_CHUNK, A_CHUNK), lambda i: (0, 0, 0)),
            pl.BlockSpec((A_GROUPS, A_CHUNK, A_CH), lambda i: (0, 0, 0)),
            pl.BlockSpec((3, B_HEADS * B_BLOCK, 3 * B_BLOCK), lambda i: (0, 0, 0)),
            pl.BlockSpec((B_HEADS * B_BLOCK, 1), lambda i: (0, 0)),
            pl.BlockSpec((A_WIDTH + B_WIDTH, D_MODEL), lambda i: (0, 0)),
        ],
        out_specs=pl.BlockSpec((EVEN_TB, D_MODEL), lambda i: (i, 0)),
        out_shape=jax.ShapeDtypeStruct((SEQ, D_MODEL), F32),
        scratch_shapes=[
            pltpu.VMEM((EVEN_TB + 2 * B_BLOCK, 2 * B_KV_WIDTH), BF16),
            pltpu.VMEM((EVEN_TB, A_WIDTH + B_WIDTH), BF16),
        ],
        compiler_params=_cparams("parallel"),
        name="even_mixer",
    )(z, z, z, h, ln_g.reshape(1, A_WIDTH), ln_b.reshape(1, A_WIDTH), w_s.astype(BF16),
      jnp.broadcast_to(b_s[:, :, None], (A_GROUPS, A_CHUNK, A_CH)), bias, sink_col, w_out)


def _dil_attn_kernel(zc_ref, kp_ref, kn_ref, vp_ref, vn_ref, bias_ref, o_ref, lse_ref,
                     k_scr, v_scr, o_scr, lse_scr, *, dil):
    t = pl.program_id(0)
    seg = ODD_TILE // dil
    nsb = seg // C_BLOCK
    nblk = SEQ // dil // C_BLOCK
    k_scr[:, 0:C_BLOCK] = kp_ref[...]
    k_scr[:, C_BLOCK:C_BLOCK + seg] = zc_ref[:, :, C_WIDTH:2 * C_WIDTH]
    k_scr[:, C_BLOCK + seg:] = kn_ref[...]
    v_scr[:, 0:C_BLOCK] = vp_ref[...]
    v_scr[:, C_BLOCK:C_BLOCK + seg] = zc_ref[:, :, 2 * C_WIDTH:3 * C_WIDTH]
    v_scr[:, C_BLOCK + seg:] = vn_ref[...]
    lane = lax.broadcasted_iota(jnp.int32, (C_BLOCK, V7X_LANES), 1)

    def body(it, carry):
        blocks = []
        for u in range(ODD_INTERLEAVE):
            n = it * ODD_INTERLEAVE + u
            r = n // nsb
            s = n % nsb
            r0 = pl.multiple_of(s * C_BLOCK, C_BLOCK)
            gb = t * nsb + s
            sel = jnp.where(gb == 0, 0, jnp.where(gb == nblk - 1, 2, 1))
            rows = (pl.ds(s * (C_BLOCK * dil) + r, C_BLOCK, stride=dil) if dil > 1
                    else pl.ds(r0, C_BLOCK))
            blocks.append((r, r0, sel, rows))
        lgs = []
        for r, r0, _, _ in blocks:
            for hd in range(C_HEADS):
                c0 = hd * C_HEAD_DIM
                q = zc_ref[r, pl.ds(r0, C_BLOCK), c0:c0 + C_HEAD_DIM]
                kw = k_scr[r, pl.ds(r0, 3 * C_BLOCK), c0:c0 + C_HEAD_DIM]
                lgs.append(_dot_nt(q, kw))
        lg = [jnp.concatenate(lgs[u * C_HEADS:(u + 1) * C_HEADS], axis=0) * (C_HEAD_DIM ** -0.5)
              + bias_ref[blk[2]] for u, blk in enumerate(blocks)]
        m = [jnp.max(x, axis=-1, keepdims=True) for x in lg]
        p = [jnp.exp(x - mm) for x, mm in zip(lg, m)]
        den = [jnp.sum(x, axis=-1, keepdims=True) for x in p]
        inv = [1.0 / d for d in den]
        lse = [mm + jnp.log(d) for mm, d in zip(m, den)]
        pb = [x.astype(BF16) for x in p]
        outs = []
        for u, (r, r0, _, _) in enumerate(blocks):
            for hd in range(C_HEADS):
                c0 = hd * C_HEAD_DIM
                vw = v_scr[r, pl.ds(r0, 3 * C_BLOCK), c0:c0 + C_HEAD_DIM]
                outs.append(_dot(pb[u][hd * C_BLOCK:(hd + 1) * C_BLOCK], vw))
        for u, (_, _, _, rows) in enumerate(blocks):
            lse_tile = jnp.zeros((C_BLOCK, V7X_LANES), F32)
            for hd in range(C_HEADS):
                o_scr[hd, rows, :] = outs[u * C_HEADS + hd] * inv[u][hd * C_BLOCK:(hd + 1) * C_BLOCK]
                lse_tile = jnp.where(lane == hd, lse[u][hd * C_BLOCK:(hd + 1) * C_BLOCK], lse_tile)
            lse_scr[rows, :] = lse_tile
        return carry

    lax.fori_loop(0, ODD_BLOCKS // ODD_INTERLEAVE, body, 0)
    for hd in range(C_HEADS):
        o_ref[:, hd * C_HEAD_DIM:(hd + 1) * C_HEAD_DIM] = o_scr[hd].astype(o_ref.dtype)
    lse_ref[...] = lse_scr[...]


def _dil_attn(zg, bias, dil):
    seg = ODD_TILE // dil
    nsb = seg // C_BLOCK
    last = SEQ // dil // C_BLOCK - 1

    def halo(j, nxt):
        if nxt:
            return pl.BlockSpec((dil, C_BLOCK, C_WIDTH),
                                lambda t: (0, jnp.minimum((t + 1) * nsb, last), j))
        return pl.BlockSpec((dil, C_BLOCK, C_WIDTH), lambda t: (0, jnp.maximum(t * nsb - 1, 0), j))

    return pl.pallas_call(
        functools.partial(_dil_attn_kernel, dil=dil),
        grid=(SEQ // ODD_TILE,),
        in_specs=[pl.BlockSpec((dil, seg, 3 * C_WIDTH), lambda t: (0, t, 0)),
                  halo(1, False), halo(1, True), halo(2, False), halo(2, True),
                  pl.BlockSpec((3, C_HEADS * C_BLOCK, 3 * C_BLOCK), lambda t: (0, 0, 0))],
        out_specs=[pl.BlockSpec((ODD_TILE, C_WIDTH), lambda t: (t, 0)),
                   pl.BlockSpec((ODD_TILE, V7X_LANES), lambda t: (t, 0))],
        out_shape=[jax.ShapeDtypeStruct((SEQ, C_WIDTH), BF16),
                   jax.ShapeDtypeStruct((SEQ, V7X_LANES), F32)],
        scratch_shapes=[pltpu.VMEM((dil, seg + 2 * C_BLOCK, C_WIDTH), BF16),
                        pltpu.VMEM((dil, seg + 2 * C_BLOCK, C_WIDTH), BF16),
                        pltpu.VMEM((C_HEADS, ODD_TILE, C_HEAD_DIM), F32),
                        pltpu.VMEM((ODD_TILE, V7X_LANES), F32)],
        compiler_params=_cparams("parallel"),
        name=f"dilated_attn_d{dil}",
    )(zg, zg, zg, zg, zg, bias)


def _combine_kernel(o0_ref, o1_ref, o2_ref, l0_ref, l1_ref, l2_ref, h_ref, wout_ref, out_ref, y_scr):
    l0 = l0_ref[...]
    l1 = l1_ref[...]
    l2 = l2_ref[...]
    m = jnp.maximum(jnp.maximum(l0, l1), l2)
    e0 = jnp.exp(l0 - m)
    e1 = jnp.exp(l1 - m)
    e2 = jnp.exp(l2 - m)
    tot = e0 + e1 + e2
    w0 = e0 / tot
    w1 = e1 / tot
    w2 = e2 / tot
    for hd in range(C_HEADS):
        c0 = hd * C_HEAD_DIM
        y = (w0[:, hd:hd + 1] * o0_ref[:, c0:c0 + C_HEAD_DIM].astype(F32)
             + w1[:, hd:hd + 1] * o1_ref[:, c0:c0 + C_HEAD_DIM].astype(F32)
             + w2[:, hd:hd + 1] * o2_ref[:, c0:c0 + C_HEAD_DIM].astype(F32))
        y_scr[:, c0:c0 + C_HEAD_DIM] = y.astype(BF16)
    out_ref[...] = h_ref[...] + _dot(y_scr[...], wout_ref[...])


def _combine(outs, lses, h, w_out):
    blk_o = pl.BlockSpec((COMB_TB, C_WIDTH), lambda i: (i, 0))
    blk_l = pl.BlockSpec((COMB_TB, V7X_LANES), lambda i: (i, 0))
    return pl.pallas_call(
        _combine_kernel,
        grid=(SEQ // COMB_TB,),
        in_specs=[blk_o, blk_o, blk_o, blk_l, blk_l, blk_l,
                  pl.BlockSpec((COMB_TB, D_MODEL), lambda i: (i, 0)),
                  pl.BlockSpec((C_WIDTH, D_MODEL), lambda i: (0, 0))],
        out_specs=pl.BlockSpec((COMB_TB, D_MODEL), lambda i: (i, 0)),
        out_shape=jax.ShapeDtypeStruct((SEQ, D_MODEL), F32),
        scratch_shapes=[pltpu.VMEM((COMB_TB, C_WIDTH), BF16)],
        compiler_params=_cparams("parallel"),
        name="group_combine_proj",
    )(*outs, *lses, h, w_out)


def _cross_kernel(h_ref, gx_ref, wq_ref, kv_ref, wo_ref, gf_ref, wr_ref, br_ref,
                  hx_ref, meta_ref, o_scr):
    h = h_ref[...]
    q = _dot(_rms(h, gx_ref[...]).astype(BF16), wq_ref[...]).astype(BF16)
    cols = [hd * X_HEAD_DIM for hd in range(X_HEADS)]
    lgs = [_dot_nt(q[:, c0:c0 + X_HEAD_DIM], kv_ref[0, :, c0:c0 + X_HEAD_DIM]) * (X_HEAD_DIM ** -0.5)
           for c0 in cols]
    ms = [jnp.max(lg, axis=-1, keepdims=True) for lg in lgs]
    ps = [jnp.exp(lg - m) for lg, m in zip(lgs, ms)]
    dens = [jnp.sum(p, axis=-1, keepdims=True) for p in ps]
    os_ = [_dot(p.astype(BF16), kv_ref[0, :, X_WIDTH + c0:X_WIDTH + c0 + X_HEAD_DIM])
           for p, c0 in zip(ps, cols)]
    for o, den, c0 in zip(os_, dens, cols):
        o_scr[:, c0:c0 + X_HEAD_DIM] = (o / den).astype(BF16)
    h2 = h + _dot(o_scr[...], wo_ref[...])
    _store_rows(hx_ref, h2)

    t = _rms(h2, gf_ref[...])
    t_hi = t.astype(BF16)
    t_lo = (t - t_hi.astype(F32)).astype(BF16)
    lt = (_dot_nt(wr_ref[0], t_hi) + _dot_nt(wr_ref[0], t_lo) + _dot_nt(wr_ref[1], t_hi)) + br_ref[...]
    g = [lt[k:k + 1, :] for k in range(MOE_GROUPS)]
    gmax = jnp.maximum(jnp.maximum(g[0], g[1]), jnp.maximum(g[2], g[3]))
    grp = jnp.where(g[0] == gmax, 0, jnp.where(g[1] == gmax, 1, jnp.where(g[2] == gmax, 2, 3)))
    g_gate = 1.0 / (jnp.exp(g[0] - gmax) + jnp.exp(g[1] - gmax) + jnp.exp(g[2] - gmax)
                    + jnp.exp(g[3] - gmax))
    e = []
    for k in range(MOE_EPG):
        rows = [lt[MOE_GROUPS + gi * MOE_EPG + k:MOE_GROUPS + gi * MOE_EPG + k + 1, :]
                for gi in range(MOE_GROUPS)]
        e.append(jnp.where(grp == 0, rows[0], jnp.where(grp == 1, rows[1],
                                                         jnp.where(grp == 2, rows[2], rows[3]))))
    v1 = jnp.maximum(jnp.maximum(e[0], e[1]), jnp.maximum(e[2], e[3]))
    i1 = jnp.where(e[0] == v1, 0, jnp.where(e[1] == v1, 1, jnp.where(e[2] == v1, 2, 3)))
    r = [jnp.where(i1 == k, -jnp.inf, e[k]) for k in range(MOE_EPG)]
    v2 = jnp.maximum(jnp.maximum(r[0], r[1]), jnp.maximum(r[2], r[3]))
    i2 = jnp.where(r[0] == v2, 0, jnp.where(r[1] == v2, 1, jnp.where(r[2] == v2, 2, 3)))
    d = jnp.exp(v2 - v1)
    w1 = g_gate / (1.0 + d)
    w2 = g_gate * d / (1.0 + d)
    first_lo = i1 < i2
    lo = jnp.where(first_lo, i1, i2)
    hi = jnp.where(first_lo, i2, i1)
    w_lo = jnp.where(first_lo, w1, w2)
    w_hi = jnp.where(first_lo, w2, w1)
    pair = jnp.where(lo == 0, hi - 1, jnp.where(lo == 1, jnp.where(hi == 3, 3, 4), 5))
    w_a = jnp.where(lo == 2, w_hi, w_lo)
    w_b = jnp.where(lo == 2, w_lo, w_hi)
    bucket = (grp * N_PAIRS + pair).astype(F32)
    row = lax.broadcasted_iota(jnp.int32, (8, CROSS_TB), 0)
    meta_ref[...] = jnp.where(row == 0, bucket, jnp.where(row == 1, w_a, jnp.where(row == 2, w_b, 0.0)))


def _cross_router(h, g_cross, wq, kv, wo, g_ffn, wr_t, br):
    full = lambda shape: pl.BlockSpec(shape, lambda i: tuple(0 for _ in shape))
    return pl.pallas_call(
        _cross_kernel,
        grid=(SEQ // CROSS_TB,),
        in_specs=[
            pl.BlockSpec((CROSS_TB, D_MODEL), lambda i: (i, 0)),
            full((1, D_MODEL)),
            full((D_MODEL, X_WIDTH)),
            full((1, MEM_LEN, 2 * X_WIDTH)),
            full((X_WIDTH, D_MODEL)),
            full((1, D_MODEL)),
            full((2, ROUTER_ROWS, D_MODEL)),
            full((ROUTER_ROWS, 1)),
        ],
        out_specs=[pl.BlockSpec((CROSS_TB * ROW_CHUNKS, V7X_LANES), lambda i: (i, 0)),
                   pl.BlockSpec((8, CROSS_TB), lambda i: (0, i))],
        out_shape=[jax.ShapeDtypeStruct((SEQ * ROW_CHUNKS, V7X_LANES), F32),
                   jax.ShapeDtypeStruct((8, SEQ), F32)],
        scratch_shapes=[pltpu.VMEM((CROSS_TB, X_WIDTH), BF16)],
        compiler_params=_cparams("parallel"),
        name="cross_attn_router",
    )(h, g_cross.reshape(1, D_MODEL), wq, kv, wo, g_ffn.reshape(1, D_MODEL), wr_t, br)


def _moe_kernel(src_ref, ea_ref, eb_ref, nused_ref,
                hx_hbm, gates_ref, gf_ref, wga_ref, wua_ref, wda_ref, wgb_ref, wub_ref, wdb_ref,
                out_ref, xbuf, wup_a, wdn_a, wup_b, wdn_b, gsem):
    k = pl.program_id(0)
    nused = nused_ref[0]
    xslot = k % 3
    gather = _RowGather(src_ref, hx_hbm, xbuf, gsem, MOE_TM)

    @pl.when(k == 0)
    def _():
        gather.start(0, 0, range(MOE_TM))
        gather.start(1, 1, range(MOE_TM))

    prev = jnp.maximum(k - 1, 0)

    @pl.when((k < nused) & ((k == 0) | (ea_ref[k] != ea_ref[prev])))
    def _():
        wup_a[:, :D_EXPERT] = wga_ref[0].astype(BF16)
        wup_a[:, D_EXPERT:] = wua_ref[0].astype(BF16)
        wdn_a[...] = wda_ref[0].astype(BF16)

    @pl.when((k < nused) & ((k == 0) | (eb_ref[k] != eb_ref[prev])))
    def _():
        wup_b[:, :D_EXPERT] = wgb_ref[0].astype(BF16)
        wup_b[:, D_EXPERT:] = wub_ref[0].astype(BF16)
        wdn_b[...] = wdb_ref[0].astype(BF16)

    @pl.when(k < nused)
    def _():
        gather.wait(xslot)
        h2 = _load_rows(xbuf, MOE_TM, (xslot,))
        gather.start(k + 2, (k + 2) % 3, range(MOE_TM))
        t = _rms(h2, gf_ref[...]).astype(BF16)
        gus = [_dot(t, wup[...]) for wup in (wup_a, wup_b)]
        hids = [(jax.nn.silu(gu[:, :D_EXPERT]) * gu[:, D_EXPERT:] * gates_ref[:, col:col + 1]).astype(BF16)
                for col, gu in enumerate(gus)]
        ys = [_dot(hid, wdn[...]) for hid, wdn in zip(hids, (wdn_a, wdn_b))]
        _store_rows(out_ref, h2 + (ys[0] + ys[1]))

    @pl.when(k >= nused)
    def _():
        out_ref[...] = jnp.zeros(out_ref.shape, F32)

    @pl.when(k == nused - 1)
    def _():
        gather.wait((k + 1) % 3)
        gather.wait((k + 2) % 3)


def _moe(hx, gates, g_ffn, w_gate, w_up, w_down, src, ea, eb, nused):
    def wspec(shape, which):
        if which == 0:
            return pl.BlockSpec((1,) + shape, lambda k, s, a, b, n: (a[k], 0, 0))
        return pl.BlockSpec((1,) + shape, lambda k, s, a, b, n: (b[k], 0, 0))

    up_shape = (D_MODEL, D_EXPERT)
    down_shape = (D_EXPERT, D_MODEL)
    grid_spec = pltpu.PrefetchScalarGridSpec(
        num_scalar_prefetch=4,
        grid=(MOE_TILES,),
        in_specs=[
            pl.BlockSpec(memory_space=pl.ANY),
            pl.BlockSpec((MOE_TM, V7X_LANES), lambda k, s, a, b, n: (k, 0)),
            pl.BlockSpec((1, D_MODEL), lambda k, s, a, b, n: (0, 0)),
            wspec(up_shape, 0), wspec(up_shape, 0), wspec(down_shape, 0),
            wspec(up_shape, 1), wspec(up_shape, 1), wspec(down_shape, 1),
        ],
        out_specs=pl.BlockSpec((MOE_TM * ROW_CHUNKS, V7X_LANES), lambda k, s, a, b, n: (k, 0)),
        scratch_shapes=[
            pltpu.VMEM((3, MOE_TM * ROW_CHUNKS, V7X_LANES), F32),
            pltpu.VMEM((D_MODEL, 2 * D_EXPERT), BF16),
            pltpu.VMEM((D_EXPERT, D_MODEL), BF16),
            pltpu.VMEM((D_MODEL, 2 * D_EXPERT), BF16),
            pltpu.VMEM((D_EXPERT, D_MODEL), BF16),
            pltpu.SemaphoreType.DMA((3,)),
        ],
    )
    return pl.pallas_call(
        _moe_kernel,
        grid_spec=grid_spec,
        out_shape=jax.ShapeDtypeStruct((MOE_TILES * MOE_TM * ROW_CHUNKS, V7X_LANES), F32),
        compiler_params=_cparams("arbitrary"),
        name="routed_moe",
    )(src, ea, eb, nused, hx, gates, g_ffn.reshape(1, D_MODEL),
      w_gate, w_up, w_down, w_gate, w_up, w_down)


def _route_tables(meta):
    bucket = meta[0].astype(jnp.int32)
    ids = jnp.arange(N_BUCKETS, dtype=jnp.int32)
    counts = jnp.sum((bucket[:, None] == ids[None, :]).astype(jnp.int32), axis=0)
    ntile = (counts + MOE_TM - 1) // MOE_TM
    pad = ntile * MOE_TM - counts
    tile_end = jnp.cumsum(ntile)
    nused = tile_end[-1]
    dummy_key = jnp.where(jnp.arange(MOE_TM - 1, dtype=jnp.int32)[None, :] < pad[:, None],
                          ids[:, None], N_BUCKETS)
    keys = jnp.concatenate([bucket, dummy_key.reshape(-1)])
    vals = jnp.concatenate([jnp.arange(SEQ, dtype=jnp.int32),
                            jnp.full((N_BUCKETS * (MOE_TM - 1),), SEQ, jnp.int32)])
    nslot = MOE_TILES * MOE_TM
    zpad = jnp.zeros((N_BUCKETS * (MOE_TM - 1),), F32)
    _, tok, ga, gb = lax.sort((keys, vals, jnp.concatenate([meta[1], zpad]), jnp.concatenate([meta[2], zpad])),
                              num_keys=1, is_stable=True)
    tok = tok[:nslot]
    gates = jnp.pad(jnp.stack([ga[:nslot], gb[:nslot]], axis=1), ((0, 0), (0, V7X_LANES - 2)))
    valid = tok < SEQ
    src = jnp.concatenate([jnp.where(valid, tok, 0), jnp.zeros((2 * MOE_TM,), jnp.int32)])
    _, pos = lax.sort((tok, jnp.arange(nslot, dtype=jnp.int32)), num_keys=1, is_stable=True)
    pos = pos[:SEQ]
    tiles = jnp.arange(MOE_TILES, dtype=jnp.int32)
    tile_bucket = jnp.minimum(jnp.sum((tiles[:, None] >= tile_end[None, :]).astype(jnp.int32), axis=1),
                              N_BUCKETS - 1)
    onehot = (tile_bucket[:, None] == ids[None, :]).astype(jnp.int32)
    base = (np.arange(N_BUCKETS) // N_PAIRS) * MOE_EPG
    ea = jnp.sum(onehot * jnp.asarray(base + np.asarray(SLOT_A)[np.arange(N_BUCKETS) % N_PAIRS],
                                      jnp.int32)[None, :], axis=1)
    eb = jnp.sum(onehot * jnp.asarray(base + np.asarray(SLOT_B)[np.arange(N_BUCKETS) % N_PAIRS],
                                      jnp.int32)[None, :], axis=1)
    return (src.astype(jnp.int32), pos.astype(jnp.int32), ea.astype(jnp.int32), eb.astype(jnp.int32),
            nused.reshape(1).astype(jnp.int32), gates)


def _final_norm_kernel(pos_ref, hs_hbm, g_ref, o_ref, hbuf, sem, *, ni):
    i = pl.program_id(0)
    tm = o_ref.shape[0]
    slot = i % 2
    gather = _RowGather(pos_ref, hs_hbm, hbuf, sem, tm)

    @pl.when(i == 0)
    def _():
        gather.start(0, 0, range(tm))

    @pl.when(i + 1 < ni)
    def _():
        gather.start(i + 1, 1 - slot, range(tm))

    gather.wait(slot)
    o_ref[...] = _rms(_load_rows(hbuf, tm, (slot,)), g_ref[...])


def _final_norm(hs, pos, g):
    tb = FINAL_TB
    ni = SEQ // tb
    grid_spec = pltpu.PrefetchScalarGridSpec(
        num_scalar_prefetch=1,
        grid=(ni,),
        in_specs=[pl.BlockSpec(memory_space=pl.ANY),
                  pl.BlockSpec((1, D_MODEL), lambda i, p: (0, 0))],
        out_specs=pl.BlockSpec((tb, D_MODEL), lambda i, p: (i, 0)),
        scratch_shapes=[pltpu.VMEM((2, tb * ROW_CHUNKS, V7X_LANES), F32),
                        pltpu.SemaphoreType.DMA((2,))],
    )
    return pl.pallas_call(
        functools.partial(_final_norm_kernel, ni=ni),
        grid_spec=grid_spec,
        out_shape=jax.ShapeDtypeStruct((SEQ, D_MODEL), F32),
        compiler_params=_cparams("arbitrary"),
        name="final_norm",
    )(pos, hs, g.reshape(1, D_MODEL))


def kernel(x, mem, ln_mix, ln_cross, ln_mem, ln_ffn, ln_final, rel_table, even_w_in, even_w_out,
           sgu_ln_g, sgu_ln_b, sgu_w, sgu_b, attn_sink, odd_w_in, odd_w_out, xq_w, xkv_w, xo_w,
           router_group_w, router_group_b, router_expert_w, router_expert_b,
           expert_w_gate, expert_w_up, expert_w_down):
    h = x.reshape(SEQ, D_MODEL)
    mem2 = mem.reshape(MEM_LEN, D_MODEL)
    bias_even = _band_bias(rel_table, B_BLOCK, B_HALF_WINDOW, 1)
    bias_odd = [_band_bias(rel_table, C_BLOCK, window // 2 // dil, dil) for window, dil in C_PAIRS]

    pos = None
    for layer in range(DEPTH):
        i = layer // 2
        if layer % 2 == 0:
            z = _proj(h, ln_mix[layer], even_w_in, w_lead=i, rows=SEQ, tm=EVEN_TB, tn=EVEN_IN, n=EVEN_IN,
                      gelu_cols=2 * A_WIDTH, pos=pos)
            if pos is not None:
                z, h = z
            h = _even_mix(z, h, sgu_ln_g[i], sgu_ln_b[i], sgu_w[i], sgu_b[i], bias_even,
                          attn_sink[i], even_w_out[i].astype(BF16))
        else:
            zg, h = _proj(h, ln_mix[layer], odd_w_in, w_lead=i, rows=SEQ, tm=PROJ_TM, tn=GATHER_TN,
                          n=3 * C_WIDTH, pos=pos)
            outs, lses = [], []
            for gi, (_, dil) in enumerate(C_PAIRS):
                if gi > 0:
                    zg = _proj(h, ln_mix[layer], odd_w_in, w_lead=i, w_col0=gi * 3 * C_WIDTH // PROJ_TN,
                               rows=SEQ, tm=PROJ_TM, tn=PROJ_TN, n=3 * C_WIDTH, dil=dil)
                o, lse = _dil_attn(zg, bias_odd[gi], dil)
                outs.append(o)
                lses.append(lse)
            h = _combine(outs, lses, h, odd_w_out[i].astype(BF16))

        kv = _proj(mem2, ln_mem[layer], xkv_w, w_lead=layer, rows=MEM_LEN, tm=MEM_LEN,
                   tn=2 * X_WIDTH, n=2 * X_WIDTH)
        wr_t = jnp.zeros((ROUTER_ROWS, D_MODEL), F32)
        wr_t = wr_t.at[:MOE_GROUPS].set(router_group_w[layer].T)
        wr_t = wr_t.at[MOE_GROUPS:MOE_GROUPS + N_EXPERTS].set(
            router_expert_w[layer].reshape(D_MODEL, N_EXPERTS).T)
        br = jnp.zeros((ROUTER_ROWS, 1), F32)
        br = br.at[:MOE_GROUPS, 0].set(router_group_b[layer])
        br = br.at[MOE_GROUPS:MOE_GROUPS + N_EXPERTS, 0].set(router_expert_b[layer].reshape(N_EXPERTS))
        wr_hi = wr_t.astype(BF16)
        wr_split = jnp.stack([wr_hi, (wr_t - wr_hi.astype(F32)).astype(BF16)])
        hx, meta = _cross_router(h, ln_cross[layer], xq_w[layer].astype(BF16), kv,
                                 xo_w[layer].astype(BF16), ln_ffn[layer], wr_split, br)

        src, pos, ea, eb, nused, gates = _route_tables(meta)
        h = _moe(hx, gates, ln_ffn[layer],
                 expert_w_gate.reshape(DEPTH * N_EXPERTS, D_MODEL, D_EXPERT),
                 expert_w_up.reshape(DEPTH * N_EXPERTS, D_MODEL, D_EXPERT),
                 expert_w_down.reshape(DEPTH * N_EXPERTS, D_EXPERT, D_MODEL),
                 src, ea + layer * N_EXPERTS, eb + layer * N_EXPERTS, nused)

    return _final_norm(h, pos, ln_final).reshape(1, SEQ, D_MODEL)
```

```python
import functools
import math

import numpy as np
import jax
import jax.numpy as jnp
from jax import lax
from jax.experimental import pallas as pl
from jax.experimental.pallas import tpu as pltpu

F32 = jnp.float32
BF16 = jnp.bfloat16

D_MODEL = 1024
SEQ = 16384
DEPTH = 4
MEM_LEN = 256
EPS = 1e-6
NEG_INF = -1e30

A_GROUPS = 4
A_CH = 128
A_WIDTH = A_GROUPS * A_CH
A_CHUNK = 128
B_HEADS = 8
B_KV_HEADS = 2
B_Q_PER_KV = B_HEADS // B_KV_HEADS
B_HEAD_DIM = 64
B_WIDTH = B_HEADS * B_HEAD_DIM
B_KV_WIDTH = B_KV_HEADS * B_HEAD_DIM
B_HALF_WINDOW = 128
B_BLOCK = 128
EVEN_IN = 2 * A_WIDTH + B_WIDTH + 2 * B_KV_WIDTH
EVEN_Q0 = 2 * A_WIDTH
EVEN_K0 = EVEN_Q0 + B_WIDTH
EVEN_V0 = EVEN_K0 + B_KV_WIDTH

C_PAIRS = ((128, 1), (512, 4), (2048, 16))
C_GROUPS = len(C_PAIRS)
C_HEADS = 8
C_HEAD_DIM = 128
C_WIDTH = C_HEADS * C_HEAD_DIM
C_BLOCK = 64
ODD_IN = C_GROUPS * 3 * C_WIDTH

REL_BUCKETS = 32
REL_MAX_DIST = 1024
REL_HEADS = 8

X_HEADS = 4
X_HEAD_DIM = 128
X_WIDTH = X_HEADS * X_HEAD_DIM

MOE_GROUPS = 4
MOE_EPG = 4
N_EXPERTS = MOE_GROUPS * MOE_EPG
D_EXPERT = 512
SLOT_A = (0, 0, 0, 1, 1, 3)
SLOT_B = (1, 2, 3, 3, 2, 2)
N_PAIRS = len(SLOT_A)
N_BUCKETS = MOE_GROUPS * N_PAIRS

V7X_LANES = 128
ROW_CHUNKS = D_MODEL // V7X_LANES
V7X_VMEM_BYTES = 64 * 1024 * 1024
VMEM_LIMIT = 62 * 1024 * 1024

PROJ_TM = 1024
PROJ_TN = 1024
EVEN_TB = 512
EVEN_STACK = 2
EVEN_SUBS = 4
ODD_TILE = PROJ_TM
ODD_BLOCKS = ODD_TILE // C_BLOCK
ODD_INTERLEAVE = 8
COMB_TB = 512
CROSS_TB = 512
MOE_TM = 256
ROUTER_ROWS = 32
MOE_TILES = (SEQ + N_BUCKETS * (MOE_TM - 1)) // MOE_TM
GATHER_TN = 1536
FINAL_TB = 512


def _cparams(*sem):
    return pltpu.CompilerParams(dimension_semantics=sem, vmem_limit_bytes=VMEM_LIMIT)


def _rms(x, g):
    return x * lax.rsqrt(jnp.mean(x * x, axis=-1, keepdims=True) + EPS) * g


def _dot(a, b):
    return jnp.dot(a, b, preferred_element_type=F32)


def _dot_nt(a, b):
    return lax.dot_general(a, b, (((1,), (1,)), ((), ())), preferred_element_type=F32)


def _load_rows(ref, n, lead=()):
    return jnp.concatenate([ref[lead + (pl.ds(c, n, stride=ROW_CHUNKS), slice(None))]
                            for c in range(ROW_CHUNKS)], axis=1)


def _store_rows(ref, val, lead=()):
    n = val.shape[0]
    for c in range(ROW_CHUNKS):
        ref[lead + (pl.ds(c, n, stride=ROW_CHUNKS), slice(None))] = val[:, c * V7X_LANES:(c + 1) * V7X_LANES]


def _row_tile(idx):
    if isinstance(idx, int):
        return pl.ds(idx * ROW_CHUNKS, ROW_CHUNKS)
    return pl.ds(pl.multiple_of(idx * ROW_CHUNKS, ROW_CHUNKS), ROW_CHUNKS)


class _RowGather:
    def __init__(self, idx_ref, src_hbm, buf, sem, tm, both_queues=False):
        self.idx_ref, self.src, self.buf, self.sem, self.tm = idx_ref, src_hbm, buf, sem, tm
        self.both_queues = both_queues

    def start(self, tile, slot, rows):
        for n, r in enumerate(rows):
            pltpu.make_async_copy(self.src.at[_row_tile(self.idx_ref[tile * self.tm + r])],
                                  self.buf.at[slot, _row_tile(r)], self.sem.at[slot]).start(
                                      priority=n % 2 if self.both_queues else 0)

    def wait(self, slot):
        pltpu.make_async_copy(self.src.at[pl.ds(0, self.tm * ROW_CHUNKS)], self.buf.at[slot],
                              self.sem.at[slot]).wait()


def _fill_xn(xf, xn_ref, scratch, dil):
    tm = xn_ref.shape[0]
    seg = tm // dil
    if dil == 1:
        xn_ref[...] = xf.astype(BF16)
        return
    xs_ref, = scratch
    for c in range(ROW_CHUNKS):
        xs_ref[c] = xf[:, c * V7X_LANES:(c + 1) * V7X_LANES]
    for r in range(dil):
        for c in range(ROW_CHUNKS):
            xn_ref[r * seg:(r + 1) * seg, c * V7X_LANES:(c + 1) * V7X_LANES] = (
                xs_ref[c, pl.ds(r, seg, stride=dil), :].astype(BF16))


def _proj_out(acc, o_ref, gelu_cols, dil):
    seg = acc.shape[0] // dil
    if gelu_cols:
        o_ref[0, :, :gelu_cols] = jax.nn.gelu(acc[:, :gelu_cols]).astype(o_ref.dtype)
        o_ref[0, :, gelu_cols:] = acc[:, gelu_cols:].astype(o_ref.dtype)
    else:
        for r in range(dil):
            o_ref[r] = acc[r * seg:(r + 1) * seg].astype(o_ref.dtype)


def _proj_kernel(h_ref, g_ref, w_ref, o_ref, xn_ref, *scratch, gelu_cols, dil):
    @pl.when(pl.program_id(1) == 0)
    def _():
        _fill_xn(_rms(h_ref[...], g_ref[...]), xn_ref, scratch, dil)

    _proj_out(_dot(xn_ref[...], w_ref[...].astype(BF16)), o_ref, gelu_cols, dil)


def _gather_proj_kernel(pos_ref, hs_hbm, g_ref, w_ref, o_ref, hnat_ref, xn_ref, hbuf, sem,
                        *, gelu_cols, ni, nj):
    i = pl.program_id(0)
    j = pl.program_id(1)
    tm = xn_ref.shape[0]
    per = tm // nj
    slot = i % 2
    gather = _RowGather(pos_ref, hs_hbm, hbuf, sem, tm)

    @pl.when((i == 0) & (j == 0))
    def _():
        gather.start(0, 0, range(tm))

    @pl.when(j == 0)
    def _():
        gather.wait(slot)
        h = _load_rows(hbuf, tm, (slot,))
        hnat_ref[...] = h
        _fill_xn(_rms(h, g_ref[...]), xn_ref, (), 1)

    nxt = jnp.where(i + 1 < ni, i + 1, 0)
    gather.start(nxt, 1 - slot, [j * per + r for r in range(per)])
    _proj_out(_dot(xn_ref[...], w_ref[...].astype(BF16)), o_ref, gelu_cols, 1)

    @pl.when((i == ni - 1) & (j == nj - 1))
    def _():
        gather.wait(1 - slot)


def _proj(h, g, w, *, rows, tm, tn, n, w_lead=0, w_col0=0, gelu_cols=0, dil=1, pos=None):
    seg = tm // dil
    ni, nj = rows // tm, n // tn
    out_z = jax.ShapeDtypeStruct((dil, rows // dil, n), BF16)
    if pos is None:
        return pl.pallas_call(
            functools.partial(_proj_kernel, gelu_cols=gelu_cols, dil=dil),
            grid=(ni, nj),
            in_specs=[
                pl.BlockSpec((tm, D_MODEL), lambda i, j: (i, 0)),
                pl.BlockSpec((1, D_MODEL), lambda i, j: (0, 0)),
                pl.BlockSpec((None, D_MODEL, tn), lambda i, j: (w_lead, 0, w_col0 + j)),
            ],
            out_specs=pl.BlockSpec((dil, seg, tn), lambda i, j: (0, i, j)),
            out_shape=out_z,
            scratch_shapes=[pltpu.VMEM((tm, D_MODEL), BF16)] + (
                [pltpu.VMEM((ROW_CHUNKS, tm, V7X_LANES), F32)] if dil > 1 else []),
            compiler_params=_cparams("parallel", "arbitrary"),
            name=f"norm_proj_d{dil}",
        )(h, g.reshape(1, D_MODEL), w)
    assert dil == 1 and tm % nj == 0
    grid_spec = pltpu.PrefetchScalarGridSpec(
        num_scalar_prefetch=1,
        grid=(ni, nj),
        in_specs=[
            pl.BlockSpec(memory_space=pl.ANY),
            pl.BlockSpec((1, D_MODEL), lambda i, j, p: (0, 0)),
            pl.BlockSpec((None, D_MODEL, tn), lambda i, j, p: (w_lead, 0, w_col0 + j)),
        ],
        out_specs=[pl.BlockSpec((1, tm, tn), lambda i, j, p: (0, i, j)),
                   pl.BlockSpec((tm, D_MODEL), lambda i, j, p: (i, 0))],
        scratch_shapes=[pltpu.VMEM((tm, D_MODEL), BF16),
                        pltpu.VMEM((2, tm * ROW_CHUNKS, V7X_LANES), F32),
                        pltpu.SemaphoreType.DMA((2,))],
    )
    return pl.pallas_call(
        functools.partial(_gather_proj_kernel, gelu_cols=gelu_cols, ni=ni, nj=nj),
        grid_spec=grid_spec,
        out_shape=[out_z, jax.ShapeDtypeStruct((rows, D_MODEL), F32)],
        compiler_params=_cparams("arbitrary", "arbitrary"),
        name="gather_norm_proj",
    )(pos, h, g.reshape(1, D_MODEL), w)


def _t5_bucket_np(rel):
    nb = REL_BUCKETS // 2
    max_exact = nb // 2
    ret = np.where(rel > 0, nb, 0)
    n = np.abs(rel)
    nf = np.maximum(n, 1).astype(np.float32)
    large = max_exact + (np.log(nf / np.float32(max_exact)) / np.float32(math.log(REL_MAX_DIST / max_exact))
                         * np.float32(nb - max_exact)).astype(np.int32)
    large = np.minimum(large, nb - 1)
    return (ret + np.where(n < max_exact, n, large)).astype(np.int32)


def _bias_kernel(table_ref, idx_ref, mask_ref, o_ref, *, block):
    idx = idx_ref[...]
    for h in range(REL_HEADS):
        acc = jnp.zeros(idx.shape, F32)
        for b in range(REL_BUCKETS):
            acc = jnp.where(idx == b, table_ref[b, h], acc)
        for v in range(3):
            o_ref[v, h * block:(h + 1) * block, :] = acc + mask_ref[v]


def _band_bias(table, block, half, dil):
    rel = np.arange(3 * block)[None, :] - block - np.arange(block)[:, None]
    band = np.abs(rel) <= half
    col = np.arange(3 * block)[None, :]
    masks = np.stack([band & (col >= block), band, band & (col < 2 * block)])
    add = np.where(masks, 0.0, NEG_INF).astype(np.float32)
    return pl.pallas_call(
        functools.partial(_bias_kernel, block=block),
        in_specs=[pl.BlockSpec(memory_space=pltpu.SMEM),
                  pl.BlockSpec(memory_space=pltpu.VMEM),
                  pl.BlockSpec(memory_space=pltpu.VMEM)],
        out_specs=pl.BlockSpec(memory_space=pltpu.VMEM),
        out_shape=jax.ShapeDtypeStruct((3, REL_HEADS * block, 3 * block), F32),
        name=f"rel_bias_d{dil}",
    )(table, jnp.asarray(_t5_bucket_np(rel * dil)), jnp.asarray(add))


def _even_mix_kernel(z_ref, kvp_ref, kvn_ref, h_ref, lng_ref, lnb_ref, ws_ref, bs_ref, bias_ref,
                     sink_ref, wout_ref, o_ref, kv_scr, y_scr):
    i = pl.program_id(0)
    nsub = EVEN_TB // B_BLOCK
    nblk = SEQ // B_BLOCK
    kv_scr[0:B_BLOCK] = kvp_ref[...]
    kv_scr[B_BLOCK:B_BLOCK + EVEN_TB] = z_ref[0, :, EVEN_K0:EVEN_IN]
    kv_scr[B_BLOCK + EVEN_TB:] = kvn_ref[...]
    lng = lng_ref[...]
    lnb = lnb_ref[...]
    for s0 in range(0, nsub, EVEN_SUBS):
        subs = []
        for s in range(s0, s0 + EVEN_SUBS):
            r0 = s * B_BLOCK
            gb = i * nsub + s
            subs.append((r0, jnp.where(gb == 0, 0, jnp.where(gb == nblk - 1, 2, 1))))
        vns = []
        for r0, _ in subs:
            va = z_ref[0, r0:r0 + A_CHUNK, A_WIDTH:2 * A_WIDTH].astype(F32)
            mu = jnp.mean(va, axis=-1, keepdims=True)
            vc = va - mu
            var = jnp.mean(vc * vc, axis=-1, keepdims=True)
            vns.append((vc * lax.rsqrt(var + EPS) * lng + lnb).astype(BF16))
        mixes = [[_dot(ws_ref[g], vn[:, g * A_CH:(g + 1) * A_CH]) for g in range(A_GROUPS)] for vn in vns]
        for (r0, _), mixed in zip(subs, mixes):
            for g in range(A_GROUPS):
                c0 = g * A_CH
                u = z_ref[0, r0:r0 + A_CHUNK, c0:c0 + A_CH].astype(F32)
                y_scr[r0:r0 + A_CHUNK, c0:c0 + A_CH] = (u * (mixed[g] + bs_ref[g])).astype(BF16)
        units = [(r0, sel, kh, kh * B_Q_PER_KV + half * EVEN_STACK)
                 for r0, sel in subs
                 for kh in range(B_KV_HEADS) for half in range(B_Q_PER_KV // EVEN_STACK)]
        lgs = []
        for r0, _, kh, hd0 in units:
            kw = kv_scr[r0:r0 + 3 * B_BLOCK, kh * B_HEAD_DIM:(kh + 1) * B_HEAD_DIM]
            q = jnp.concatenate(
                [z_ref[0, r0:r0 + B_BLOCK,
                       EVEN_Q0 + (hd0 + g) * B_HEAD_DIM:EVEN_Q0 + (hd0 + g + 1) * B_HEAD_DIM]
                 for g in range(EVEN_STACK)], axis=0)
            lgs.append(_dot_nt(q, kw))
        lgs = [lg * (B_HEAD_DIM ** -0.5) + bias_ref[sel, hd0 * B_BLOCK:(hd0 + EVEN_STACK) * B_BLOCK, :]
               for lg, (_, sel, _, hd0) in zip(lgs, units)]
        sks = [sink_ref[hd0 * B_BLOCK:(hd0 + EVEN_STACK) * B_BLOCK, :] for _, _, _, hd0 in units]
        ms = [jnp.maximum(jnp.max(lg, axis=-1, keepdims=True), sk) for lg, sk in zip(lgs, sks)]
        ps = [jnp.exp(lg - m) for lg, m in zip(lgs, ms)]
        dens = [jnp.sum(p, axis=-1, keepdims=True) + jnp.exp(sk - m) for p, sk, m in zip(ps, sks, ms)]
        os_ = []
        for p, (r0, _, kh, _) in zip(ps, units):
            vw = kv_scr[r0:r0 + 3 * B_BLOCK,
                        B_KV_WIDTH + kh * B_HEAD_DIM:B_KV_WIDTH + (kh + 1) * B_HEAD_DIM]
            os_.append(_dot(p.astype(BF16), vw))
        for o, den, (r0, _, _, hd0) in zip(os_, dens, units):
            o = o * (1.0 / den)
            for g in range(EVEN_STACK):
                c0 = A_WIDTH + (hd0 + g) * B_HEAD_DIM
                y_scr[r0:r0 + B_BLOCK, c0:c0 + B_HEAD_DIM] = o[g * B_BLOCK:(g + 1) * B_BLOCK].astype(BF16)
    o_ref[...] = h_ref[...] + _dot(y_scr[...], wout_ref[...])


def _even_mix(z, h, ln_g, ln_b, w_s, b_s, bias, sink, w_out):
    nsub = EVEN_TB // B_BLOCK
    nblk = SEQ // B_BLOCK
    kv_cb = EVEN_K0 // (2 * B_KV_WIDTH)
    sink_col = jnp.broadcast_to(sink.reshape(B_HEADS, 1, 1), (B_HEADS, B_BLOCK, 1)).reshape(
        B_HEADS * B_BLOCK, 1)
    return pl.pallas_call(
        _even_mix_kernel,
        grid=(SEQ // EVEN_TB,),
        in_specs=[
            pl.BlockSpec((1, EVEN_TB, EVEN_IN), lambda i: (0, i, 0)),
            pl.BlockSpec((None, B_BLOCK, 2 * B_KV_WIDTH),
                         lambda i: (0, jnp.maximum(i * nsub - 1, 0), kv_cb)),
            pl.BlockSpec((None, B_BLOCK, 2 * B_KV_WIDTH),
                         lambda i: (0, jnp.minimum((i + 1) * nsub, nblk - 1), kv_cb)),
            pl.BlockSpec((EVEN_TB, D_MODEL), lambda i: (i, 0)),
            pl.BlockSpec((1, A_WIDTH), lambda i: (0, 0)),
            pl.BlockSpec((1, A_WIDTH), lambda i: (0, 0)),
            pl.BlockSpec((A_GROUPS, A_CHUNK, A_CHUNK), lambda i: (0, 0, 0)),
            pl.BlockSpec((A_GROUPS, A_CHUNK, A_CH), lambda i: (0, 0, 0)),
            pl.BlockSpec((3, B_HEADS * B_BLOCK, 3 * B_BLOCK), lambda i: (0, 0, 0)),
            pl.BlockSpec((B_HEADS * B_BLOCK, 1), lambda i: (0, 0)),
            pl.BlockSpec((A_WIDTH + B_WIDTH, D_MODEL), lambda i: (0, 0)),
        ],
        out_specs=pl.BlockSpec((EVEN_TB, D_MODEL), lambda i: (i, 0)),
        out_shape=jax.ShapeDtypeStruct((SEQ, D_MODEL), F32),
        scratch_shapes=[
            pltpu.VMEM((EVEN_TB + 2 * B_BLOCK, 2 * B_KV_WIDTH), BF16),
            pltpu.VMEM((EVEN_TB, A_WIDTH + B_WIDTH), BF16),
        ],
        compiler_params=_cparams("parallel"),
        name="even_mixer",
    )(z, z, z, h, ln_g.reshape(1, A_WIDTH), ln_b.reshape(1, A_WIDTH), w_s.astype(BF16),
      jnp.broadcast_to(b_s[:, :, None], (A_GROUPS, A_CHUNK, A_CH)), bias, sink_col, w_out)


def _dil_attn_kernel(zc_ref, kp_ref, kn_ref, vp_ref, vn_ref, bias_ref, o_ref, lse_ref,
                     k_scr, v_scr, o_scr, lse_scr, *, dil):
    t = pl.program_id(0)
    seg = ODD_TILE // dil
    nsb = seg // C_BLOCK
    nblk = SEQ // dil // C_BLOCK
    k_scr[:, 0:C_BLOCK] = kp_ref[...]
    k_scr[:, C_BLOCK:C_BLOCK + seg] = zc_ref[:, :, C_WIDTH:2 * C_WIDTH]
    k_scr[:, C_BLOCK + seg:] = kn_ref[...]
    v_scr[:, 0:C_BLOCK] = vp_ref[...]
    v_scr[:, C_BLOCK:C_BLOCK + seg] = zc_ref[:, :, 2 * C_WIDTH:3 * C_WIDTH]
    v_scr[:, C_BLOCK + seg:] = vn_ref[...]
    lane = lax.broadcasted_iota(jnp.int32, (C_BLOCK, V7X_LANES), 1)

    def body(it, carry):
        blocks = []
        for u in range(ODD_INTERLEAVE):
            n = it * ODD_INTERLEAVE + u
            r = n // nsb
            s = n % nsb
            r0 = pl.multiple_of(s * C_BLOCK, C_BLOCK)
            gb = t * nsb + s
            sel = jnp.where(gb == 0, 0, jnp.where(gb == nblk - 1, 2, 1))
            rows = (pl.ds(s * (C_BLOCK * dil) + r, C_BLOCK, stride=dil) if dil > 1
                    else pl.ds(r0, C_BLOCK))
            blocks.append((r, r0, sel, rows))
        lgs = []
        for r, r0, _, _ in blocks:
            for hd in range(C_HEADS):
                c0 = hd * C_HEAD_DIM
                q = zc_ref[r, pl.ds(r0, C_BLOCK), c0:c0 + C_HEAD_DIM]
                kw = k_scr[r, pl.ds(r0, 3 * C_BLOCK), c0:c0 + C_HEAD_DIM]
                lgs.append(_dot_nt(q, kw))
        lg = [jnp.concatenate(lgs[u * C_HEADS:(u + 1) * C_HEADS], axis=0) * (C_HEAD_DIM ** -0.5)
              + bias_ref[blk[2]] for u, blk in enumerate(blocks)]
        m = [jnp.max(x, axis=-1, keepdims=True) for x in lg]
        p = [jnp.exp(x - mm) for x, mm in zip(lg, m)]
        den = [jnp.sum(x, axis=-1, keepdims=True) for x in p]
        inv = [1.0 / d for d in den]
        lse = [mm + jnp.log(d) for mm, d in zip(m, den)]
        pb = [x.astype(BF16) for x in p]
        outs = []
        for u, (r, r0, _, _) in enumerate(blocks):
            for hd in range(C_HEADS):
                c0 = hd * C_HEAD_DIM
                vw = v_scr[r, pl.ds(r0, 3 * C_BLOCK), c0:c0 + C_HEAD_DIM]
                outs.append(_dot(pb[u][hd * C_BLOCK:(hd + 1) * C_BLOCK], vw))
        for u, (_, _, _, rows) in enumerate(blocks):
            lse_tile = jnp.zeros((C_BLOCK, V7X_LANES), F32)
            for hd in range(C_HEADS):
                o_scr[hd, rows, :] = outs[u * C_HEADS + hd] * inv[u][hd * C_BLOCK:(hd + 1) * C_BLOCK]
                lse_tile = jnp.where(lane == hd, lse[u][hd * C_BLOCK:(hd + 1) * C_BLOCK], lse_tile)
            lse_scr[rows, :] = lse_tile
        return carry

    lax.fori_loop(0, ODD_BLOCKS // ODD_INTERLEAVE, body, 0)
    for hd in range(C_HEADS):
        o_ref[:, hd * C_HEAD_DIM:(hd + 1) * C_HEAD_DIM] = o_scr[hd].astype(o_ref.dtype)
    lse_ref[...] = lse_scr[...]


def _dil_attn(zg, bias, dil):
    seg = ODD_TILE // dil
    nsb = seg // C_BLOCK
    last = SEQ // dil // C_BLOCK - 1

    def halo(j, nxt):
        if nxt:
            return pl.BlockSpec((dil, C_BLOCK, C_WIDTH),
                                lambda t: (0, jnp.minimum((t + 1) * nsb, last), j))
        return pl.BlockSpec((dil, C_BLOCK, C_WIDTH), lambda t: (0, jnp.maximum(t * nsb - 1, 0), j))

    return pl.pallas_call(
        functools.partial(_dil_attn_kernel, dil=dil),
        grid=(SEQ // ODD_TILE,),
        in_specs=[pl.BlockSpec((dil, seg, 3 * C_WIDTH), lambda t: (0, t, 0)),
                  halo(1, False), halo(1, True), halo(2, False), halo(2, True),
                  pl.BlockSpec((3, C_HEADS * C_BLOCK, 3 * C_BLOCK), lambda t: (0, 0, 0))],
        out_specs=[pl.BlockSpec((ODD_TILE, C_WIDTH), lambda t: (t, 0)),
                   pl.BlockSpec((ODD_TILE, V7X_LANES), lambda t: (t, 0))],
        out_shape=[jax.ShapeDtypeStruct((SEQ, C_WIDTH), BF16),
                   jax.ShapeDtypeStruct((SEQ, V7X_LANES), F32)],
        scratch_shapes=[pltpu.VMEM((dil, seg + 2 * C_BLOCK, C_WIDTH), BF16),
                        pltpu.VMEM((dil, seg + 2 * C_BLOCK, C_WIDTH), BF16),
                        pltpu.VMEM((C_HEADS, ODD_TILE, C_HEAD_DIM), F32),
                        pltpu.VMEM((ODD_TILE, V7X_LANES), F32)],
        compiler_params=_cparams("parallel"),
        name=f"dilated_attn_d{dil}",
    )(zg, zg, zg, zg, zg, bias)


def _combine_kernel(o0_ref, o1_ref, o2_ref, l0_ref, l1_ref, l2_ref, h_ref, wout_ref, out_ref, y_scr):
    l0 = l0_ref[...]
    l1 = l1_ref[...]
    l2 = l2_ref[...]
    m = jnp.maximum(jnp.maximum(l0, l1), l2)
    e0 = jnp.exp(l0 - m)
    e1 = jnp.exp(l1 - m)
    e2 = jnp.exp(l2 - m)
    tot = e0 + e1 + e2
    w0 = e0 / tot
    w1 = e1 / tot
    w2 = e2 / tot
    for hd in range(C_HEADS):
        c0 = hd * C_HEAD_DIM
        y = (w0[:, hd:hd + 1] * o0_ref[:, c0:c0 + C_HEAD_DIM].astype(F32)
             + w1[:, hd:hd + 1] * o1_ref[:, c0:c0 + C_HEAD_DIM].astype(F32)
             + w2[:, hd:hd + 1] * o2_ref[:, c0:c0 + C_HEAD_DIM].astype(F32))
        y_scr[:, c0:c0 + C_HEAD_DIM] = y.astype(BF16)
    out_ref[...] = h_ref[...] + _dot(y_scr[...], wout_ref[...])


def _combine(outs, lses, h, w_out):
    blk_o = pl.BlockSpec((COMB_TB, C_WIDTH), lambda i: (i, 0))
    blk_l = pl.BlockSpec((COMB_TB, V7X_LANES), lambda i: (i, 0))
    return pl.pallas_call(
        _combine_kernel,
        grid=(SEQ // COMB_TB,),
        in_specs=[blk_o, blk_o, blk_o, blk_l, blk_l, blk_l,
                  pl.BlockSpec((COMB_TB, D_MODEL), lambda i: (i, 0)),
                  pl.BlockSpec((C_WIDTH, D_MODEL), lambda i: (0, 0))],
        out_specs=pl.BlockSpec((COMB_TB, D_MODEL), lambda i: (i, 0)),
        out_shape=jax.ShapeDtypeStruct((SEQ, D_MODEL), F32),
        scratch_shapes=[pltpu.VMEM((COMB_TB, C_WIDTH), BF16)],
        compiler_params=_cparams("parallel"),
        name="group_combine_proj",
    )(*outs, *lses, h, w_out)


def _cross_kernel(h_ref, gx_ref, wq_ref, kv_ref, wo_ref, gf_ref, wr_ref, br_ref,
                  hx_ref, meta_ref, o_scr):
    h = h_ref[...]
    q = _dot(_rms(h, gx_ref[...]).astype(BF16), wq_ref[...]).astype(BF16)
    cols = [hd * X_HEAD_DIM for hd in range(X_HEADS)]
    lgs = [_dot_nt(q[:, c0:c0 + X_HEAD_DIM], kv_ref[0, :, c0:c0 + X_HEAD_DIM]) * (X_HEAD_DIM ** -0.5)
           for c0 in cols]
    ms = [jnp.max(lg, axis=-1, keepdims=True) for lg in lgs]
    ps = [jnp.exp(lg - m) for lg, m in zip(lgs, ms)]
    dens = [jnp.sum(p, axis=-1, keepdims=True) for p in ps]
    os_ = [_dot(p.astype(BF16), kv_ref[0, :, X_WIDTH + c0:X_WIDTH + c0 + X_HEAD_DIM])
           for p, c0 in zip(ps, cols)]
    for o, den, c0 in zip(os_, dens, cols):
        o_scr[:, c0:c0 + X_HEAD_DIM] = (o / den).astype(BF16)
    h2 = h + _dot(o_scr[...], wo_ref[...])
    _store_rows(hx_ref, h2)

    t = _rms(h2, gf_ref[...])
    t_hi = t.astype(BF16)
    t_lo = (t - t_hi.astype(F32)).astype(BF16)
    lt = (_dot_nt(wr_ref[0], t_hi) + _dot_nt(wr_ref[0], t_lo) + _dot_nt(wr_ref[1], t_hi)) + br_ref[...]
    g = [lt[k:k + 1, :] for k in range(MOE_GROUPS)]
    gmax = jnp.maximum(jnp.maximum(g[0], g[1]), jnp.maximum(g[2], g[3]))
    grp = jnp.where(g[0] == gmax, 0, jnp.where(g[1] == gmax, 1, jnp.where(g[2] == gmax, 2, 3)))
    g_gate = 1.0 / (jnp.exp(g[0] - gmax) + jnp.exp(g[1] - gmax) + jnp.exp(g[2] - gmax)
                    + jnp.exp(g[3] - gmax))
    e = []
    for k in range(MOE_EPG):
        rows = [lt[MOE_GROUPS + gi * MOE_EPG + k:MOE_GROUPS + gi * MOE_EPG + k + 1, :]
                for gi in range(MOE_GROUPS)]
        e.append(jnp.where(grp == 0, rows[0], jnp.where(grp == 1, rows[1],
                                                         jnp.where(grp == 2, rows[2], rows[3]))))
    v1 = jnp.maximum(jnp.maximum(e[0], e[1]), jnp.maximum(e[2], e[3]))
    i1 = jnp.where(e[0] == v1, 0, jnp.where(e[1] == v1, 1, jnp.where(e[2] == v1, 2, 3)))
    r = [jnp.where(i1 == k, -jnp.inf, e[k]) for k in range(MOE_EPG)]
    v2 = jnp.maximum(jnp.maximum(r[0], r[1]), jnp.maximum(r[2], r[3]))
    i2 = jnp.where(r[0] == v2, 0, jnp.where(r[1] == v2, 1, jnp.where(r[2] == v2, 2, 3)))
    d = jnp.exp(v2 - v1)
    w1 = g_gate / (1.0 + d)
    w2 = g_gate * d / (1.0 + d)
    first_lo = i1 < i2
    lo = jnp.where(first_lo, i1, i2)
    hi = jnp.where(first_lo, i2, i1)
    w_lo = jnp.where(first_lo, w1, w2)
    w_hi = jnp.where(first_lo, w2, w1)
    pair = jnp.where(lo == 0, hi - 1, jnp.where(lo == 1, jnp.where(hi == 3, 3, 4), 5))
    w_a = jnp.where(lo == 2, w_hi, w_lo)
    w_b = jnp.where(lo == 2, w_lo, w_hi)
    bucket = (grp * N_PAIRS + pair).astype(F32)
    row = lax.broadcasted_iota(jnp.int32, (8, CROSS_TB), 0)
    meta_ref[...] = jnp.where(row == 0, bucket, jnp.where(row == 1, w_a, jnp.where(row == 2, w_b, 0.0)))


def _cross_router(h, g_cross, wq, kv, wo, g_ffn, wr_t, br):
    full = lambda shape: pl.BlockSpec(shape, lambda i: tuple(0 for _ in shape))
    return pl.pallas_call(
        _cross_kernel,
        grid=(SEQ // CROSS_TB,),
        in_specs=[
            pl.BlockSpec((CROSS_TB, D_MODEL), lambda i: (i, 0)),
            full((1, D_MODEL)),
            full((D_MODEL, X_WIDTH)),
            full((1, MEM_LEN, 2 * X_WIDTH)),
            full((X_WIDTH, D_MODEL)),
            full((1, D_MODEL)),
            full((2, ROUTER_ROWS, D_MODEL)),
            full((ROUTER_ROWS, 1)),
        ],
        out_specs=[pl.BlockSpec((CROSS_TB * ROW_CHUNKS, V7X_LANES), lambda i: (i, 0)),
                   pl.BlockSpec((8, CROSS_TB), lambda i: (0, i))],
        out_shape=[jax.ShapeDtypeStruct((SEQ * ROW_CHUNKS, V7X_LANES), F32),
                   jax.ShapeDtypeStruct((8, SEQ), F32)],
        scratch_shapes=[pltpu.VMEM((CROSS_TB, X_WIDTH), BF16)],
        compiler_params=_cparams("parallel"),
        name="cross_attn_router",
    )(h, g_cross.reshape(1, D_MODEL), wq, kv, wo, g_ffn.reshape(1, D_MODEL), wr_t, br)


def _moe_kernel(src_ref, ea_ref, eb_ref, nused_ref,
                hx_hbm, gates_ref, gf_ref, wga_ref, wua_ref, wda_ref, wgb_ref, wub_ref, wdb_ref,
                out_ref, xbuf, wup_a, wdn_a, wup_b, wdn_b, gsem):
    k = pl.program_id(0)
    nused = nused_ref[0]
    xslot = k % 3
    gather = _RowGather(src_ref, hx_hbm, xbuf, gsem, MOE_TM)

    @pl.when(k == 0)
    def _():
        gather.start(0, 0, range(MOE_TM))
        gather.start(1, 1, range(MOE_TM))

    prev = jnp.maximum(k - 1, 0)

    @pl.when((k < nused) & ((k == 0) | (ea_ref[k] != ea_ref[prev])))
    def _():
        wup_a[:, :D_EXPERT] = wga_ref[0].astype(BF16)
        wup_a[:, D_EXPERT:] = wua_ref[0].astype(BF16)
        wdn_a[...] = wda_ref[0].astype(BF16)

    @pl.when((k < nused) & ((k == 0) | (eb_ref[k] != eb_ref[prev])))
    def _():
        wup_b[:, :D_EXPERT] = wgb_ref[0].astype(BF16)
        wup_b[:, D_EXPERT:] = wub_ref[0].astype(BF16)
        wdn_b[...] = wdb_ref[0].astype(BF16)

    @pl.when(k < nused)
    def _():
        gather.wait(xslot)
        h2 = _load_rows(xbuf, MOE_TM, (xslot,))
        gather.start(k + 2, (k + 2) % 3, range(MOE_TM))
        t = _rms(h2, gf_ref[...]).astype(BF16)
        gus = [_dot(t, wup[...]) for wup in (wup_a, wup_b)]
        hids = [(jax.nn.silu(gu[:, :D_EXPERT]) * gu[:, D_EXPERT:] * gates_ref[:, col:col + 1]).astype(BF16)
                for col, gu in enumerate(gus)]
        ys = [_dot(hid, wdn[...]) for hid, wdn in zip(hids, (wdn_a, wdn_b))]
        _store_rows(out_ref, h2 + (ys[0] + ys[1]))

    @pl.when(k >= nused)
    def _():
        out_ref[...] = jnp.zeros(out_ref.shape, F32)

    @pl.when(k == nused - 1)
    def _():
        gather.wait((k + 1) % 3)
        gather.wait((k + 2) % 3)


def _moe(hx, gates, g_ffn, w_gate, w_up, w_down, src, ea, eb, nused):
    def wspec(shape, which):
        if which == 0:
            return pl.BlockSpec((1,) + shape, lambda k, s, a, b, n: (a[k], 0, 0))
        return pl.BlockSpec((1,) + shape, lambda k, s, a, b, n: (b[k], 0, 0))

    up_shape = (D_MODEL, D_EXPERT)
    down_shape = (D_EXPERT, D_MODEL)
    grid_spec = pltpu.PrefetchScalarGridSpec(
        num_scalar_prefetch=4,
        grid=(MOE_TILES,),
        in_specs=[
            pl.BlockSpec(memory_space=pl.ANY),
            pl.BlockSpec((MOE_TM, V7X_LANES), lambda k, s, a, b, n: (k, 0)),
            pl.BlockSpec((1, D_MODEL), lambda k, s, a, b, n: (0, 0)),
            wspec(up_shape, 0), wspec(up_shape, 0), wspec(down_shape, 0),
            wspec(up_shape, 1), wspec(up_shape, 1), wspec(down_shape, 1),
        ],
        out_specs=pl.BlockSpec((MOE_TM * ROW_CHUNKS, V7X_LANES), lambda k, s, a, b, n: (k, 0)),
        scratch_shapes=[
            pltpu.VMEM((3, MOE_TM * ROW_CHUNKS, V7X_LANES), F32),
            pltpu.VMEM((D_MODEL, 2 * D_EXPERT), BF16),
            pltpu.VMEM((D_EXPERT, D_MODEL), BF16),
            pltpu.VMEM((D_MODEL, 2 * D_EXPERT), BF16),
            pltpu.VMEM((D_EXPERT, D_MODEL), BF16),
            pltpu.SemaphoreType.DMA((3,)),
        ],
    )
    return pl.pallas_call(
        _moe_kernel,
        grid_spec=grid_spec,
        out_shape=jax.ShapeDtypeStruct((MOE_TILES * MOE_TM * ROW_CHUNKS, V7X_LANES), F32),
        compiler_params=_cparams("arbitrary"),
        name="routed_moe",
    )(src, ea, eb, nused, hx, gates, g_ffn.reshape(1, D_MODEL),
      w_gate, w_up, w_down, w_gate, w_up, w_down)


def _route_tables(meta):
    bucket = meta[0].astype(jnp.int32)
    ids = jnp.arange(N_BUCKETS, dtype=jnp.int32)
    counts = jnp.sum((bucket[:, None] == ids[None, :]).astype(jnp.int32), axis=0)
    ntile = (counts + MOE_TM - 1) // MOE_TM
    pad = ntile * MOE_TM - counts
    tile_end = jnp.cumsum(ntile)
    nused = tile_end[-1]
    dummy_key = jnp.where(jnp.arange(MOE_TM - 1, dtype=jnp.int32)[None, :] < pad[:, None],
                          ids[:, None], N_BUCKETS)
    keys = jnp.concatenate([bucket, dummy_key.reshape(-1)])
    vals = jnp.concatenate([jnp.arange(SEQ, dtype=jnp.int32),
                            jnp.full((N_BUCKETS * (MOE_TM - 1),), SEQ, jnp.int32)])
    nslot = MOE_TILES * MOE_TM
    zpad = jnp.zeros((N_BUCKETS * (MOE_TM - 1),), F32)
    _, tok, ga, gb = lax.sort((keys, vals, jnp.concatenate([meta[1], zpad]), jnp.concatenate([meta[2], zpad])),
                              num_keys=1, is_stable=True)
    tok = tok[:nslot]
    gates = jnp.pad(jnp.stack([ga[:nslot], gb[:nslot]], axis=1), ((0, 0), (0, V7X_LANES - 2)))
    valid = tok < SEQ
    src = jnp.concatenate([jnp.where(valid, tok, 0), jnp.zeros((2 * MOE_TM,), jnp.int32)])
    _, pos = lax.sort((tok, jnp.arange(nslot, dtype=jnp.int32)), num_keys=1, is_stable=True)
    pos = pos[:SEQ]
    tiles = jnp.arange(MOE_TILES, dtype=jnp.int32)
    tile_bucket = jnp.minimum(jnp.sum((tiles[:, None] >= tile_end[None, :]).astype(jnp.int32), axis=1),
                              N_BUCKETS - 1)
    onehot = (tile_bucket[:, None] == ids[None, :]).astype(jnp.int32)
    base = (np.arange(N_BUCKETS) // N_PAIRS) * MOE_EPG
    ea = jnp.sum(onehot * jnp.asarray(base + np.asarray(SLOT_A)[np.arange(N_BUCKETS) % N_PAIRS],
                                      jnp.int32)[None, :], axis=1)
    eb = jnp.sum(onehot * jnp.asarray(base + np.asarray(SLOT_B)[np.arange(N_BUCKETS) % N_PAIRS],
                                      jnp.int32)[None, :], axis=1)
    return (src.astype(jnp.int32), pos.astype(jnp.int32), ea.astype(jnp.int32), eb.astype(jnp.int32),
            nused.reshape(1).astype(jnp.int32), gates)


def _final_norm_kernel(pos_ref, hs_hbm, g_ref, o_ref, hbuf, sem, *, ni):
    i = pl.program_id(0)
    tm = o_ref.shape[0]
    slot = i % 2
    gather = _RowGather(pos_ref, hs_hbm, hbuf, sem, tm, both_queues=True)

    @pl.when(i == 0)
    def _():
        gather.start(0, 0, range(tm))

    @pl.when(i + 1 < ni)
    def _():
        gather.start(i + 1, 1 - slot, range(tm))

    gather.wait(slot)
    o_ref[...] = _rms(_load_rows(hbuf, tm, (slot,)), g_ref[...])


def _final_norm(hs, pos, g):
    tb = FINAL_TB
    ni = SEQ // tb
    grid_spec = pltpu.PrefetchScalarGridSpec(
        num_scalar_prefetch=1,
        grid=(ni,),
        in_specs=[pl.BlockSpec(memory_space=pl.ANY),
                  pl.BlockSpec((1, D_MODEL), lambda i, p: (0, 0))],
        out_specs=pl.BlockSpec((tb, D_MODEL), lambda i, p: (i, 0)),
        scratch_shapes=[pltpu.VMEM((2, tb * ROW_CHUNKS, V7X_LANES), F32),
                        pltpu.SemaphoreType.DMA((2,))],
    )
    return pl.pallas_call(
        functools.partial(_final_norm_kernel, ni=ni),
        grid_spec=grid_spec,
        out_shape=jax.ShapeDtypeStruct((SEQ, D_MODEL), F32),
        compiler_params=_cparams("arbitrary"),
        name="final_norm",
    )(pos, hs, g.reshape(1, D_MODEL))


def kernel(x, mem, ln_mix, ln_cross, ln_mem, ln_ffn, ln_final, rel_table, even_w_in, even_w_out,
           sgu_ln_g, sgu_ln_b, sgu_w, sgu_b, attn_sink, odd_w_in, odd_w_out, xq_w, xkv_w, xo_w,
           router_group_w, router_group_b, router_expert_w, router_expert_b,
           expert_w_gate, expert_w_up, expert_w_down):
    h = x.reshape(SEQ, D_MODEL)
    mem2 = mem.reshape(MEM_LEN, D_MODEL)
    bias_even = _band_bias(rel_table, B_BLOCK, B_HALF_WINDOW, 1)
    bias_odd = [_band_bias(rel_table, C_BLOCK, window // 2 // dil, dil) for window, dil in C_PAIRS]

    pos = None
    for layer in range(DEPTH):
        i = layer // 2
        if layer % 2 == 0:
            z = _proj(h, ln_mix[layer], even_w_in, w_lead=i, rows=SEQ, tm=EVEN_TB, tn=EVEN_IN, n=EVEN_IN,
                      gelu_cols=2 * A_WIDTH, pos=pos)
            if pos is not None:
                z, h = z
            h = _even_mix(z, h, sgu_ln_g[i], sgu_ln_b[i], sgu_w[i], sgu_b[i], bias_even,
                          attn_sink[i], even_w_out[i].astype(BF16))
        else:
            zg, h = _proj(h, ln_mix[layer], odd_w_in, w_lead=i, rows=SEQ, tm=PROJ_TM, tn=GATHER_TN,
                          n=3 * C_WIDTH, pos=pos)
            outs, lses = [], []
            for gi, (_, dil) in enumerate(C_PAIRS):
                if gi > 0:
                    zg = _proj(h, ln_mix[layer], odd_w_in, w_lead=i, w_col0=gi * 3 * C_WIDTH // PROJ_TN,
                               rows=SEQ, tm=PROJ_TM, tn=PROJ_TN, n=3 * C_WIDTH, dil=dil)
                o, lse = _dil_attn(zg, bias_odd[gi], dil)
                outs.append(o)
                lses.append(lse)
            h = _combine(outs, lses, h, odd_w_out[i].astype(BF16))

        kv = _proj(mem2, ln_mem[layer], xkv_w, w_lead=layer, rows=MEM_LEN, tm=MEM_LEN,
                   tn=2 * X_WIDTH, n=2 * X_WIDTH)
        wr_t = jnp.zeros((ROUTER_ROWS, D_MODEL), F32)
        wr_t = wr_t.at[:MOE_GROUPS].set(router_group_w[layer].T)
        wr_t = wr_t.at[MOE_GROUPS:MOE_GROUPS + N_EXPERTS].set(
            router_expert_w[layer].reshape(D_MODEL, N_EXPERTS).T)
        br = jnp.zeros((ROUTER_ROWS, 1), F32)
        br = br.at[:MOE_GROUPS, 0].set(router_group_b[layer])
        br = br.at[MOE_GROUPS:MOE_GROUPS + N_EXPERTS, 0].set(router_expert_b[layer].reshape(N_EXPERTS))
        wr_hi = wr_t.astype(BF16)
        wr_split = jnp.stack([wr_hi, (wr_t - wr_hi.astype(F32)).astype(BF16)])
        hx, meta = _cross_router(h, ln_cross[layer], xq_w[layer].astype(BF16), kv,
                                 xo_w[layer].astype(BF16), ln_ffn[layer], wr_split, br)

        src, pos, ea, eb, nused, gates = _route_tables(meta)
        h = _moe(hx, gates, ln_ffn[layer],
                 expert_w_gate.reshape(DEPTH * N_EXPERTS, D_MODEL, D_EXPERT),
                 expert_w_up.reshape(DEPTH * N_EXPERTS, D_MODEL, D_EXPERT),
                 expert_w_down.reshape(DEPTH * N_EXPERTS, D_EXPERT, D_MODEL),
                 src, ea + layer * N_EXPERTS, eb + layer * N_EXPERTS, nused)

    return _final_norm(h, pos, ln_final).reshape(1, SEQ, D_MODEL)
```

```python
import functools
import math

import numpy as np
import jax
import jax.numpy as jnp
from jax import lax
from jax.experimental import pallas as pl
from jax.experimental.pallas import tpu as pltpu

F32 = jnp.float32
BF16 = jnp.bfloat16

D_MODEL = 1024
SEQ = 16384
DEPTH = 4
MEM_LEN = 256
EPS = 1e-6
NEG_INF = -1e30

A_GROUPS = 4
A_CH = 128
A_WIDTH = A_GROUPS * A_CH
A_CHUNK = 128
B_HEADS = 8
B_KV_HEADS = 2
B_Q_PER_KV = B_HEADS // B_KV_HEADS
B_HEAD_DIM = 64
B_WIDTH = B_HEADS * B_HEAD_DIM
B_KV_WIDTH = B_KV_HEADS * B_HEAD_DIM
B_HALF_WINDOW = 128
B_BLOCK = 128
EVEN_IN = 2 * A_WIDTH + B_WIDTH + 2 * B_KV_WIDTH
EVEN_Q0 = 2 * A_WIDTH
EVEN_K0 = EVEN_Q0 + B_WIDTH
EVEN_V0 = EVEN_K0 + B_KV_WIDTH

C_PAIRS = ((128, 1), (512, 4), (2048, 16))
C_GROUPS = len(C_PAIRS)
C_HEADS = 8
C_HEAD_DIM = 128
C_WIDTH = C_HEADS * C_HEAD_DIM
C_BLOCK = 64
ODD_IN = C_GROUPS * 3 * C_WIDTH

REL_BUCKETS = 32
REL_MAX_DIST = 1024
REL_HEADS = 8

X_HEADS = 4
X_HEAD_DIM = 128
X_WIDTH = X_HEADS * X_HEAD_DIM

MOE_GROUPS = 4
MOE_EPG = 4
N_EXPERTS = MOE_GROUPS * MOE_EPG
D_EXPERT = 512
SLOT_A = (0, 0, 0, 1, 1, 3)
SLOT_B = (1, 2, 3, 3, 2, 2)
N_PAIRS = len(SLOT_A)
N_BUCKETS = MOE_GROUPS * N_PAIRS

V7X_LANES = 128
ROW_CHUNKS = D_MODEL // V7X_LANES
STRIDE_STEP = 4
V7X_VMEM_BYTES = 64 * 1024 * 1024
VMEM_LIMIT = 62 * 1024 * 1024

PROJ_TM = 1024
PROJ_TN = 1024
EVEN_TB = 512
EVEN_STACK = 2
EVEN_SUBS = 4
ODD_TILE = PROJ_TM
ODD_BLOCKS = ODD_TILE // C_BLOCK
ODD_INTERLEAVE = 8
COMB_TB = 512
CROSS_TB = 512
MOE_TM = 256
ROUTER_ROWS = 32
MOE_TILES = (SEQ + N_BUCKETS * (MOE_TM - 1)) // MOE_TM
GATHER_TN = 1536
FINAL_TB = 512


def _cparams(*sem):
    return pltpu.CompilerParams(dimension_semantics=sem, vmem_limit_bytes=VMEM_LIMIT)


def _rms(x, g):
    return x * lax.rsqrt(jnp.mean(x * x, axis=-1, keepdims=True) + EPS) * g


def _dot(a, b):
    return jnp.dot(a, b, preferred_element_type=F32)


def _dot_nt(a, b):
    return lax.dot_general(a, b, (((1,), (1,)), ((), ())), preferred_element_type=F32)


def _load_rows(ref, n, lead=()):
    return jnp.concatenate([ref[lead + (pl.ds(c, n, stride=ROW_CHUNKS), slice(None))]
                            for c in range(ROW_CHUNKS)], axis=1)


def _store_rows(ref, val, lead=()):
    n = val.shape[0]
    for c in range(ROW_CHUNKS):
        ref[lead + (pl.ds(c, n, stride=ROW_CHUNKS), slice(None))] = val[:, c * V7X_LANES:(c + 1) * V7X_LANES]


def _row_tile(idx):
    if isinstance(idx, int):
        return pl.ds(idx * ROW_CHUNKS, ROW_CHUNKS)
    return pl.ds(pl.multiple_of(idx * ROW_CHUNKS, ROW_CHUNKS), ROW_CHUNKS)


class _RowGather:
    def __init__(self, idx_ref, src_hbm, buf, sem, tm, both_queues=False):
        self.idx_ref, self.src, self.buf, self.sem, self.tm = idx_ref, src_hbm, buf, sem, tm
        self.both_queues = both_queues

    def start(self, tile, slot, rows):
        for n, r in enumerate(rows):
            pltpu.make_async_copy(self.src.at[_row_tile(self.idx_ref[tile * self.tm + r])],
                                  self.buf.at[slot, _row_tile(r)], self.sem.at[slot]).start(
                                      priority=n % 2 if self.both_queues else 0)

    def wait(self, slot):
        pltpu.make_async_copy(self.src.at[pl.ds(0, self.tm * ROW_CHUNKS)], self.buf.at[slot],
                              self.sem.at[slot]).wait()


def _fill_xn(xf, xn_ref, scratch, dil):
    tm = xn_ref.shape[0]
    seg = tm // dil
    if dil == 1:
        xn_ref[...] = xf.astype(BF16)
        return
    xs_ref, ys_ref = scratch
    for c in range(ROW_CHUNKS):
        xs_ref[c] = xf[:, c * V7X_LANES:(c + 1) * V7X_LANES]
    src = xs_ref
    if dil == STRIDE_STEP * STRIDE_STEP:
        quarter = tm // STRIDE_STEP
        for q in range(STRIDE_STEP):
            for c in range(ROW_CHUNKS):
                ys_ref[c, q * quarter:(q + 1) * quarter, :] = xs_ref[c, pl.ds(q, quarter, stride=STRIDE_STEP), :]
        src = ys_ref
    else:
        assert dil == STRIDE_STEP
    for sl in range(dil):
        start = (sl // STRIDE_STEP) * (tm // STRIDE_STEP) + sl % STRIDE_STEP if dil > STRIDE_STEP else sl
        for c in range(ROW_CHUNKS):
            xn_ref[sl * seg:(sl + 1) * seg, c * V7X_LANES:(c + 1) * V7X_LANES] = (
                src[c, pl.ds(start, seg, stride=STRIDE_STEP), :].astype(BF16))


def _slab_residue(sl, dil):
    if dil == STRIDE_STEP * STRIDE_STEP:
        return (sl % STRIDE_STEP) * STRIDE_STEP + sl // STRIDE_STEP
    return sl


def _proj_out(acc, o_ref, gelu_cols, dil):
    seg = acc.shape[0] // dil
    if gelu_cols:
        o_ref[0, :, :gelu_cols] = jax.nn.gelu(acc[:, :gelu_cols]).astype(o_ref.dtype)
        o_ref[0, :, gelu_cols:] = acc[:, gelu_cols:].astype(o_ref.dtype)
    else:
        for r in range(dil):
            o_ref[r] = acc[r * seg:(r + 1) * seg].astype(o_ref.dtype)


def _proj_kernel(h_ref, g_ref, w_ref, o_ref, xn_ref, *scratch, gelu_cols, dil):
    @pl.when(pl.program_id(1) == 0)
    def _():
        _fill_xn(_rms(h_ref[...], g_ref[...]), xn_ref, scratch, dil)

    _proj_out(_dot(xn_ref[...], w_ref[...].astype(BF16)), o_ref, gelu_cols, dil)


def _gather_proj_kernel(pos_ref, hs_hbm, g_ref, w_ref, o_ref, hnat_ref, xn_ref, hbuf, sem,
                        *, gelu_cols, ni, nj):
    i = pl.program_id(0)
    j = pl.program_id(1)
    tm = xn_ref.shape[0]
    per = tm // nj
    slot = i % 2
    gather = _RowGather(pos_ref, hs_hbm, hbuf, sem, tm)

    @pl.when((i == 0) & (j == 0))
    def _():
        gather.start(0, 0, range(tm))

    @pl.when(j == 0)
    def _():
        gather.wait(slot)
        h = _load_rows(hbuf, tm, (slot,))
        hnat_ref[...] = h
        _fill_xn(_rms(h, g_ref[...]), xn_ref, (), 1)

    nxt = jnp.where(i + 1 < ni, i + 1, 0)
    gather.start(nxt, 1 - slot, [j * per + r for r in range(per)])
    _proj_out(_dot(xn_ref[...], w_ref[...].astype(BF16)), o_ref, gelu_cols, 1)

    @pl.when((i == ni - 1) & (j == nj - 1))
    def _():
        gather.wait(1 - slot)


def _proj(h, g, w, *, rows, tm, tn, n, w_lead=0, w_col0=0, gelu_cols=0, dil=1, pos=None):
    seg = tm // dil
    ni, nj = rows // tm, n // tn
    out_z = jax.ShapeDtypeStruct((dil, rows // dil, n), BF16)
    if pos is None:
        return pl.pallas_call(
            functools.partial(_proj_kernel, gelu_cols=gelu_cols, dil=dil),
            grid=(ni, nj),
            in_specs=[
                pl.BlockSpec((tm, D_MODEL), lambda i, j: (i, 0)),
                pl.BlockSpec((1, D_MODEL), lambda i, j: (0, 0)),
                pl.BlockSpec((None, D_MODEL, tn), lambda i, j: (w_lead, 0, w_col0 + j)),
            ],
            out_specs=pl.BlockSpec((dil, seg, tn), lambda i, j: (0, i, j)),
            out_shape=out_z,
            scratch_shapes=[pltpu.VMEM((tm, D_MODEL), BF16)] + (
                [pltpu.VMEM((ROW_CHUNKS, tm, V7X_LANES), F32)] * 2 if dil > 1 else []),
            compiler_params=_cparams("parallel", "arbitrary"),
            name=f"norm_proj_d{dil}",
        )(h, g.reshape(1, D_MODEL), w)
    assert dil == 1 and tm % nj == 0
    grid_spec = pltpu.PrefetchScalarGridSpec(
        num_scalar_prefetch=1,
        grid=(ni, nj),
        in_specs=[
            pl.BlockSpec(memory_space=pl.ANY),
            pl.BlockSpec((1, D_MODEL), lambda i, j, p: (0, 0)),
            pl.BlockSpec((None, D_MODEL, tn), lambda i, j, p: (w_lead, 0, w_col0 + j)),
        ],
        out_specs=[pl.BlockSpec((1, tm, tn), lambda i, j, p: (0, i, j)),
                   pl.BlockSpec((tm, D_MODEL), lambda i, j, p: (i, 0))],
        scratch_shapes=[pltpu.VMEM((tm, D_MODEL), BF16),
                        pltpu.VMEM((2, tm * ROW_CHUNKS, V7X_LANES), F32),
                        pltpu.SemaphoreType.DMA((2,))],
    )
    return pl.pallas_call(
        functools.partial(_gather_proj_kernel, gelu_cols=gelu_cols, ni=ni, nj=nj),
        grid_spec=grid_spec,
        out_shape=[out_z, jax.ShapeDtypeStruct((rows, D_MODEL), F32)],
        compiler_params=_cparams("arbitrary", "arbitrary"),
        name="gather_norm_proj",
    )(pos, h, g.reshape(1, D_MODEL), w)


def _t5_bucket_np(rel):
    nb = REL_BUCKETS // 2
    max_exact = nb // 2
    ret = np.where(rel > 0, nb, 0)
    n = np.abs(rel)
    nf = np.maximum(n, 1).astype(np.float32)
    large = max_exact + (np.log(nf / np.float32(max_exact)) / np.float32(math.log(REL_MAX_DIST / max_exact))
                         * np.float32(nb - max_exact)).astype(np.int32)
    large = np.minimum(large, nb - 1)
    return (ret + np.where(n < max_exact, n, large)).astype(np.int32)


def _bias_kernel(table_ref, idx_ref, mask_ref, o_ref, *, block):
    idx = idx_ref[...]
    for h in range(REL_HEADS):
        acc = jnp.zeros(idx.shape, F32)
        for b in range(REL_BUCKETS):
            acc = jnp.where(idx == b, table_ref[b, h], acc)
        for v in range(3):
            o_ref[v, h * block:(h + 1) * block, :] = acc + mask_ref[v]


def _band_bias(table, block, half, dil):
    rel = np.arange(3 * block)[None, :] - block - np.arange(block)[:, None]
    band = np.abs(rel) <= half
    col = np.arange(3 * block)[None, :]
    masks = np.stack([band & (col >= block), band, band & (col < 2 * block)])
    add = np.where(masks, 0.0, NEG_INF).astype(np.float32)
    return pl.pallas_call(
        functools.partial(_bias_kernel, block=block),
        in_specs=[pl.BlockSpec(memory_space=pltpu.SMEM),
                  pl.BlockSpec(memory_space=pltpu.VMEM),
                  pl.BlockSpec(memory_space=pltpu.VMEM)],
        out_specs=pl.BlockSpec(memory_space=pltpu.VMEM),
        out_shape=jax.ShapeDtypeStruct((3, REL_HEADS * block, 3 * block), F32),
        name=f"rel_bias_d{dil}",
    )(table, jnp.asarray(_t5_bucket_np(rel * dil)), jnp.asarray(add))


def _even_mix_kernel(z_ref, kvp_ref, kvn_ref, h_ref, lng_ref, lnb_ref, ws_ref, bs_ref, bias_ref,
                     sink_ref, wout_ref, o_ref, kv_scr, y_scr):
    i = pl.program_id(0)
    nsub = EVEN_TB // B_BLOCK
    nblk = SEQ // B_BLOCK
    kv_scr[0:B_BLOCK] = kvp_ref[...]
    kv_scr[B_BLOCK:B_BLOCK + EVEN_TB] = z_ref[0, :, EVEN_K0:EVEN_IN]
    kv_scr[B_BLOCK + EVEN_TB:] = kvn_ref[...]
    lng = lng_ref[...]
    lnb = lnb_ref[...]
    for s0 in range(0, nsub, EVEN_SUBS):
        subs = []
        for s in range(s0, s0 + EVEN_SUBS):
            r0 = s * B_BLOCK
            gb = i * nsub + s
            subs.append((r0, jnp.where(gb == 0, 0, jnp.where(gb == nblk - 1, 2, 1))))
        vns = []
        for r0, _ in subs:
            va = z_ref[0, r0:r0 + A_CHUNK, A_WIDTH:2 * A_WIDTH].astype(F32)
            mu = jnp.mean(va, axis=-1, keepdims=True)
            vc = va - mu
            var = jnp.mean(vc * vc, axis=-1, keepdims=True)
            vns.append((vc * lax.rsqrt(var + EPS) * lng + lnb).astype(BF16))
        mixes = [[_dot(ws_ref[g], vn[:, g * A_CH:(g + 1) * A_CH]) for g in range(A_GROUPS)] for vn in vns]
        for (r0, _), mixed in zip(subs, mixes):
            for g in range(A_GROUPS):
                c0 = g * A_CH
                u = z_ref[0, r0:r0 + A_CHUNK, c0:c0 + A_CH].astype(F32)
                y_scr[r0:r0 + A_CHUNK, c0:c0 + A_CH] = (u * (mixed[g] + bs_ref[g])).astype(BF16)
        units = [(r0, sel, kh, kh * B_Q_PER_KV + half * EVEN_STACK)
                 for r0, sel in subs
                 for kh in range(B_KV_HEADS) for half in range(B_Q_PER_KV // EVEN_STACK)]
        lgs = []
        for r0, _, kh, hd0 in units:
            kw = kv_scr[r0:r0 + 3 * B_BLOCK, kh * B_HEAD_DIM:(kh + 1) * B_HEAD_DIM]
            q = jnp.concatenate(
                [z_ref[0, r0:r0 + B_BLOCK,
                       EVEN_Q0 + (hd0 + g) * B_HEAD_DIM:EVEN_Q0 + (hd0 + g + 1) * B_HEAD_DIM]
                 for g in range(EVEN_STACK)], axis=0)
            lgs.append(_dot_nt(q, kw))
        lgs = [lg * (B_HEAD_DIM ** -0.5) + bias_ref[sel, hd0 * B_BLOCK:(hd0 + EVEN_STACK) * B_BLOCK, :]
               for lg, (_, sel, _, hd0) in zip(lgs, units)]
        sks = [sink_ref[hd0 * B_BLOCK:(hd0 + EVEN_STACK) * B_BLOCK, :] for _, _, _, hd0 in units]
        ms = [jnp.maximum(jnp.max(lg, axis=-1, keepdims=True), sk) for lg, sk in zip(lgs, sks)]
        ps = [jnp.exp(lg - m) for lg, m in zip(lgs, ms)]
        dens = [jnp.sum(p, axis=-1, keepdims=True) + jnp.exp(sk - m) for p, sk, m in zip(ps, sks, ms)]
        os_ = []
        for p, (r0, _, kh, _) in zip(ps, units):
            vw = kv_scr[r0:r0 + 3 * B_BLOCK,
                        B_KV_WIDTH + kh * B_HEAD_DIM:B_KV_WIDTH + (kh + 1) * B_HEAD_DIM]
            os_.append(_dot(p.astype(BF16), vw))
        for o, den, (r0, _, _, hd0) in zip(os_, dens, units):
            o = o * (1.0 / den)
            for g in range(EVEN_STACK):
                c0 = A_WIDTH + (hd0 + g) * B_HEAD_DIM
                y_scr[r0:r0 + B_BLOCK, c0:c0 + B_HEAD_DIM] = o[g * B_BLOCK:(g + 1) * B_BLOCK].astype(BF16)
    o_ref[...] = h_ref[...] + _dot(y_scr[...], wout_ref[...])


def _even_mix(z, h, ln_g, ln_b, w_s, b_s, bias, sink, w_out):
    nsub = EVEN_TB // B_BLOCK
    nblk = SEQ // B_BLOCK
    kv_cb = EVEN_K0 // (2 * B_KV_WIDTH)
    sink_col = jnp.broadcast_to(sink.reshape(B_HEADS, 1, 1), (B_HEADS, B_BLOCK, 1)).reshape(
        B_HEADS * B_BLOCK, 1)
    return pl.pallas_call(
        _even_mix_kernel,
        grid=(SEQ // EVEN_TB,),
        in_specs=[
            pl.BlockSpec((1, EVEN_TB, EVEN_IN), lambda i: (0, i, 0)),
            pl.BlockSpec((None, B_BLOCK, 2 * B_KV_WIDTH),
                         lambda i: (0, jnp.maximum(i * nsub - 1, 0), kv_cb)),
            pl.BlockSpec((None, B_BLOCK, 2 * B_KV_WIDTH),
                         lambda i: (0, jnp.minimum((i + 1) * nsub, nblk - 1), kv_cb)),
            pl.BlockSpec((EVEN_TB, D_MODEL), lambda i: (i, 0)),
            pl.BlockSpec((1, A_WIDTH), lambda i: (0, 0)),
            pl.BlockSpec((1, A_WIDTH), lambda i: (0, 0)),
            pl.BlockSpec((A_GROUPS, A_CHUNK, A_CHUNK), lambda i: (0, 0, 0)),
            pl.BlockSpec((A_GROUPS, A_CHUNK, A_CH), lambda i: (0, 0, 0)),
            pl.BlockSpec((3, B_HEADS * B_BLOCK, 3 * B_BLOCK), lambda i: (0, 0, 0)),
            pl.BlockSpec((B_HEADS * B_BLOCK, 1), lambda i: (0, 0)),
            pl.BlockSpec((A_WIDTH + B_WIDTH, D_MODEL), lambda i: (0, 0)),
        ],
        out_specs=pl.BlockSpec((EVEN_TB, D_MODEL), lambda i: (i, 0)),
        out_shape=jax.ShapeDtypeStruct((SEQ, D_MODEL), F32),
        scratch_shapes=[
            pltpu.VMEM((EVEN_TB + 2 * B_BLOCK, 2 * B_KV_WIDTH), BF16),
            pltpu.VMEM((EVEN_TB, A_WIDTH + B_WIDTH), BF16),
        ],
        compiler_params=_cparams("parallel"),
        name="even_mixer",
    )(z, z, z, h, ln_g.reshape(1, A_WIDTH), ln_b.reshape(1, A_WIDTH), w_s.astype(BF16),
      jnp.broadcast_to(b_s[:, :, None], (A_GROUPS, A_CHUNK, A_CH)), bias, sink_col, w_out)


def _dil_attn_kernel(zc_ref, kp_ref, kn_ref, vp_ref, vn_ref, bias_ref, o_ref, lse_ref,
                     k_scr, v_scr, o_scr, lse_scr, *, dil):
    t = pl.program_id(0)
    seg = ODD_TILE // dil
    nsb = seg // C_BLOCK
    nblk = SEQ // dil // C_BLOCK
    k_scr[:, 0:C_BLOCK] = kp_ref[...]
    k_scr[:, C_BLOCK:C_BLOCK + seg] = zc_ref[:, :, C_WIDTH:2 * C_WIDTH]
    k_scr[:, C_BLOCK + seg:] = kn_ref[...]
    v_scr[:, 0:C_BLOCK] = vp_ref[...]
    v_scr[:, C_BLOCK:C_BLOCK + seg] = zc_ref[:, :, 2 * C_WIDTH:3 * C_WIDTH]
    v_scr[:, C_BLOCK + seg:] = vn_ref[...]
    lane = lax.broadcasted_iota(jnp.int32, (C_BLOCK, V7X_LANES), 1)

    def body(it, carry):
        blocks = []
        for u in range(ODD_INTERLEAVE):
            n = it * ODD_INTERLEAVE + u
            r = n // nsb
            s = n % nsb
            r0 = pl.multiple_of(s * C_BLOCK, C_BLOCK)
            gb = t * nsb + s
            sel = jnp.where(gb == 0, 0, jnp.where(gb == nblk - 1, 2, 1))
            rows = (pl.ds(s * (C_BLOCK * dil) + _slab_residue(r, dil), C_BLOCK, stride=dil) if dil > 1
                    else pl.ds(r0, C_BLOCK))
            blocks.append((r, r0, sel, rows))
        lgs = []
        for r, r0, _, _ in blocks:
            for hd in range(C_HEADS):
                c0 = hd * C_HEAD_DIM
                q = zc_ref[r, pl.ds(r0, C_BLOCK), c0:c0 + C_HEAD_DIM]
                kw = k_scr[r, pl.ds(r0, 3 * C_BLOCK), c0:c0 + C_HEAD_DIM]
                lgs.append(_dot_nt(q, kw))
        lg = [jnp.concatenate(lgs[u * C_HEADS:(u + 1) * C_HEADS], axis=0) * (C_HEAD_DIM ** -0.5)
              + bias_ref[blk[2]] for u, blk in enumerate(blocks)]
        m = [jnp.max(x, axis=-1, keepdims=True) for x in lg]
        p = [jnp.exp(x - mm) for x, mm in zip(lg, m)]
        den = [jnp.sum(x, axis=-1, keepdims=True) for x in p]
        inv = [1.0 / d for d in den]
        lse = [mm + jnp.log(d) for mm, d in zip(m, den)]
        pb = [x.astype(BF16) for x in p]
        outs = []
        for u, (r, r0, _, _) in enumerate(blocks):
            for hd in range(C_HEADS):
                c0 = hd * C_HEAD_DIM
                vw = v_scr[r, pl.ds(r0, 3 * C_BLOCK), c0:c0 + C_HEAD_DIM]
                outs.append(_dot(pb[u][hd * C_BLOCK:(hd + 1) * C_BLOCK], vw))
        for u, (_, _, _, rows) in enumerate(blocks):
            lse_tile = jnp.zeros((C_BLOCK, V7X_LANES), F32)
            for hd in range(C_HEADS):
                o_scr[hd, rows, :] = outs[u * C_HEADS + hd] * inv[u][hd * C_BLOCK:(hd + 1) * C_BLOCK]
                lse_tile = jnp.where(lane == hd, lse[u][hd * C_BLOCK:(hd + 1) * C_BLOCK], lse_tile)
            lse_scr[rows, :] = lse_tile
        return carry

    lax.fori_loop(0, ODD_BLOCKS // ODD_INTERLEAVE, body, 0)
    for hd in range(C_HEADS):
        o_ref[:, hd * C_HEAD_DIM:(hd + 1) * C_HEAD_DIM] = o_scr[hd].astype(o_ref.dtype)
    lse_ref[...] = lse_scr[...]


def _dil_attn(zg, bias, dil):
    seg = ODD_TILE // dil
    nsb = seg // C_BLOCK
    last = SEQ // dil // C_BLOCK - 1

    def halo(j, nxt):
        if nxt:
            return pl.BlockSpec((dil, C_BLOCK, C_WIDTH),
                                lambda t: (0, jnp.minimum((t + 1) * nsb, last), j))
        return pl.BlockSpec((dil, C_BLOCK, C_WIDTH), lambda t: (0, jnp.maximum(t * nsb - 1, 0), j))

    return pl.pallas_call(
        functools.partial(_dil_attn_kernel, dil=dil),
        grid=(SEQ // ODD_TILE,),
        in_specs=[pl.BlockSpec((dil, seg, 3 * C_WIDTH), lambda t: (0, t, 0)),
                  halo(1, False), halo(1, True), halo(2, False), halo(2, True),
                  pl.BlockSpec((3, C_HEADS * C_BLOCK, 3 * C_BLOCK), lambda t: (0, 0, 0))],
        out_specs=[pl.BlockSpec((ODD_TILE, C_WIDTH), lambda t: (t, 0)),
                   pl.BlockSpec((ODD_TILE, V7X_LANES), lambda t: (t, 0))],
        out_shape=[jax.ShapeDtypeStruct((SEQ, C_WIDTH), BF16),
                   jax.ShapeDtypeStruct((SEQ, V7X_LANES), F32)],
        scratch_shapes=[pltpu.VMEM((dil, seg + 2 * C_BLOCK, C_WIDTH), BF16),
                        pltpu.VMEM((dil, seg + 2 * C_BLOCK, C_WIDTH), BF16),
                        pltpu.VMEM((C_HEADS, ODD_TILE, C_HEAD_DIM), F32),
                        pltpu.VMEM((ODD_TILE, V7X_LANES), F32)],
        compiler_params=_cparams("parallel"),
        name=f"dilated_attn_d{dil}",
    )(zg, zg, zg, zg, zg, bias)


def _combine_kernel(o0_ref, o1_ref, o2_ref, l0_ref, l1_ref, l2_ref, h_ref, wout_ref, out_ref, y_scr):
    l0 = l0_ref[...]
    l1 = l1_ref[...]
    l2 = l2_ref[...]
    m = jnp.maximum(jnp.maximum(l0, l1), l2)
    e0 = jnp.exp(l0 - m)
    e1 = jnp.exp(l1 - m)
    e2 = jnp.exp(l2 - m)
    tot = e0 + e1 + e2
    w0 = e0 / tot
    w1 = e1 / tot
    w2 = e2 / tot
    for hd in range(C_HEADS):
        c0 = hd * C_HEAD_DIM
        y = (w0[:, hd:hd + 1] * o0_ref[:, c0:c0 + C_HEAD_DIM].astype(F32)
             + w1[:, hd:hd + 1] * o1_ref[:, c0:c0 + C_HEAD_DIM].astype(F32)
             + w2[:, hd:hd + 1] * o2_ref[:, c0:c0 + C_HEAD_DIM].astype(F32))
        y_scr[:, c0:c0 + C_HEAD_DIM] = y.astype(BF16)
    out_ref[...] = h_ref[...] + _dot(y_scr[...], wout_ref[...])


def _combine(outs, lses, h, w_out):
    blk_o = pl.BlockSpec((COMB_TB, C_WIDTH), lambda i: (i, 0))
    blk_l = pl.BlockSpec((COMB_TB, V7X_LANES), lambda i: (i, 0))
    return pl.pallas_call(
        _combine_kernel,
        grid=(SEQ // COMB_TB,),
        in_specs=[blk_o, blk_o, blk_o, blk_l, blk_l, blk_l,
                  pl.BlockSpec((COMB_TB, D_MODEL), lambda i: (i, 0)),
                  pl.BlockSpec((C_WIDTH, D_MODEL), lambda i: (0, 0))],
        out_specs=pl.BlockSpec((COMB_TB, D_MODEL), lambda i: (i, 0)),
        out_shape=jax.ShapeDtypeStruct((SEQ, D_MODEL), F32),
        scratch_shapes=[pltpu.VMEM((COMB_TB, C_WIDTH), BF16)],
        compiler_params=_cparams("parallel"),
        name="group_combine_proj",
    )(*outs, *lses, h, w_out)


def _cross_kernel(h_ref, gx_ref, wq_ref, kv_ref, wo_ref, gf_ref, wr_ref, br_ref,
                  hx_ref, meta_ref, o_scr):
    h = h_ref[...]
    q = _dot(_rms(h, gx_ref[...]).astype(BF16), wq_ref[...]).astype(BF16)
    cols = [hd * X_HEAD_DIM for hd in range(X_HEADS)]
    lgs = [_dot_nt(q[:, c0:c0 + X_HEAD_DIM], kv_ref[0, :, c0:c0 + X_HEAD_DIM]) * (X_HEAD_DIM ** -0.5)
           for c0 in cols]
    ms = [jnp.max(lg, axis=-1, keepdims=True) for lg in lgs]
    ps = [jnp.exp(lg - m) for lg, m in zip(lgs, ms)]
    dens = [jnp.sum(p, axis=-1, keepdims=True) for p in ps]
    os_ = [_dot(p.astype(BF16), kv_ref[0, :, X_WIDTH + c0:X_WIDTH + c0 + X_HEAD_DIM])
           for p, c0 in zip(ps, cols)]
    for o, den, c0 in zip(os_, dens, cols):
        o_scr[:, c0:c0 + X_HEAD_DIM] = (o / den).astype(BF16)
    h2 = h + _dot(o_scr[...], wo_ref[...])
    _store_rows(hx_ref, h2)

    t = _rms(h2, gf_ref[...])
    t_hi = t.astype(BF16)
    t_lo = (t - t_hi.astype(F32)).astype(BF16)
    lt = (_dot_nt(wr_ref[0], t_hi) + _dot_nt(wr_ref[0], t_lo) + _dot_nt(wr_ref[1], t_hi)) + br_ref[...]
    g = [lt[k:k + 1, :] for k in range(MOE_GROUPS)]
    gmax = jnp.maximum(jnp.maximum(g[0], g[1]), jnp.maximum(g[2], g[3]))
    grp = jnp.where(g[0] == gmax, 0, jnp.where(g[1] == gmax, 1, jnp.where(g[2] == gmax, 2, 3)))
    g_gate = 1.0 / (jnp.exp(g[0] - gmax) + jnp.exp(g[1] - gmax) + jnp.exp(g[2] - gmax)
                    + jnp.exp(g[3] - gmax))
    e = []
    for k in range(MOE_EPG):
        rows = [lt[MOE_GROUPS + gi * MOE_EPG + k:MOE_GROUPS + gi * MOE_EPG + k + 1, :]
                for gi in range(MOE_GROUPS)]
        e.append(jnp.where(grp == 0, rows[0], jnp.where(grp == 1, rows[1],
                                                         jnp.where(grp == 2, rows[2], rows[3]))))
    v1 = jnp.maximum(jnp.maximum(e[0], e[1]), jnp.maximum(e[2], e[3]))
    i1 = jnp.where(e[0] == v1, 0, jnp.where(e[1] == v1, 1, jnp.where(e[2] == v1, 2, 3)))
    r = [jnp.where(i1 == k, -jnp.inf, e[k]) for k in range(MOE_EPG)]
    v2 = jnp.maximum(jnp.maximum(r[0], r[1]), jnp.maximum(r[2], r[3]))
    i2 = jnp.where(r[0] == v2, 0, jnp.where(r[1] == v2, 1, jnp.where(r[2] == v2, 2, 3)))
    d = jnp.exp(v2 - v1)
    w1 = g_gate / (1.0 + d)
    w2 = g_gate * d / (1.0 + d)
    first_lo = i1 < i2
    lo = jnp.where(first_lo, i1, i2)
    hi = jnp.where(first_lo, i2, i1)
    w_lo = jnp.where(first_lo, w1, w2)
    w_hi = jnp.where(first_lo, w2, w1)
    pair = jnp.where(lo == 0, hi - 1, jnp.where(lo == 1, jnp.where(hi == 3, 3, 4), 5))
    w_a = jnp.where(lo == 2, w_hi, w_lo)
    w_b = jnp.where(lo == 2, w_lo, w_hi)
    bucket = (grp * N_PAIRS + pair).astype(F32)
    row = lax.broadcasted_iota(jnp.int32, (8, CROSS_TB), 0)
    meta_ref[...] = jnp.where(row == 0, bucket, jnp.where(row == 1, w_a, jnp.where(row == 2, w_b, 0.0)))


def _cross_router(h, g_cross, wq, kv, wo, g_ffn, wr_t, br):
    full = lambda shape: pl.BlockSpec(shape, lambda i: tuple(0 for _ in shape))
    return pl.pallas_call(
        _cross_kernel,
        grid=(SEQ // CROSS_TB,),
        in_specs=[
            pl.BlockSpec((CROSS_TB, D_MODEL), lambda i: (i, 0)),
            full((1, D_MODEL)),
            full((D_MODEL, X_WIDTH)),
            full((1, MEM_LEN, 2 * X_WIDTH)),
            full((X_WIDTH, D_MODEL)),
            full((1, D_MODEL)),
            full((2, ROUTER_ROWS, D_MODEL)),
            full((ROUTER_ROWS, 1)),
        ],
        out_specs=[pl.BlockSpec((CROSS_TB * ROW_CHUNKS, V7X_LANES), lambda i: (i, 0)),
                   pl.BlockSpec((8, CROSS_TB), lambda i: (0, i))],
        out_shape=[jax.ShapeDtypeStruct((SEQ * ROW_CHUNKS, V7X_LANES), F32),
                   jax.ShapeDtypeStruct((8, SEQ), F32)],
        scratch_shapes=[pltpu.VMEM((CROSS_TB, X_WIDTH), BF16)],
        compiler_params=_cparams("parallel"),
        name="cross_attn_router",
    )(h, g_cross.reshape(1, D_MODEL), wq, kv, wo, g_ffn.reshape(1, D_MODEL), wr_t, br)


def _moe_kernel(src_ref, ea_ref, eb_ref, nused_ref,
                hx_hbm, gates_ref, gf_ref, wga_ref, wua_ref, wda_ref, wgb_ref, wub_ref, wdb_ref,
                out_ref, xbuf, wup_a, wdn_a, wup_b, wdn_b, gsem):
    k = pl.program_id(0)
    nused = nused_ref[0]
    xslot = k % 3
    gather = _RowGather(src_ref, hx_hbm, xbuf, gsem, MOE_TM)

    @pl.when(k == 0)
    def _():
        gather.start(0, 0, range(MOE_TM))
        gather.start(1, 1, range(MOE_TM))

    prev = jnp.maximum(k - 1, 0)

    @pl.when((k < nused) & ((k == 0) | (ea_ref[k] != ea_ref[prev])))
    def _():
        wup_a[:, :D_EXPERT] = wga_ref[0].astype(BF16)
        wup_a[:, D_EXPERT:] = wua_ref[0].astype(BF16)
        wdn_a[...] = wda_ref[0].astype(BF16)

    @pl.when((k < nused) & ((k == 0) | (eb_ref[k] != eb_ref[prev])))
    def _():
        wup_b[:, :D_EXPERT] = wgb_ref[0].astype(BF16)
        wup_b[:, D_EXPERT:] = wub_ref[0].astype(BF16)
        wdn_b[...] = wdb_ref[0].astype(BF16)

    @pl.when(k < nused)
    def _():
        gather.wait(xslot)
        h2 = _load_rows(xbuf, MOE_TM, (xslot,))
        gather.start(k + 2, (k + 2) % 3, range(MOE_TM))
        t = _rms(h2, gf_ref[...]).astype(BF16)
        gus = [_dot(t, wup[...]) for wup in (wup_a, wup_b)]
        hids = [(jax.nn.silu(gu[:, :D_EXPERT]) * gu[:, D_EXPERT:] * gates_ref[:, col:col + 1]).astype(BF16)
                for col, gu in enumerate(gus)]
        ys = [_dot(hid, wdn[...]) for hid, wdn in zip(hids, (wdn_a, wdn_b))]
        _store_rows(out_ref, h2 + (ys[0] + ys[1]))

    @pl.when(k >= nused)
    def _():
        out_ref[...] = jnp.zeros(out_ref.shape, F32)

    @pl.when(k == nused - 1)
    def _():
        gather.wait((k + 1) % 3)
        gather.wait((k + 2) % 3)


def _moe(hx, gates, g_ffn, w_gate, w_up, w_down, src, ea, eb, nused):
    def wspec(shape, which):
        if which == 0:
            return pl.BlockSpec((1,) + shape, lambda k, s, a, b, n: (a[k], 0, 0))
        return pl.BlockSpec((1,) + shape, lambda k, s, a, b, n: (b[k], 0, 0))

    up_shape = (D_MODEL, D_EXPERT)
    down_shape = (D_EXPERT, D_MODEL)
    grid_spec = pltpu.PrefetchScalarGridSpec(
        num_scalar_prefetch=4,
        grid=(MOE_TILES,),
        in_specs=[
            pl.BlockSpec(memory_space=pl.ANY),
            pl.BlockSpec((MOE_TM, V7X_LANES), lambda k, s, a, b, n: (k, 0)),
            pl.BlockSpec((1, D_MODEL), lambda k, s, a, b, n: (0, 0)),
            wspec(up_shape, 0), wspec(up_shape, 0), wspec(down_shape, 0),
            wspec(up_shape, 1), wspec(up_shape, 1), wspec(down_shape, 1),
        ],
        out_specs=pl.BlockSpec((MOE_TM * ROW_CHUNKS, V7X_LANES), lambda k, s, a, b, n: (k, 0)),
        scratch_shapes=[
            pltpu.VMEM((3, MOE_TM * ROW_CHUNKS, V7X_LANES), F32),
            pltpu.VMEM((D_MODEL, 2 * D_EXPERT), BF16),
            pltpu.VMEM((D_EXPERT, D_MODEL), BF16),
            pltpu.VMEM((D_MODEL, 2 * D_EXPERT), BF16),
            pltpu.VMEM((D_EXPERT, D_MODEL), BF16),
            pltpu.SemaphoreType.DMA((3,)),
        ],
    )
    return pl.pallas_call(
        _moe_kernel,
        grid_spec=grid_spec,
        out_shape=jax.ShapeDtypeStruct((MOE_TILES * MOE_TM * ROW_CHUNKS, V7X_LANES), F32),
        compiler_params=_cparams("arbitrary"),
        name="routed_moe",
    )(src, ea, eb, nused, hx, gates, g_ffn.reshape(1, D_MODEL),
      w_gate, w_up, w_down, w_gate, w_up, w_down)


def _route_tables(meta):
    bucket = meta[0].astype(jnp.int32)
    ids = jnp.arange(N_BUCKETS, dtype=jnp.int32)
    counts = jnp.sum((bucket[:, None] == ids[None, :]).astype(jnp.int32), axis=0)
    ntile = (counts + MOE_TM - 1) // MOE_TM
    pad = ntile * MOE_TM - counts
    tile_end = jnp.cumsum(ntile)
    nused = tile_end[-1]
    dummy_key = jnp.where(jnp.arange(MOE_TM - 1, dtype=jnp.int32)[None, :] < pad[:, None],
                          ids[:, None], N_BUCKETS)
    keys = jnp.concatenate([bucket, dummy_key.reshape(-1)])
    vals = jnp.concatenate([jnp.arange(SEQ, dtype=jnp.int32),
                            jnp.full((N_BUCKETS * (MOE_TM - 1),), SEQ, jnp.int32)])
    nslot = MOE_TILES * MOE_TM
    zpad = jnp.zeros((N_BUCKETS * (MOE_TM - 1),), F32)
    _, tok, ga, gb = lax.sort((keys, vals, jnp.concatenate([meta[1], zpad]), jnp.concatenate([meta[2], zpad])),
                              num_keys=1, is_stable=True)
    tok = tok[:nslot]
    gates = jnp.pad(jnp.stack([ga[:nslot], gb[:nslot]], axis=1), ((0, 0), (0, V7X_LANES - 2)))
    valid = tok < SEQ
    src = jnp.concatenate([jnp.where(valid, tok, 0), jnp.zeros((2 * MOE_TM,), jnp.int32)])
    _, pos = lax.sort((tok, jnp.arange(nslot, dtype=jnp.int32)), num_keys=1, is_stable=True)
    pos = pos[:SEQ]
    tiles = jnp.arange(MOE_TILES, dtype=jnp.int32)
    tile_bucket = jnp.minimum(jnp.sum((tiles[:, None] >= tile_end[None, :]).astype(jnp.int32), axis=1),
                              N_BUCKETS - 1)
    onehot = (tile_bucket[:, None] == ids[None, :]).astype(jnp.int32)
    base = (np.arange(N_BUCKETS) // N_PAIRS) * MOE_EPG
    ea = jnp.sum(onehot * jnp.asarray(base + np.asarray(SLOT_A)[np.arange(N_BUCKETS) % N_PAIRS],
                                      jnp.int32)[None, :], axis=1)
    eb = jnp.sum(onehot * jnp.asarray(base + np.asarray(SLOT_B)[np.arange(N_BUCKETS) % N_PAIRS],
                                      jnp.int32)[None, :], axis=1)
    return (src.astype(jnp.int32), pos.astype(jnp.int32), ea.astype(jnp.int32), eb.astype(jnp.int32),
            nused.reshape(1).astype(jnp.int32), gates)


def _final_norm_kernel(pos_ref, hs_hbm, g_ref, o_ref, hbuf, sem, *, ni):
    i = pl.program_id(0)
    tm = o_ref.shape[0]
    slot = i % 2
    gather = _RowGather(pos_ref, hs_hbm, hbuf, sem, tm, both_queues=True)

    @pl.when(i == 0)
    def _():
        gather.start(0, 0, range(tm))

    @pl.when(i + 1 < ni)
    def _():
        gather.start(i + 1, 1 - slot, range(tm))

    gather.wait(slot)
    o_ref[...] = _rms(_load_rows(hbuf, tm, (slot,)), g_ref[...])


def _final_norm(hs, pos, g):
    tb = FINAL_TB
    ni = SEQ // tb
    grid_spec = pltpu.PrefetchScalarGridSpec(
        num_scalar_prefetch=1,
        grid=(ni,),
        in_specs=[pl.BlockSpec(memory_space=pl.ANY),
                  pl.BlockSpec((1, D_MODEL), lambda i, p: (0, 0))],
        out_specs=pl.BlockSpec((tb, D_MODEL), lambda i, p: (i, 0)),
        scratch_shapes=[pltpu.VMEM((2, tb * ROW_CHUNKS, V7X_LANES), F32),
                        pltpu.SemaphoreType.DMA((2,))],
    )
    return pl.pallas_call(
        functools.partial(_final_norm_kernel, ni=ni),
        grid_spec=grid_spec,
        out_shape=jax.ShapeDtypeStruct((SEQ, D_MODEL), F32),
        compiler_params=_cparams("arbitrary"),
        name="final_norm",
    )(pos, hs, g.reshape(1, D_MODEL))


def kernel(x, mem, ln_mix, ln_cross, ln_mem, ln_ffn, ln_final, rel_table, even_w_in, even_w_out,
           sgu_ln_g, sgu_ln_b, sgu_w, sgu_b, attn_sink, odd_w_in, odd_w_out, xq_w, xkv_w, xo_w,
           router_group_w, router_group_b, router_expert_w, router_expert_b,
           expert_w_gate, expert_w_up, expert_w_down):
    h = x.reshape(SEQ, D_MODEL)
    mem2 = mem.reshape(MEM_LEN, D_MODEL)
    bias_even = _band_bias(rel_table, B_BLOCK, B_HALF_WINDOW, 1)
    bias_odd = [_band_bias(rel_table, C_BLOCK, window // 2 // dil, dil) for window, dil in C_PAIRS]

    pos = None
    for layer in range(DEPTH):
        i = layer // 2
        if layer % 2 == 0:
            z = _proj(h, ln_mix[layer], even_w_in, w_lead=i, rows=SEQ, tm=EVEN_TB, tn=EVEN_IN, n=EVEN_IN,
                      gelu_cols=2 * A_WIDTH, pos=pos)
            if pos is not None:
                z, h = z
            h = _even_mix(z, h, sgu_ln_g[i], sgu_ln_b[i], sgu_w[i], sgu_b[i], bias_even,
                          attn_sink[i], even_w_out[i].astype(BF16))
        else:
            zg, h = _proj(h, ln_mix[layer], odd_w_in, w_lead=i, rows=SEQ, tm=PROJ_TM, tn=GATHER_TN,
                          n=3 * C_WIDTH, pos=pos)
            outs, lses = [], []
            for gi, (_, dil) in enumerate(C_PAIRS):
                if gi > 0:
                    zg = _proj(h, ln_mix[layer], odd_w_in, w_lead=i, w_col0=gi * 3 * C_WIDTH // PROJ_TN,
                               rows=SEQ, tm=PROJ_TM, tn=PROJ_TN, n=3 * C_WIDTH, dil=dil)
                o, lse = _dil_attn(zg, bias_odd[gi], dil)
                outs.append(o)
                lses.append(lse)
            h = _combine(outs, lses, h, odd_w_out[i].astype(BF16))

        kv = _proj(mem2, ln_mem[layer], xkv_w, w_lead=layer, rows=MEM_LEN, tm=MEM_LEN,
                   tn=2 * X_WIDTH, n=2 * X_WIDTH)
        wr_t = jnp.zeros((ROUTER_ROWS, D_MODEL), F32)
        wr_t = wr_t.at[:MOE_GROUPS].set(router_group_w[layer].T)
        wr_t = wr_t.at[MOE_GROUPS:MOE_GROUPS + N_EXPERTS].set(
            router_expert_w[layer].reshape(D_MODEL, N_EXPERTS).T)
        br = jnp.zeros((ROUTER_ROWS, 1), F32)
        br = br.at[:MOE_GROUPS, 0].set(router_group_b[layer])
        br = br.at[MOE_GROUPS:MOE_GROUPS + N_EXPERTS, 0].set(router_expert_b[layer].reshape(N_EXPERTS))
        wr_hi = wr_t.astype(BF16)
        wr_split = jnp.stack([wr_hi, (wr_t - wr_hi.astype(F32)).astype(BF16)])
        hx, meta = _cross_router(h, ln_cross[layer], xq_w[layer].astype(BF16), kv,
                                 xo_w[layer].astype(BF16), ln_ffn[layer], wr_split, br)

        src, pos, ea, eb, nused, gates = _route_tables(meta)
        h = _moe(hx, gates, ln_ffn[layer],
                 expert_w_gate.reshape(DEPTH * N_EXPERTS, D_MODEL, D_EXPERT),
                 expert_w_up.reshape(DEPTH * N_EXPERTS, D_MODEL, D_EXPERT),
                 expert_w_down.reshape(DEPTH * N_EXPERTS, D_EXPERT, D_MODEL),
                 src, ea + layer * N_EXPERTS, eb + layer * N_EXPERTS, nused)

    return _final_norm(h, pos, ln_final).reshape(1, SEQ, D_MODEL)
```

```python
import functools
import math

import numpy as np
import jax
import jax.numpy as jnp
from jax import lax
from jax.experimental import pallas as pl
from jax.experimental.pallas import tpu as pltpu

F32 = jnp.float32
BF16 = jnp.bfloat16

D_MODEL = 1024
SEQ = 16384
DEPTH = 4
MEM_LEN = 256
EPS = 1e-6
NEG_INF = -1e30

A_GROUPS = 4
A_CH = 128
A_WIDTH = A_GROUPS * A_CH
A_CHUNK = 128
B_HEADS = 8
B_KV_HEADS = 2
B_Q_PER_KV = B_HEADS // B_KV_HEADS
B_HEAD_DIM = 64
B_WIDTH = B_HEADS * B_HEAD_DIM
B_KV_WIDTH = B_KV_HEADS * B_HEAD_DIM
B_HALF_WINDOW = 128
B_BLOCK = 128
EVEN_IN = 2 * A_WIDTH + B_WIDTH + 2 * B_KV_WIDTH
EVEN_Q0 = 2 * A_WIDTH
EVEN_K0 = EVEN_Q0 + B_WIDTH
EVEN_V0 = EVEN_K0 + B_KV_WIDTH

C_PAIRS = ((128, 1), (512, 4), (2048, 16))
C_GROUPS = len(C_PAIRS)
C_HEADS = 8
C_HEAD_DIM = 128
C_WIDTH = C_HEADS * C_HEAD_DIM
C_BLOCK = 64
ODD_IN = C_GROUPS * 3 * C_WIDTH

REL_BUCKETS = 32
REL_MAX_DIST = 1024
REL_HEADS = 8

X_HEADS = 4
X_HEAD_DIM = 128
X_WIDTH = X_HEADS * X_HEAD_DIM

MOE_GROUPS = 4
MOE_EPG = 4
N_EXPERTS = MOE_GROUPS * MOE_EPG
D_EXPERT = 512
SLOT_A = (0, 0, 0, 1, 1, 3)
SLOT_B = (1, 2, 3, 3, 2, 2)
N_PAIRS = len(SLOT_A)
N_BUCKETS = MOE_GROUPS * N_PAIRS

V7X_LANES = 128
ROW_CHUNKS = D_MODEL // V7X_LANES
STRIDE_STEP = 4
V7X_VMEM_BYTES = 64 * 1024 * 1024
VMEM_LIMIT = 62 * 1024 * 1024

PROJ_TM = 1024
PROJ_TN = 1024
EVEN_TB = 512
EVEN_STACK = 2
EVEN_SUBS = 4
ODD_TILE = PROJ_TM
ODD_BLOCKS = ODD_TILE // C_BLOCK
ODD_INTERLEAVE = 8
COMB_TB = 512
CROSS_TB = 512
MOE_TM = 256
ROUTER_ROWS = 32
MOE_TILES = (SEQ + N_BUCKETS * (MOE_TM - 1)) // MOE_TM
GATHER_TN = 1536
FINAL_TB = 512


def _cparams(*sem):
    return pltpu.CompilerParams(dimension_semantics=sem, vmem_limit_bytes=VMEM_LIMIT)


def _rms(x, g):
    return x * lax.rsqrt(jnp.mean(x * x, axis=-1, keepdims=True) + EPS) * g


def _dot(a, b):
    return jnp.dot(a, b, preferred_element_type=F32)


def _dot_nt(a, b):
    return lax.dot_general(a, b, (((1,), (1,)), ((), ())), preferred_element_type=F32)


def _load_rows(ref, n, lead=()):
    return jnp.concatenate([ref[lead + (pl.ds(c, n, stride=ROW_CHUNKS), slice(None))]
                            for c in range(ROW_CHUNKS)], axis=1)


def _store_rows(ref, val, lead=()):
    n = val.shape[0]
    for c in range(ROW_CHUNKS):
        ref[lead + (pl.ds(c, n, stride=ROW_CHUNKS), slice(None))] = val[:, c * V7X_LANES:(c + 1) * V7X_LANES]


def _row_tile(idx):
    if isinstance(idx, int):
        return pl.ds(idx * ROW_CHUNKS, ROW_CHUNKS)
    return pl.ds(pl.multiple_of(idx * ROW_CHUNKS, ROW_CHUNKS), ROW_CHUNKS)


class _RowGather:
    def __init__(self, idx_ref, src_hbm, buf, sem, tm, both_queues=False):
        self.idx_ref, self.src, self.buf, self.sem, self.tm = idx_ref, src_hbm, buf, sem, tm
        self.both_queues = both_queues

    def start(self, tile, slot, rows):
        for n, r in enumerate(rows):
            pltpu.make_async_copy(self.src.at[_row_tile(self.idx_ref[tile * self.tm + r])],
                                  self.buf.at[slot, _row_tile(r)], self.sem.at[slot]).start(
                                      priority=n % 2 if self.both_queues else 0)

    def wait(self, slot):
        pltpu.make_async_copy(self.src.at[pl.ds(0, self.tm * ROW_CHUNKS)], self.buf.at[slot],
                              self.sem.at[slot]).wait()


def _fill_xn(xf, xn_ref, scratch, dil):
    tm = xn_ref.shape[0]
    seg = tm // dil
    if dil == 1:
        xn_ref[...] = xf.astype(BF16)
        return
    xs_ref, ys_ref = scratch
    for c in range(ROW_CHUNKS):
        xs_ref[c] = xf[:, c * V7X_LANES:(c + 1) * V7X_LANES]
    src = xs_ref
    if dil == STRIDE_STEP * STRIDE_STEP:
        quarter = tm // STRIDE_STEP
        for q in range(STRIDE_STEP):
            for c in range(ROW_CHUNKS):
                ys_ref[c, q * quarter:(q + 1) * quarter, :] = xs_ref[c, pl.ds(q, quarter, stride=STRIDE_STEP), :]
        src = ys_ref
    else:
        assert dil == STRIDE_STEP
    for sl in range(dil):
        start = (sl // STRIDE_STEP) * (tm // STRIDE_STEP) + sl % STRIDE_STEP if dil > STRIDE_STEP else sl
        for c in range(ROW_CHUNKS):
            xn_ref[sl * seg:(sl + 1) * seg, c * V7X_LANES:(c + 1) * V7X_LANES] = (
                src[c, pl.ds(start, seg, stride=STRIDE_STEP), :].astype(BF16))


def _slab_residue(sl, dil):
    if dil == STRIDE_STEP * STRIDE_STEP:
        return (sl % STRIDE_STEP) * STRIDE_STEP + sl // STRIDE_STEP
    return sl


def _proj_out(acc, o_ref, gelu_cols, dil):
    seg = acc.shape[0] // dil
    if gelu_cols:
        o_ref[0, :, :gelu_cols] = jax.nn.gelu(acc[:, :gelu_cols]).astype(o_ref.dtype)
        o_ref[0, :, gelu_cols:] = acc[:, gelu_cols:].astype(o_ref.dtype)
    else:
        for r in range(dil):
            o_ref[r] = acc[r * seg:(r + 1) * seg].astype(o_ref.dtype)


def _proj_kernel(h_ref, g_ref, w_ref, o_ref, xn_ref, *scratch, gelu_cols, dil):
    @pl.when(pl.program_id(1) == 0)
    def _():
        _fill_xn(_rms(h_ref[...], g_ref[...]), xn_ref, scratch, dil)

    _proj_out(_dot(xn_ref[...], w_ref[...].astype(BF16)), o_ref, gelu_cols, dil)


def _gather_proj_kernel(pos_ref, hs_hbm, g_ref, w_ref, o_ref, hnat_ref, xn_ref, hbuf, sem,
                        *, gelu_cols, ni, nj):
    i = pl.program_id(0)
    j = pl.program_id(1)
    tm = xn_ref.shape[0]
    per = tm // nj
    slot = i % 2
    gather = _RowGather(pos_ref, hs_hbm, hbuf, sem, tm)

    @pl.when((i == 0) & (j == 0))
    def _():
        gather.start(0, 0, range(tm))

    @pl.when(j == 0)
    def _():
        gather.wait(slot)
        h = _load_rows(hbuf, tm, (slot,))
        hnat_ref[...] = h
        _fill_xn(_rms(h, g_ref[...]), xn_ref, (), 1)

    nxt = jnp.where(i + 1 < ni, i + 1, 0)
    gather.start(nxt, 1 - slot, [j * per + r for r in range(per)])
    _proj_out(_dot(xn_ref[...], w_ref[...].astype(BF16)), o_ref, gelu_cols, 1)

    @pl.when((i == ni - 1) & (j == nj - 1))
    def _():
        gather.wait(1 - slot)


def _proj(h, g, w, *, rows, tm, tn, n, w_lead=0, w_col0=0, gelu_cols=0, dil=1, pos=None):
    seg = tm // dil
    ni, nj = rows // tm, n // tn
    out_z = jax.ShapeDtypeStruct((dil, rows // dil, n), BF16)
    if pos is None:
        return pl.pallas_call(
            functools.partial(_proj_kernel, gelu_cols=gelu_cols, dil=dil),
            grid=(ni, nj),
            in_specs=[
                pl.BlockSpec((tm, D_MODEL), lambda i, j: (i, 0)),
                pl.BlockSpec((1, D_MODEL), lambda i, j: (0, 0)),
                pl.BlockSpec((None, D_MODEL, tn), lambda i, j: (w_lead, 0, w_col0 + j)),
            ],
            out_specs=pl.BlockSpec((dil, seg, tn), lambda i, j: (0, i, j)),
            out_shape=out_z,
            scratch_shapes=[pltpu.VMEM((tm, D_MODEL), BF16)] + (
                [pltpu.VMEM((ROW_CHUNKS, tm, V7X_LANES), F32)] * 2 if dil > 1 else []),
            compiler_params=_cparams("parallel", "arbitrary"),
            name=f"norm_proj_d{dil}",
        )(h, g.reshape(1, D_MODEL), w)
    assert dil == 1 and tm % nj == 0
    grid_spec = pltpu.PrefetchScalarGridSpec(
        num_scalar_prefetch=1,
        grid=(ni, nj),
        in_specs=[
            pl.BlockSpec(memory_space=pl.ANY),
            pl.BlockSpec((1, D_MODEL), lambda i, j, p: (0, 0)),
            pl.BlockSpec((None, D_MODEL, tn), lambda i, j, p: (w_lead, 0, w_col0 + j)),
        ],
        out_specs=[pl.BlockSpec((1, tm, tn), lambda i, j, p: (0, i, j)),
                   pl.BlockSpec((tm, D_MODEL), lambda i, j, p: (i, 0))],
        scratch_shapes=[pltpu.VMEM((tm, D_MODEL), BF16),
                        pltpu.VMEM((2, tm * ROW_CHUNKS, V7X_LANES), F32),
                        pltpu.SemaphoreType.DMA((2,))],
    )
    return pl.pallas_call(
        functools.partial(_gather_proj_kernel, gelu_cols=gelu_cols, ni=ni, nj=nj),
        grid_spec=grid_spec,
        out_shape=[out_z, jax.ShapeDtypeStruct((rows, D_MODEL), F32)],
        compiler_params=_cparams("arbitrary", "arbitrary"),
        name="gather_norm_proj",
    )(pos, h, g.reshape(1, D_MODEL), w)


def _t5_bucket_np(rel):
    nb = REL_BUCKETS // 2
    max_exact = nb // 2
    ret = np.where(rel > 0, nb, 0)
    n = np.abs(rel)
    nf = np.maximum(n, 1).astype(np.float32)
    large = max_exact + (np.log(nf / np.float32(max_exact)) / np.float32(math.log(REL_MAX_DIST / max_exact))
                         * np.float32(nb - max_exact)).astype(np.int32)
    large = np.minimum(large, nb - 1)
    return (ret + np.where(n < max_exact, n, large)).astype(np.int32)


def _bias_kernel(table_ref, idx_ref, mask_ref, o_ref, *, block):
    idx = idx_ref[...]
    for h in range(REL_HEADS):
        acc = jnp.zeros(idx.shape, F32)
        for b in range(REL_BUCKETS):
            acc = jnp.where(idx == b, table_ref[b, h], acc)
        for v in range(3):
            o_ref[v, h * block:(h + 1) * block, :] = acc + mask_ref[v]


def _band_bias(table, block, half, dil):
    rel = np.arange(3 * block)[None, :] - block - np.arange(block)[:, None]
    band = np.abs(rel) <= half
    col = np.arange(3 * block)[None, :]
    masks = np.stack([band & (col >= block), band, band & (col < 2 * block)])
    add = np.where(masks, 0.0, NEG_INF).astype(np.float32)
    return pl.pallas_call(
        functools.partial(_bias_kernel, block=block),
        in_specs=[pl.BlockSpec(memory_space=pltpu.SMEM),
                  pl.BlockSpec(memory_space=pltpu.VMEM),
                  pl.BlockSpec(memory_space=pltpu.VMEM)],
        out_specs=pl.BlockSpec(memory_space=pltpu.VMEM),
        out_shape=jax.ShapeDtypeStruct((3, REL_HEADS * block, 3 * block), F32),
        name=f"rel_bias_d{dil}",
    )(table, jnp.asarray(_t5_bucket_np(rel * dil)), jnp.asarray(add))


def _even_mix_kernel(z_ref, kvp_ref, kvn_ref, h_ref, lng_ref, lnb_ref, ws_ref, bs_ref, bias_ref,
                     sink_ref, wout_ref, o_ref, kv_scr, y_scr):
    i = pl.program_id(0)
    nsub = EVEN_TB // B_BLOCK
    nblk = SEQ // B_BLOCK
    kv_scr[0:B_BLOCK] = kvp_ref[...]
    kv_scr[B_BLOCK:B_BLOCK + EVEN_TB] = z_ref[0, :, EVEN_K0:EVEN_IN]
    kv_scr[B_BLOCK + EVEN_TB:] = kvn_ref[...]
    lng = lng_ref[...]
    lnb = lnb_ref[...]
    for s0 in range(0, nsub, EVEN_SUBS):
        subs = []
        for s in range(s0, s0 + EVEN_SUBS):
            r0 = s * B_BLOCK
            gb = i * nsub + s
            subs.append((r0, jnp.where(gb == 0, 0, jnp.where(gb == nblk - 1, 2, 1))))
        vns = []
        for r0, _ in subs:
            va = z_ref[0, r0:r0 + A_CHUNK, A_WIDTH:2 * A_WIDTH].astype(F32)
            mu = jnp.mean(va, axis=-1, keepdims=True)
            vc = va - mu
            var = jnp.mean(vc * vc, axis=-1, keepdims=True)
            vns.append((vc * lax.rsqrt(var + EPS) * lng + lnb).astype(BF16))
        mixes = [[_dot(ws_ref[g], vn[:, g * A_CH:(g + 1) * A_CH]) for g in range(A_GROUPS)] for vn in vns]
        for (r0, _), mixed in zip(subs, mixes):
            for g in range(A_GROUPS):
                c0 = g * A_CH
                u = z_ref[0, r0:r0 + A_CHUNK, c0:c0 + A_CH].astype(F32)
                y_scr[r0:r0 + A_CHUNK, c0:c0 + A_CH] = (u * (mixed[g] + bs_ref[g])).astype(BF16)
        units = [(r0, sel, kh, kh * B_Q_PER_KV + half * EVEN_STACK)
                 for r0, sel in subs
                 for kh in range(B_KV_HEADS) for half in range(B_Q_PER_KV // EVEN_STACK)]
        lgs = []
        for r0, _, kh, hd0 in units:
            kw = kv_scr[r0:r0 + 3 * B_BLOCK, kh * B_HEAD_DIM:(kh + 1) * B_HEAD_DIM]
            q = jnp.concatenate(
                [z_ref[0, r0:r0 + B_BLOCK,
                       EVEN_Q0 + (hd0 + g) * B_HEAD_DIM:EVEN_Q0 + (hd0 + g + 1) * B_HEAD_DIM]
                 for g in range(EVEN_STACK)], axis=0)
            lgs.append(_dot_nt(q, kw))
        lgs = [lg * (B_HEAD_DIM ** -0.5) + bias_ref[sel, hd0 * B_BLOCK:(hd0 + EVEN_STACK) * B_BLOCK, :]
               for lg, (_, sel, _, hd0) in zip(lgs, units)]
        sks = [sink_ref[hd0 * B_BLOCK:(hd0 + EVEN_STACK) * B_BLOCK, :] for _, _, _, hd0 in units]
        ms = [jnp.maximum(jnp.max(lg, axis=-1, keepdims=True), sk) for lg, sk in zip(lgs, sks)]
        ps = [jnp.exp(lg - m) for lg, m in zip(lgs, ms)]
        dens = [jnp.sum(p, axis=-1, keepdims=True) + jnp.exp(sk - m) for p, sk, m in zip(ps, sks, ms)]
        os_ = []
        for p, (r0, _, kh, _) in zip(ps, units):
            vw = kv_scr[r0:r0 + 3 * B_BLOCK,
                        B_KV_WIDTH + kh * B_HEAD_DIM:B_KV_WIDTH + (kh + 1) * B_HEAD_DIM]
            os_.append(_dot(p.astype(BF16), vw))
        for o, den, (r0, _, _, hd0) in zip(os_, dens, units):
            o = o * (1.0 / den)
            for g in range(EVEN_STACK):
                c0 = A_WIDTH + (hd0 + g) * B_HEAD_DIM
                y_scr[r0:r0 + B_BLOCK, c0:c0 + B_HEAD_DIM] = o[g * B_BLOCK:(g + 1) * B_BLOCK].astype(BF16)
    o_ref[...] = h_ref[...] + _dot(y_scr[...], wout_ref[...])


def _even_mix(z, h, ln_g, ln_b, w_s, b_s, bias, sink, w_out):
    nsub = EVEN_TB // B_BLOCK
    nblk = SEQ // B_BLOCK
    kv_cb = EVEN_K0 // (2 * B_KV_WIDTH)
    sink_col = jnp.broadcast_to(sink.reshape(B_HEADS, 1, 1), (B_HEADS, B_BLOCK, 1)).reshape(
        B_HEADS * B_BLOCK, 1)
    return pl.pallas_call(
        _even_mix_kernel,
        grid=(SEQ // EVEN_TB,),
        in_specs=[
            pl.BlockSpec((1, EVEN_TB, EVEN_IN), lambda i: (0, i, 0)),
            pl.BlockSpec((None, B_BLOCK, 2 * B_KV_WIDTH),
                         lambda i: (0, jnp.maximum(i * nsub - 1, 0), kv_cb)),
            pl.BlockSpec((None, B_BLOCK, 2 * B_KV_WIDTH),
                         lambda i: (0, jnp.minimum((i + 1) * nsub, nblk - 1), kv_cb)),
            pl.BlockSpec((EVEN_TB, D_MODEL), lambda i: (i, 0)),
            pl.BlockSpec((1, A_WIDTH), lambda i: (0, 0)),
            pl.BlockSpec((1, A_WIDTH), lambda i: (0, 0)),
            pl.BlockSpec((A_GROUPS, A_CHUNK, A_CHUNK), lambda i: (0, 0, 0)),
            pl.BlockSpec((A_GROUPS, A_CHUNK, A_CH), lambda i: (0, 0, 0)),
            pl.BlockSpec((3, B_HEADS * B_BLOCK, 3 * B_BLOCK), lambda i: (0, 0, 0)),
            pl.BlockSpec((B_HEADS * B_BLOCK, 1), lambda i: (0, 0)),
            pl.BlockSpec((A_WIDTH + B_WIDTH, D_MODEL), lambda i: (0, 0)),
        ],
        out_specs=pl.BlockSpec((EVEN_TB, D_MODEL), lambda i: (i, 0)),
        out_shape=jax.ShapeDtypeStruct((SEQ, D_MODEL), F32),
        scratch_shapes=[
            pltpu.VMEM((EVEN_TB + 2 * B_BLOCK, 2 * B_KV_WIDTH), BF16),
            pltpu.VMEM((EVEN_TB, A_WIDTH + B_WIDTH), BF16),
        ],
        compiler_params=_cparams("parallel"),
        name="even_mixer",
    )(z, z, z, h, ln_g.reshape(1, A_WIDTH), ln_b.reshape(1, A_WIDTH), w_s.astype(BF16),
      jnp.broadcast_to(b_s[:, :, None], (A_GROUPS, A_CHUNK, A_CH)), bias, sink_col, w_out)


def _dil_attn_kernel(zc_ref, kp_ref, kn_ref, vp_ref, vn_ref, bias_ref, o_ref, lse_ref,
                     k_scr, v_scr, o_scr, lse_scr, *, dil):
    t = pl.program_id(0)
    seg = ODD_TILE // dil
    nsb = seg // C_BLOCK
    nblk = SEQ // dil // C_BLOCK
    k_scr[:, 0:C_BLOCK] = kp_ref[...]
    k_scr[:, C_BLOCK:C_BLOCK + seg] = zc_ref[:, :, C_WIDTH:2 * C_WIDTH]
    k_scr[:, C_BLOCK + seg:] = kn_ref[...]
    v_scr[:, 0:C_BLOCK] = vp_ref[...]
    v_scr[:, C_BLOCK:C_BLOCK + seg] = zc_ref[:, :, 2 * C_WIDTH:3 * C_WIDTH]
    v_scr[:, C_BLOCK + seg:] = vn_ref[...]
    lane = lax.broadcasted_iota(jnp.int32, (C_BLOCK, V7X_LANES), 1)

    def body(it, carry):
        blocks = []
        for u in range(ODD_INTERLEAVE):
            n = it * ODD_INTERLEAVE + u
            r = n // nsb
            s = n % nsb
            r0 = pl.multiple_of(s * C_BLOCK, C_BLOCK)
            gb = t * nsb + s
            sel = jnp.where(gb == 0, 0, jnp.where(gb == nblk - 1, 2, 1))
            rows = (pl.ds(s * (C_BLOCK * dil) + _slab_residue(r, dil), C_BLOCK, stride=dil) if dil > 1
                    else pl.ds(r0, C_BLOCK))
            blocks.append((r, r0, sel, rows))
        lgs = []
        for r, r0, _, _ in blocks:
            for hd in range(C_HEADS):
                c0 = hd * C_HEAD_DIM
                q = zc_ref[r, pl.ds(r0, C_BLOCK), c0:c0 + C_HEAD_DIM]
                kw = k_scr[r, pl.ds(r0, 3 * C_BLOCK), c0:c0 + C_HEAD_DIM]
                lgs.append(_dot_nt(q, kw))
        lg = [jnp.concatenate(lgs[u * C_HEADS:(u + 1) * C_HEADS], axis=0) * (C_HEAD_DIM ** -0.5)
              + bias_ref[blk[2]] for u, blk in enumerate(blocks)]
        m = [jnp.max(x, axis=-1, keepdims=True) for x in lg]
        p = [jnp.exp(x - mm) for x, mm in zip(lg, m)]
        den = [jnp.sum(x, axis=-1, keepdims=True) for x in p]
        inv = [1.0 / d for d in den]
        lse = [mm + jnp.log(d) for mm, d in zip(m, den)]
        pb = [x.astype(BF16) for x in p]
        outs = []
        for u, (r, r0, _, _) in enumerate(blocks):
            for hd in range(C_HEADS):
                c0 = hd * C_HEAD_DIM
                vw = v_scr[r, pl.ds(r0, 3 * C_BLOCK), c0:c0 + C_HEAD_DIM]
                outs.append(_dot(pb[u][hd * C_BLOCK:(hd + 1) * C_BLOCK], vw))
        for u, (_, _, _, rows) in enumerate(blocks):
            lse_tile = jnp.zeros((C_BLOCK, V7X_LANES), F32)
            for hd in range(C_HEADS):
                o_scr[hd, rows, :] = outs[u * C_HEADS + hd] * inv[u][hd * C_BLOCK:(hd + 1) * C_BLOCK]
                lse_tile = jnp.where(lane == hd, lse[u][hd * C_BLOCK:(hd + 1) * C_BLOCK], lse_tile)
            lse_scr[rows, :] = lse_tile
        return carry

    lax.fori_loop(0, ODD_BLOCKS // ODD_INTERLEAVE, body, 0)
    for hd in range(C_HEADS):
        o_ref[:, hd * C_HEAD_DIM:(hd + 1) * C_HEAD_DIM] = o_scr[hd].astype(o_ref.dtype)
    lse_ref[...] = lse_scr[...]


def _dil_attn(zg, bias, dil):
    seg = ODD_TILE // dil
    nsb = seg // C_BLOCK
    last = SEQ // dil // C_BLOCK - 1

    def halo(j, nxt):
        if nxt:
            return pl.BlockSpec((dil, C_BLOCK, C_WIDTH),
                                lambda t: (0, jnp.minimum((t + 1) * nsb, last), j))
        return pl.BlockSpec((dil, C_BLOCK, C_WIDTH), lambda t: (0, jnp.maximum(t * nsb - 1, 0), j))

    return pl.pallas_call(
        functools.partial(_dil_attn_kernel, dil=dil),
        grid=(SEQ // ODD_TILE,),
        in_specs=[pl.BlockSpec((dil, seg, 3 * C_WIDTH), lambda t: (0, t, 0)),
                  halo(1, False), halo(1, True), halo(2, False), halo(2, True),
                  pl.BlockSpec((3, C_HEADS * C_BLOCK, 3 * C_BLOCK), lambda t: (0, 0, 0))],
        out_specs=[pl.BlockSpec((ODD_TILE, C_WIDTH), lambda t: (t, 0)),
                   pl.BlockSpec((ODD_TILE, V7X_LANES), lambda t: (t, 0))],
        out_shape=[jax.ShapeDtypeStruct((SEQ, C_WIDTH), BF16),
                   jax.ShapeDtypeStruct((SEQ, V7X_LANES), F32)],
        scratch_shapes=[pltpu.VMEM((dil, seg + 2 * C_BLOCK, C_WIDTH), BF16),
                        pltpu.VMEM((dil, seg + 2 * C_BLOCK, C_WIDTH), BF16),
                        pltpu.VMEM((C_HEADS, ODD_TILE, C_HEAD_DIM), F32),
                        pltpu.VMEM((ODD_TILE, V7X_LANES), F32)],
        compiler_params=_cparams("parallel"),
        name=f"dilated_attn_d{dil}",
    )(zg, zg, zg, zg, zg, bias)


def _combine_kernel(o0_ref, o1_ref, o2_ref, l0_ref, l1_ref, l2_ref, h_ref, wout_ref, out_ref, y_scr):
    l0 = l0_ref[...]
    l1 = l1_ref[...]
    l2 = l2_ref[...]
    m = jnp.maximum(jnp.maximum(l0, l1), l2)
    e0 = jnp.exp(l0 - m)
    e1 = jnp.exp(l1 - m)
    e2 = jnp.exp(l2 - m)
    tot = e0 + e1 + e2
    w0 = e0 / tot
    w1 = e1 / tot
    w2 = e2 / tot
    for hd in range(C_HEADS):
        c0 = hd * C_HEAD_DIM
        y = (w0[:, hd:hd + 1] * o0_ref[:, c0:c0 + C_HEAD_DIM].astype(F32)
             + w1[:, hd:hd + 1] * o1_ref[:, c0:c0 + C_HEAD_DIM].astype(F32)
             + w2[:, hd:hd + 1] * o2_ref[:, c0:c0 + C_HEAD_DIM].astype(F32))
        y_scr[:, c0:c0 + C_HEAD_DIM] = y.astype(BF16)
    out_ref[...] = h_ref[...] + _dot(y_scr[...], wout_ref[...])


def _combine(outs, lses, h, w_out):
    blk_o = pl.BlockSpec((COMB_TB, C_WIDTH), lambda i: (i, 0))
    blk_l = pl.BlockSpec((COMB_TB, V7X_LANES), lambda i: (i, 0))
    return pl.pallas_call(
        _combine_kernel,
        grid=(SEQ // COMB_TB,),
        in_specs=[blk_o, blk_o, blk_o, blk_l, blk_l, blk_l,
                  pl.BlockSpec((COMB_TB, D_MODEL), lambda i: (i, 0)),
                  pl.BlockSpec((C_WIDTH, D_MODEL), lambda i: (0, 0))],
        out_specs=pl.BlockSpec((COMB_TB, D_MODEL), lambda i: (i, 0)),
        out_shape=jax.ShapeDtypeStruct((SEQ, D_MODEL), F32),
        scratch_shapes=[pltpu.VMEM((COMB_TB, C_WIDTH), BF16)],
        compiler_params=_cparams("parallel"),
        name="group_combine_proj",
    )(*outs, *lses, h, w_out)


def _cross_kernel(h_ref, gx_ref, wq_ref, kv_ref, wo_ref, gf_ref, wr_ref, br_ref,
                  hx_ref, meta_ref, o_scr):
    h = h_ref[...]
    q = _dot(_rms(h, gx_ref[...]).astype(BF16), wq_ref[...]).astype(BF16)
    cols = [hd * X_HEAD_DIM for hd in range(X_HEADS)]
    lgs = [_dot_nt(q[:, c0:c0 + X_HEAD_DIM], kv_ref[0, :, c0:c0 + X_HEAD_DIM]) * (X_HEAD_DIM ** -0.5)
           for c0 in cols]
    ms = [jnp.max(lg, axis=-1, keepdims=True) for lg in lgs]
    ps = [jnp.exp(lg - m) for lg, m in zip(lgs, ms)]
    dens = [jnp.sum(p, axis=-1, keepdims=True) for p in ps]
    os_ = [_dot(p.astype(BF16), kv_ref[0, :, X_WIDTH + c0:X_WIDTH + c0 + X_HEAD_DIM])
           for p, c0 in zip(ps, cols)]
    for o, den, c0 in zip(os_, dens, cols):
        o_scr[:, c0:c0 + X_HEAD_DIM] = (o / den).astype(BF16)
    h2 = h + _dot(o_scr[...], wo_ref[...])
    _store_rows(hx_ref, h2)

    t = _rms(h2, gf_ref[...])
    t_hi = t.astype(BF16)
    t_lo = (t - t_hi.astype(F32)).astype(BF16)
    lt = (_dot_nt(wr_ref[0], t_hi) + _dot_nt(wr_ref[0], t_lo) + _dot_nt(wr_ref[1], t_hi)) + br_ref[...]
    g = [lt[k:k + 1, :] for k in range(MOE_GROUPS)]
    gmax = jnp.maximum(jnp.maximum(g[0], g[1]), jnp.maximum(g[2], g[3]))
    grp = jnp.where(g[0] == gmax, 0, jnp.where(g[1] == gmax, 1, jnp.where(g[2] == gmax, 2, 3)))
    g_gate = 1.0 / (jnp.exp(g[0] - gmax) + jnp.exp(g[1] - gmax) + jnp.exp(g[2] - gmax)
                    + jnp.exp(g[3] - gmax))
    e = []
    for k in range(MOE_EPG):
        rows = [lt[MOE_GROUPS + gi * MOE_EPG + k:MOE_GROUPS + gi * MOE_EPG + k + 1, :]
                for gi in range(MOE_GROUPS)]
        e.append(jnp.where(grp == 0, rows[0], jnp.where(grp == 1, rows[1],
                                                         jnp.where(grp == 2, rows[2], rows[3]))))
    v1 = jnp.maximum(jnp.maximum(e[0], e[1]), jnp.maximum(e[2], e[3]))
    i1 = jnp.where(e[0] == v1, 0, jnp.where(e[1] == v1, 1, jnp.where(e[2] == v1, 2, 3)))
    r = [jnp.where(i1 == k, -jnp.inf, e[k]) for k in range(MOE_EPG)]
    v2 = jnp.maximum(jnp.maximum(r[0], r[1]), jnp.maximum(r[2], r[3]))
    i2 = jnp.where(r[0] == v2, 0, jnp.where(r[1] == v2, 1, jnp.where(r[2] == v2, 2, 3)))
    d = jnp.exp(v2 - v1)
    w1 = g_gate / (1.0 + d)
    w2 = g_gate * d / (1.0 + d)
    first_lo = i1 < i2
    lo = jnp.where(first_lo, i1, i2)
    hi = jnp.where(first_lo, i2, i1)
    w_lo = jnp.where(first_lo, w1, w2)
    w_hi = jnp.where(first_lo, w2, w1)
    pair = jnp.where(lo == 0, hi - 1, jnp.where(lo == 1, jnp.where(hi == 3, 3, 4), 5))
    w_a = jnp.where(lo == 2, w_hi, w_lo)
    w_b = jnp.where(lo == 2, w_lo, w_hi)
    bucket = (grp * N_PAIRS + pair).astype(F32)
    row = lax.broadcasted_iota(jnp.int32, (8, CROSS_TB), 0)
    meta_ref[...] = jnp.where(row == 0, bucket, jnp.where(row == 1, w_a, jnp.where(row == 2, w_b, 0.0)))


def _cross_router(h, g_cross, wq, kv, wo, g_ffn, wr_t, br):
    full = lambda shape: pl.BlockSpec(shape, lambda i: tuple(0 for _ in shape))
    return pl.pallas_call(
        _cross_kernel,
        grid=(SEQ // CROSS_TB,),
        in_specs=[
            pl.BlockSpec((CROSS_TB, D_MODEL), lambda i: (i, 0)),
            full((1, D_MODEL)),
            full((D_MODEL, X_WIDTH)),
            full((1, MEM_LEN, 2 * X_WIDTH)),
            full((X_WIDTH, D_MODEL)),
            full((1, D_MODEL)),
            full((2, ROUTER_ROWS, D_MODEL)),
            full((ROUTER_ROWS, 1)),
        ],
        out_specs=[pl.BlockSpec((CROSS_TB * ROW_CHUNKS, V7X_LANES), lambda i: (i, 0)),
                   pl.BlockSpec((8, CROSS_TB), lambda i: (0, i))],
        out_shape=[jax.ShapeDtypeStruct((SEQ * ROW_CHUNKS, V7X_LANES), F32),
                   jax.ShapeDtypeStruct((8, SEQ), F32)],
        scratch_shapes=[pltpu.VMEM((CROSS_TB, X_WIDTH), BF16)],
        compiler_params=_cparams("parallel"),
        name="cross_attn_router",
    )(h, g_cross.reshape(1, D_MODEL), wq, kv, wo, g_ffn.reshape(1, D_MODEL), wr_t, br)


def _moe_kernel(src_ref, ea_ref, eb_ref, nused_ref,
                hx_hbm, gates_ref, gf_ref, wga_ref, wua_ref, wda_ref, wgb_ref, wub_ref, wdb_ref,
                out_ref, xbuf, wup_a, wdn_a, wup_b, wdn_b, gsem):
    k = pl.program_id(0)
    nused = nused_ref[0]
    xslot = k % 3
    gather = _RowGather(src_ref, hx_hbm, xbuf, gsem, MOE_TM, both_queues=True)

    @pl.when(k == 0)
    def _():
        gather.start(0, 0, range(MOE_TM))
        gather.start(1, 1, range(MOE_TM))

    prev = jnp.maximum(k - 1, 0)

    @pl.when((k < nused) & ((k == 0) | (ea_ref[k] != ea_ref[prev])))
    def _():
        wup_a[:, :D_EXPERT] = wga_ref[0].astype(BF16)
        wup_a[:, D_EXPERT:] = wua_ref[0].astype(BF16)
        wdn_a[...] = wda_ref[0].astype(BF16)

    @pl.when((k < nused) & ((k == 0) | (eb_ref[k] != eb_ref[prev])))
    def _():
        wup_b[:, :D_EXPERT] = wgb_ref[0].astype(BF16)
        wup_b[:, D_EXPERT:] = wub_ref[0].astype(BF16)
        wdn_b[...] = wdb_ref[0].astype(BF16)

    @pl.when(k < nused)
    def _():
        gather.wait(xslot)
        h2 = _load_rows(xbuf, MOE_TM, (xslot,))
        gather.start(k + 2, (k + 2) % 3, range(MOE_TM))
        t = _rms(h2, gf_ref[...]).astype(BF16)
        gus = [_dot(t, wup[...]) for wup in (wup_a, wup_b)]
        hids = [(jax.nn.silu(gu[:, :D_EXPERT]) * gu[:, D_EXPERT:] * gates_ref[:, col:col + 1]).astype(BF16)
                for col, gu in enumerate(gus)]
        ys = [_dot(hid, wdn[...]) for hid, wdn in zip(hids, (wdn_a, wdn_b))]
        _store_rows(out_ref, h2 + (ys[0] + ys[1]))

    @pl.when(k >= nused)
    def _():
        out_ref[...] = jnp.zeros(out_ref.shape, F32)

    @pl.when(k == nused - 1)
    def _():
        gather.wait((k + 1) % 3)
        gather.wait((k + 2) % 3)


def _moe(hx, gates, g_ffn, w_gate, w_up, w_down, src, ea, eb, nused):
    def wspec(shape, which):
        if which == 0:
            return pl.BlockSpec((1,) + shape, lambda k, s, a, b, n: (a[k], 0, 0))
        return pl.BlockSpec((1,) + shape, lambda k, s, a, b, n: (b[k], 0, 0))

    up_shape = (D_MODEL, D_EXPERT)
    down_shape = (D_EXPERT, D_MODEL)
    grid_spec = pltpu.PrefetchScalarGridSpec(
        num_scalar_prefetch=4,
        grid=(MOE_TILES,),
        in_specs=[
            pl.BlockSpec(memory_space=pl.ANY),
            pl.BlockSpec((MOE_TM, V7X_LANES), lambda k, s, a, b, n: (k, 0)),
            pl.BlockSpec((1, D_MODEL), lambda k, s, a, b, n: (0, 0)),
            wspec(up_shape, 0), wspec(up_shape, 0), wspec(down_shape, 0),
            wspec(up_shape, 1), wspec(up_shape, 1), wspec(down_shape, 1),
        ],
        out_specs=pl.BlockSpec((MOE_TM * ROW_CHUNKS, V7X_LANES), lambda k, s, a, b, n: (k, 0)),
        scratch_shapes=[
            pltpu.VMEM((3, MOE_TM * ROW_CHUNKS, V7X_LANES), F32),
            pltpu.VMEM((D_MODEL, 2 * D_EXPERT), BF16),
            pltpu.VMEM((D_EXPERT, D_MODEL), BF16),
            pltpu.VMEM((D_MODEL, 2 * D_EXPERT), BF16),
            pltpu.VMEM((D_EXPERT, D_MODEL), BF16),
            pltpu.SemaphoreType.DMA((3,)),
        ],
    )
    return pl.pallas_call(
        _moe_kernel,
        grid_spec=grid_spec,
        out_shape=jax.ShapeDtypeStruct((MOE_TILES * MOE_TM * ROW_CHUNKS, V7X_LANES), F32),
        compiler_params=_cparams("arbitrary"),
        name="routed_moe",
    )(src, ea, eb, nused, hx, gates, g_ffn.reshape(1, D_MODEL),
      w_gate, w_up, w_down, w_gate, w_up, w_down)


def _route_tables(meta):
    bucket = meta[0].astype(jnp.int32)
    ids = jnp.arange(N_BUCKETS, dtype=jnp.int32)
    counts = jnp.sum((bucket[:, None] == ids[None, :]).astype(jnp.int32), axis=0)
    ntile = (counts + MOE_TM - 1) // MOE_TM
    pad = ntile * MOE_TM - counts
    tile_end = jnp.cumsum(ntile)
    nused = tile_end[-1]
    dummy_key = jnp.where(jnp.arange(MOE_TM - 1, dtype=jnp.int32)[None, :] < pad[:, None],
                          ids[:, None], N_BUCKETS)
    keys = jnp.concatenate([bucket, dummy_key.reshape(-1)])
    vals = jnp.concatenate([jnp.arange(SEQ, dtype=jnp.int32),
                            jnp.full((N_BUCKETS * (MOE_TM - 1),), SEQ, jnp.int32)])
    nslot = MOE_TILES * MOE_TM
    zpad = jnp.zeros((N_BUCKETS * (MOE_TM - 1),), F32)
    _, tok, ga, gb = lax.sort((keys, vals, jnp.concatenate([meta[1], zpad]), jnp.concatenate([meta[2], zpad])),
                              num_keys=1, is_stable=True)
    tok = tok[:nslot]
    gates = jnp.pad(jnp.stack([ga[:nslot], gb[:nslot]], axis=1), ((0, 0), (0, V7X_LANES - 2)))
    valid = tok < SEQ
    src = jnp.concatenate([jnp.where(valid, tok, 0), jnp.zeros((2 * MOE_TM,), jnp.int32)])
    _, pos = lax.sort((tok, jnp.arange(nslot, dtype=jnp.int32)), num_keys=1, is_stable=True)
    pos = pos[:SEQ]
    tiles = jnp.arange(MOE_TILES, dtype=jnp.int32)
    tile_bucket = jnp.minimum(jnp.sum((tiles[:, None] >= tile_end[None, :]).astype(jnp.int32), axis=1),
                              N_BUCKETS - 1)
    onehot = (tile_bucket[:, None] == ids[None, :]).astype(jnp.int32)
    base = (np.arange(N_BUCKETS) // N_PAIRS) * MOE_EPG
    ea = jnp.sum(onehot * jnp.asarray(base + np.asarray(SLOT_A)[np.arange(N_BUCKETS) % N_PAIRS],
                                      jnp.int32)[None, :], axis=1)
    eb = jnp.sum(onehot * jnp.asarray(base + np.asarray(SLOT_B)[np.arange(N_BUCKETS) % N_PAIRS],
                                      jnp.int32)[None, :], axis=1)
    return (src.astype(jnp.int32), pos.astype(jnp.int32), ea.astype(jnp.int32), eb.astype(jnp.int32),
            nused.reshape(1).astype(jnp.int32), gates)


def _final_norm_kernel(pos_ref, hs_hbm, g_ref, o_ref, hbuf, sem, *, ni):
    i = pl.program_id(0)
    tm = o_ref.shape[0]
    slot = i % 2
    gather = _RowGather(pos_ref, hs_hbm, hbuf, sem, tm, both_queues=True)

    @pl.when(i == 0)
    def _():
        gather.start(0, 0, range(tm))

    @pl.when(i + 1 < ni)
    def _():
        gather.start(i + 1, 1 - slot, range(tm))

    gather.wait(slot)
    o_ref[...] = _rms(_load_rows(hbuf, tm, (slot,)), g_ref[...])


def _final_norm(hs, pos, g):
    tb = FINAL_TB
    ni = SEQ // tb
    grid_spec = pltpu.PrefetchScalarGridSpec(
        num_scalar_prefetch=1,
        grid=(ni,),
        in_specs=[pl.BlockSpec(memory_space=pl.ANY),
                  pl.BlockSpec((1, D_MODEL), lambda i, p: (0, 0))],
        out_specs=pl.BlockSpec((tb, D_MODEL), lambda i, p: (i, 0)),
        scratch_shapes=[pltpu.VMEM((2, tb * ROW_CHUNKS, V7X_LANES), F32),
                        pltpu.SemaphoreType.DMA((2,))],
    )
    return pl.pallas_call(
        functools.partial(_final_norm_kernel, ni=ni),
        grid_spec=grid_spec,
        out_shape=jax.ShapeDtypeStruct((SEQ, D_MODEL), F32),
        compiler_params=_cparams("arbitrary"),
        name="final_norm",
    )(pos, hs, g.reshape(1, D_MODEL))


def kernel(x, mem, ln_mix, ln_cross, ln_mem, ln_ffn, ln_final, rel_table, even_w_in, even_w_out,
           sgu_ln_g, sgu_ln_b, sgu_w, sgu_b, attn_sink, odd_w_in, odd_w_out, xq_w, xkv_w, xo_w,
           router_group_w, router_group_b, router_expert_w, router_expert_b,
           expert_w_gate, expert_w_up, expert_w_down):
    h = x.reshape(SEQ, D_MODEL)
    mem2 = mem.reshape(MEM_LEN, D_MODEL)
    bias_even = _band_bias(rel_table, B_BLOCK, B_HALF_WINDOW, 1)
    bias_odd = [_band_bias(rel_table, C_BLOCK, window // 2 // dil, dil) for window, dil in C_PAIRS]

    pos = None
    for layer in range(DEPTH):
        i = layer // 2
        if layer % 2 == 0:
            z = _proj(h, ln_mix[layer], even_w_in, w_lead=i, rows=SEQ, tm=EVEN_TB, tn=EVEN_IN, n=EVEN_IN,
                      gelu_cols=2 * A_WIDTH, pos=pos)
            if pos is not None:
                z, h = z
            h = _even_mix(z, h, sgu_ln_g[i], sgu_ln_b[i], sgu_w[i], sgu_b[i], bias_even,
                          attn_sink[i], even_w_out[i].astype(BF16))
        else:
            zg, h = _proj(h, ln_mix[layer], odd_w_in, w_lead=i, rows=SEQ, tm=PROJ_TM, tn=GATHER_TN,
                          n=3 * C_WIDTH, pos=pos)
            outs, lses = [], []
            for gi, (_, dil) in enumerate(C_PAIRS):
                if gi > 0:
                    zg = _proj(h, ln_mix[layer], odd_w_in, w_lead=i, w_col0=gi * 3 * C_WIDTH // PROJ_TN,
                               rows=SEQ, tm=PROJ_TM, tn=PROJ_TN, n=3 * C_WIDTH, dil=dil)
                o, lse = _dil_attn(zg, bias_odd[gi], dil)
                outs.append(o)
                lses.append(lse)
            h = _combine(outs, lses, h, odd_w_out[i].astype(BF16))

        kv = _proj(mem2, ln_mem[layer], xkv_w, w_lead=layer, rows=MEM_LEN, tm=MEM_LEN,
                   tn=2 * X_WIDTH, n=2 * X_WIDTH)
        wr_t = jnp.zeros((ROUTER_ROWS, D_MODEL), F32)
        wr_t = wr_t.at[:MOE_GROUPS].set(router_group_w[layer].T)
        wr_t = wr_t.at[MOE_GROUPS:MOE_GROUPS + N_EXPERTS].set(
            router_expert_w[layer].reshape(D_MODEL, N_EXPERTS).T)
        br = jnp.zeros((ROUTER_ROWS, 1), F32)
        br = br.at[:MOE_GROUPS, 0].set(router_group_b[layer])
        br = br.at[MOE_GROUPS:MOE_GROUPS + N_EXPERTS, 0].set(router_expert_b[layer].reshape(N_EXPERTS))
        wr_hi = wr_t.astype(BF16)
        wr_split = jnp.stack([wr_hi, (wr_t - wr_hi.astype(F32)).astype(BF16)])
        hx, meta = _cross_router(h, ln_cross[layer], xq_w[layer].astype(BF16), kv,
                                 xo_w[layer].astype(BF16), ln_ffn[layer], wr_split, br)

        src, pos, ea, eb, nused, gates = _route_tables(meta)
        h = _moe(hx, gates, ln_ffn[layer],
                 expert_w_gate.reshape(DEPTH * N_EXPERTS, D_MODEL, D_EXPERT),
                 expert_w_up.reshape(DEPTH * N_EXPERTS, D_MODEL, D_EXPERT),
                 expert_w_down.reshape(DEPTH * N_EXPERTS, D_EXPERT, D_MODEL),
                 src, ea + layer * N_EXPERTS, eb + layer * N_EXPERTS, nused)

    return _final_norm(h, pos, ln_final).reshape(1, SEQ, D_MODEL)
```
